```python
import math
import jax, jax.numpy as jnp
from jax import lax
import numpy as np

D_MODEL = 1024
BATCH = 8
SEQ = 4096
DEPTH = 1

HEAD_DIM = 64
ATTN_GROUPS = ((128, 1), (512, 4), (2048, 16))
ATTN_HEADS_PER_GROUP = 4
ATTN_HEADS = ATTN_HEADS_PER_GROUP * len(ATTN_GROUPS)
ATTN_WIDTH = ATTN_HEADS * HEAD_DIM
BAND_BLOCK = 128
RET_HEADS = 6
RET_QK_DIM = 64
RET_V_DIM = 128
RET_QK_WIDTH = RET_HEADS * RET_QK_DIM
RET_V_WIDTH = RET_HEADS * RET_V_DIM
RET_CHUNK = 128
MEM_LEN = 256
MEM_HEADS = 4
MEM_HEAD_DIM = 128
MEM_WIDTH = MEM_HEADS * MEM_HEAD_DIM
N_BRANCHES = 3
IN_WIDTH = 3 * ATTN_WIDTH + 2 * RET_QK_WIDTH + 2 * RET_V_WIDTH + MEM_WIDTH
D_FF = 2816
CONV_WIDTH = 3
ROPE_THETA = 10000.0
EPS = 1e-6
NEG_INF = -1e30

kernel_name = 'hybrid_dilated_retention_memory_encoder'


def rms_norm(x, g):
    xf = x.astype(jnp.float32)
    y = xf * lax.rsqrt(jnp.mean(xf * xf, axis=-1, keepdims=True) + EPS)
    return (y * g.astype(jnp.float32)).astype(x.dtype)


def rotary(x, pos):
    d = x.shape[-1]
    inv = ROPE_THETA ** (-jnp.arange(0, d, 2, dtype=jnp.float32) / d)
    ang = pos.astype(jnp.float32)[:, None] * inv[None, :]
    cos, sin = jnp.cos(ang), jnp.sin(ang)
    x1, x2 = jnp.split(x.astype(jnp.float32), 2, axis=-1)
    return jnp.concatenate([x1 * cos - x2 * sin, x1 * sin + x2 * cos], axis=-1).astype(x.dtype)


def _heads(t, n):
    B, S, _ = t.shape
    return t.reshape(B, S, n, -1).transpose(0, 2, 1, 3)


def _merge_heads(t):
    B, H, S, d = t.shape
    return t.transpose(0, 2, 1, 3).reshape(B, S, H * d)


def banded_attention(q, k, v, half):
    B, H, L, hd = q.shape
    nb = -(-L // BAND_BLOCK)
    Lp = nb * BAND_BLOCK
    span = BAND_BLOCK + 2 * half
    qb = jnp.pad(q, ((0, 0), (0, 0), (0, Lp - L), (0, 0))).reshape(B, H, nb, BAND_BLOCK, hd)
    pad_kv = ((0, 0), (0, 0), (half, Lp - L + half), (0, 0))
    kp = jnp.pad(k, pad_kv)
    vp = jnp.pad(v, pad_kv)
    key_idx = jnp.arange(nb)[:, None] * BAND_BLOCK + jnp.arange(span)[None, :]
    kb = jnp.take(kp, key_idx, axis=2)
    vb = jnp.take(vp, key_idx, axis=2)
    q_pos = jnp.arange(Lp).reshape(nb, BAND_BLOCK)
    k_pos = key_idx - half
    valid = ((jnp.abs(q_pos[:, :, None] - k_pos[:, None, :]) <= half)
             & (k_pos[:, None, :] >= 0) & (k_pos[:, None, :] < L))
    s = jnp.einsum('bhnqd,bhnkd->bhnqk', qb, kb).astype(jnp.float32) * (hd ** -0.5)
    s = jnp.where(valid, s, NEG_INF)
    lse = jax.nn.logsumexp(s, axis=-1)
    p = jnp.exp(s - lse[..., None]).astype(v.dtype)
    o = jnp.einsum('bhnqk,bhnkd->bhnqd', p, vb)
    o = o.reshape(B, H, Lp, hd)[:, :, :L]
    lse = lse.reshape(B, H, Lp)[:, :, :L]
    return o, lse


def dilated_group_attention(q, k, v, window, dilation):
    B, H, S, hd = q.shape
    half = window // (2 * dilation)
    L = S // dilation

    def to_residue(t):
        return t.reshape(B, H, L, dilation, hd).transpose(0, 1, 3, 2, 4).reshape(B, H * dilation, L, hd)

    o, lse = banded_attention(to_residue(q), to_residue(k), to_residue(v), half)
    o = o.reshape(B, H, dilation, L, hd).transpose(0, 1, 3, 2, 4).reshape(B, H, S, hd)
    lse = lse.reshape(B, H, dilation, L).transpose(0, 1, 3, 2).reshape(B, H, S)
    return o, lse


def dilated_attention_branch(q, k, v):
    B, _, S, _ = q.shape
    outs, lses = [], []
    for g, (window, dilation) in enumerate(ATTN_GROUPS):
        hs = slice(g * ATTN_HEADS_PER_GROUP, (g + 1) * ATTN_HEADS_PER_GROUP)
        o, lse = dilated_group_attention(q[:, hs], k[:, hs], v[:, hs], window, dilation)
        outs.append(o)
        lses.append(lse)
    alpha = jax.nn.softmax(jnp.stack(lses, axis=0), axis=0)
    y = jnp.stack(outs, axis=0) * alpha[..., None].astype(q.dtype)
    return y.transpose(1, 3, 0, 2, 4).reshape(B, S, ATTN_WIDTH)


def retention_chunkwise(q, k, v, log_gamma, include_diag):
    B, H, S, dk = q.shape
    dv = v.shape[-1]
    C = RET_CHUNK
    N = S // C
    qc = q.reshape(B, H, N, C, dk)
    kc = k.reshape(B, H, N, C, dk)
    vc = v.reshape(B, H, N, C, dv)
    idx = jnp.arange(C, dtype=jnp.float32)
    diff = idx[:, None] - idx[None, :]
    mask = diff >= 0 if include_diag else diff > 0
    decay_in = jnp.where(mask[None], jnp.exp(log_gamma[:, None, None] * jnp.where(mask, diff, 0.0)[None]), 0.0)
    s = jnp.einsum('bhncd,bhnmd->bhncm', qc, kc).astype(jnp.float32) * decay_in[None, :, None]
    inner = jnp.einsum('bhncm,bhnme->bhnce', s, vc.astype(jnp.float32))
    zeta = jnp.exp(log_gamma[:, None] * (C - 1.0 - idx)[None, :])
    kv = jnp.einsum('bhncd,bhnce->bhnde', kc.astype(jnp.float32) * zeta[None, :, None, :, None],
                    vc.astype(jnp.float32))
    chunk_decay = jnp.exp(log_gamma * C)[None, :, None, None]

    def step(state, kv_n):
        return state * chunk_decay + kv_n, state

    _, prev = lax.scan(step, jnp.zeros((B, H, dk, dv), jnp.float32), kv.transpose(2, 0, 1, 3, 4))
    prev = prev.transpose(1, 2, 0, 3, 4)
    xi = jnp.exp(log_gamma[:, None] * (idx + 1.0)[None, :])
    cross = jnp.einsum('bhncd,bhnde->bhnce', qc.astype(jnp.float32), prev) * xi[None, :, None, :, None]
    return (inner + cross).reshape(B, H, S, dv)


def retention_branch(q, k, v, gate, decay_fwd, decay_bwd, g_ret):
    lg_f = jax.nn.log_sigmoid(decay_fwd.astype(jnp.float32))
    lg_b = jax.nn.log_sigmoid(decay_bwd.astype(jnp.float32))
    y_f = retention_chunkwise(q, k, v, lg_f, True)
    flip = lambda t: jnp.flip(t, axis=2)
    y_b = flip(retention_chunkwise(flip(q), flip(k), flip(v), lg_b, False))
    y = y_f + y_b
    mu = jnp.mean(y, axis=-1, keepdims=True)
    var = jnp.mean(jnp.square(y - mu), axis=-1, keepdims=True)
    y = (y - mu) * lax.rsqrt(var + EPS)
    y = _merge_heads(y) * g_ret.astype(jnp.float32)
    return (y * jax.nn.silu(gate.astype(jnp.float32))).astype(gate.dtype)


def memory_cross_attention(q, mem_n, w_mem_kv):
    km, vm = jnp.split(mem_n @ w_mem_kv, 2, axis=-1)
    km = _heads(km, MEM_HEADS)
    vm = _heads(vm, MEM_HEADS)
    s = jnp.einsum('bhsd,bhmd->bhsm', q, km).astype(jnp.float32) * (MEM_HEAD_DIM ** -0.5)
    p = jax.nn.softmax(s, axis=-1).astype(vm.dtype)
    return _merge_heads(jnp.einsum('bhsm,bhmd->bhsd', p, vm))


def depthwise_conv(u, w, b):
    C = u.shape[-1]
    out = lax.conv_general_dilated(
        u, w[:, None, :].astype(u.dtype), window_strides=(1,),
        padding=((CONV_WIDTH // 2, CONV_WIDTH // 2),),
        dimension_numbers=('NWC', 'WIO', 'NWC'), feature_group_count=C)
    return out + b


def hybrid_layer(h, mem, g_mix, w_in, w_mem_kv, g_mem, decay_fwd, decay_bwd, g_ret,
                 w_proj_attn, w_proj_ret, w_proj_mem, w_gate, b_gate, w_out,
                 g_ffn, w_up, conv_w, conv_b, w_down):
    B, S, _ = h.shape
    pos = jnp.arange(S)
    n = rms_norm(h, g_mix)
    proj = n @ w_in
    sizes = [ATTN_WIDTH, ATTN_WIDTH, ATTN_WIDTH, RET_QK_WIDTH, RET_QK_WIDTH, RET_V_WIDTH, RET_V_WIDTH]
    cuts = []
    acc = 0
    for s_ in sizes:
        acc += s_
        cuts.append(acc)
    qa, ka, va, qr, kr, vr, gr, qm = jnp.split(proj, cuts, axis=-1)

    qa = rotary(_heads(qa, ATTN_HEADS), pos)
    ka = rotary(_heads(ka, ATTN_HEADS), pos)
    y_a = dilated_attention_branch(qa, ka, _heads(va, ATTN_HEADS))

    qr = rotary(_heads(qr, RET_HEADS), pos)
    kr = rotary(_heads(kr, RET_HEADS), pos) * (RET_QK_DIM ** -0.5)
    y_r = retention_branch(qr, kr, _heads(vr, RET_HEADS), gr, decay_fwd, decay_bwd, g_ret)

    y_m = memory_cross_attention(_heads(qm, MEM_HEADS), rms_norm(mem, g_mem), w_mem_kv)

    gates = jax.nn.sigmoid(n @ w_gate + b_gate).reshape(B, S, N_BRANCHES, D_MODEL)
    merged = (gates[:, :, 0] * (y_a @ w_proj_attn)
              + gates[:, :, 1] * (y_r @ w_proj_ret)
              + gates[:, :, 2] * (y_m @ w_proj_mem))
    h = h + merged @ w_out

    u = depthwise_conv(rms_norm(h, g_ffn) @ w_up, conv_w, conv_b)
    a, b = jnp.split(u, 2, axis=-1)
    return h + (jax.nn.silu(a) * b) @ w_down


def setup_inputs(seed: int = 0) -> dict:
    key = jax.random.key(seed)
    ks = jax.random.split(key, 24)
    f32 = jnp.float32

    def nrm(k, shape, fan_in):
        return jax.random.normal(k, shape, f32) * (fan_in ** -0.5)

    def gain(k, shape):
        return 1.0 + 0.01 * jax.random.normal(k, shape, f32)

    a = 5.0 + jnp.arange(RET_HEADS, dtype=f32)
    decay_logit = jnp.log(2.0 ** a - 1.0)
    return {
        'x': jax.random.normal(ks[0], (BATCH, SEQ, D_MODEL), f32),
        'mem': jax.random.normal(ks[1], (BATCH, MEM_LEN, D_MODEL), f32),
        'g_mix': gain(ks[2], (DEPTH, D_MODEL)),
        'w_in': nrm(ks[3], (DEPTH, D_MODEL, IN_WIDTH), D_MODEL),
        'w_mem_kv': nrm(ks[4], (DEPTH, D_MODEL, 2 * MEM_WIDTH), D_MODEL),
        'g_mem': gain(ks[5], (DEPTH, D_MODEL)),
        'ret_decay_fwd': decay_logit[None] + 0.1 * jax.random.normal(ks[6], (DEPTH, RET_HEADS), f32),
        'ret_decay_bwd': decay_logit[None] + 0.1 * jax.random.normal(ks[7], (DEPTH, RET_HEADS), f32),
        'g_ret': gain(ks[8], (DEPTH, RET_V_WIDTH)),
        'w_proj_attn': nrm(ks[9], (DEPTH, ATTN_WIDTH, D_MODEL), ATTN_WIDTH),
        'w_proj_ret': nrm(ks[10], (DEPTH, RET_V_WIDTH, D_MODEL), RET_V_WIDTH),
        'w_proj_mem': nrm(ks[11], (DEPTH, MEM_WIDTH, D_MODEL), MEM_WIDTH),
        'w_gate': nrm(ks[12], (DEPTH, D_MODEL, N_BRANCHES * D_MODEL), D_MODEL),
        'b_gate': 0.01 * jax.random.normal(ks[13], (DEPTH, N_BRANCHES * D_MODEL), f32),
        'w_out': nrm(ks[14], (DEPTH, D_MODEL, D_MODEL), D_MODEL),
        'g_ffn': gain(ks[15], (DEPTH, D_MODEL)),
        'w_up': nrm(ks[16], (DEPTH, D_MODEL, 2 * D_FF), D_MODEL),
        'conv_w': nrm(ks[17], (DEPTH, CONV_WIDTH, 2 * D_FF), CONV_WIDTH),
        'conv_b': 0.01 * jax.random.normal(ks[18], (DEPTH, 2 * D_FF), f32),
        'w_down': nrm(ks[19], (DEPTH, D_FF, D_MODEL), D_FF),
        'g_final': gain(ks[20], (D_MODEL,)),
    }


def reference(x, mem, g_mix, w_in, w_mem_kv, g_mem, ret_decay_fwd, ret_decay_bwd, g_ret,
              w_proj_attn, w_proj_ret, w_proj_mem, w_gate, b_gate, w_out,
              g_ffn, w_up, conv_w, conv_b, w_down, g_final):
    h = x
    for l in range(DEPTH):
        h = hybrid_layer(h, mem, g_mix[l], w_in[l], w_mem_kv[l], g_mem[l],
                         ret_decay_fwd[l], ret_decay_bwd[l], g_ret[l],
                         w_proj_attn[l], w_proj_ret[l], w_proj_mem[l], w_gate[l], b_gate[l], w_out[l],
                         g_ffn[l], w_up[l], conv_w[l], conv_b[l], w_down[l])
    return rms_norm(h, g_final)
```

```python
import functools
import math

import jax
import jax.numpy as jnp
from jax import lax
from jax.experimental import pallas as pl
from jax.experimental.pallas import tpu as pltpu

D_MODEL = 1024
HEAD_DIM = 64
ATTN_GROUPS = ((128, 1), (512, 4), (2048, 16))
GROUP_WIDTH = 4 * HEAD_DIM
ATTN_WIDTH = 3 * GROUP_WIDTH
BAND_HALF = 64
RET_HEADS = 6
RET_QK_DIM = 64
RET_V_DIM = 128
RET_QK_WIDTH = RET_HEADS * RET_QK_DIM
RET_V_WIDTH = RET_HEADS * RET_V_DIM
MEM_HEADS = 4
MEM_HEAD_DIM = 128
MEM_WIDTH = MEM_HEADS * MEM_HEAD_DIM
D_FF = 2816
ROPE_THETA = 10000.0
EPS = 1e-6
NEG_INF = -1e30

LANES = 128
SUBLANES = 8
MXU_WIDTH = 256
VMEM_LIMIT = 56 * 1024 * 1024

ROW_TILE = 512
RET_CHUNK = 128
Q_BLOCK = 128
K_SPAN = Q_BLOCK + 2 * BAND_HALF

BF16 = jnp.bfloat16
F32 = jnp.float32


def _params(n_grid_axes):
    return pltpu.CompilerParams(
        dimension_semantics=("arbitrary",) * n_grid_axes, vmem_limit_bytes=VMEM_LIMIT)


def _resident(shape):
    nd = len(shape)
    return pl.BlockSpec(shape, lambda *_: (0,) * nd, pipeline_mode=pl.Buffered(1))


def _rms(x, g):
    return x * lax.rsqrt(jnp.mean(x * x, axis=-1, keepdims=True) + EPS) * g


def _sigmoid(x):
    return 1.0 / (1.0 + jnp.exp(-x))


_IN_SEGMENTS = (
    (GROUP_WIDTH, True, HEAD_DIM ** -0.5), (GROUP_WIDTH, True, HEAD_DIM ** -0.5),
    (GROUP_WIDTH, True, HEAD_DIM ** -0.5),
    (GROUP_WIDTH, True, 1.0), (GROUP_WIDTH, True, 1.0), (GROUP_WIDTH, True, 1.0),
    (GROUP_WIDTH, False, 1.0), (GROUP_WIDTH, False, 1.0), (GROUP_WIDTH, False, 1.0),
    (RET_QK_WIDTH, True, 1.0),
    (RET_QK_WIDTH, True, RET_QK_DIM ** -0.5),
    (RET_V_WIDTH, False, 1.0),
    (RET_V_WIDTH, False, 1.0),
    (MEM_WIDTH, False, 1.0),
)


def _in_proj_kernel(x_ref, g_ref, cos_ref, sin_ref, w_ref, *out_refs):
    n = _rms(x_ref[...], g_ref[...]).astype(BF16)
    cos = cos_ref[...]
    sin = sin_ref[...]
    lane = lax.broadcasted_iota(jnp.int32, cos.shape, 1)
    low_half = (lane & (HEAD_DIM // 2)) == 0

    def rotate(a):
        partner = jnp.where(low_half, pltpu.roll(a, LANES - HEAD_DIM // 2, 1),
                            pltpu.roll(a, HEAD_DIM // 2, 1))
        return a * cos + partner * sin

    col = 0
    for out_ref, (width, rotary, scale) in zip(out_refs, _IN_SEGMENTS):
        for c in range(0, width, MXU_WIDTH):
            cw = min(MXU_WIDTH, width - c)
            acc = jnp.dot(n, w_ref[:, col + c:col + c + cw], preferred_element_type=F32)
            for j in range(0, cw, LANES):
                a = acc[:, j:j + LANES]
                if rotary:
                    a = rotate(a)
                if scale != 1.0:
                    a = a * scale
                out_ref[:, c + j:c + j + LANES] = a.astype(BF16)
        col += width


def _in_proj(x2, g_mix, cos_t, sin_t, w_in, seq):
    t = x2.shape[0]
    tm = ROW_TILE
    tiles_per_seq = seq // tm
    row = lambda i: (i, 0)
    pos = lambda i: (i % tiles_per_seq, 0)
    out_shape = [jax.ShapeDtypeStruct((t, w), BF16) for (w, _, _) in _IN_SEGMENTS]
    out_specs = [pl.BlockSpec((tm, w), row) for (w, _, _) in _IN_SEGMENTS]
    return pl.pallas_call(
        _in_proj_kernel,
        grid=(t // tm,),
        in_specs=[
            pl.BlockSpec((tm, D_MODEL), row),
            _resident((1, D_MODEL)),
            pl.BlockSpec((tm, LANES), pos),
            pl.BlockSpec((tm, LANES), pos),
            _resident(w_in.shape),
        ],
        out_specs=out_specs,
        out_shape=out_shape,
        compiler_params=_params(1),
        name="in_proj",
    )(x2, g_mix, cos_t, sin_t, w_in)


def _band_attn_kernel(q_ref, k_ref, v_ref, o_ref, lse_ref, *, seq_len, n_res):
    n_blocks = seq_len // Q_BLOCK
    qi = lax.broadcasted_iota(jnp.int32, (Q_BLOCK, K_SPAN), 0)
    kj = lax.broadcasted_iota(jnp.int32, (Q_BLOCK, K_SPAN), 1)
    rel = qi - kj
    lane = lax.broadcasted_iota(jnp.int32, (Q_BLOCK, LANES), 1)
    head0 = lane < HEAD_DIM

    def block(i, carry):
        q0 = pl.multiple_of(i * Q_BLOCK, Q_BLOCK)
        k0 = pl.multiple_of(jnp.clip(i * Q_BLOCK - BAND_HALF, 0, seq_len - K_SPAN), BAND_HALF)
        valid = jnp.abs(rel + (q0 - k0)) <= BAND_HALF
        for pair in range(n_res * 2):
            c0 = pair * LANES
            qp = q_ref[0, pl.ds(q0, Q_BLOCK), c0:c0 + LANES]
            kp = k_ref[0, pl.ds(k0, K_SPAN), c0:c0 + LANES]
            vp = v_ref[0, pl.ds(k0, K_SPAN), c0:c0 + LANES]
            outs = []
            for h in range(2):
                mask_h = head0 if h == 0 else jnp.logical_not(head0)
                qh = jnp.where(mask_h, qp, jnp.zeros_like(qp))
                s = lax.dot_general(qh, kp, (((1,), (1,)), ((), ())),
                                    preferred_element_type=F32)
                s = jnp.where(valid, s, NEG_INF)
                m = jnp.max(s, axis=-1, keepdims=True)
                p = jnp.exp(s - m)
                l = jnp.sum(p, axis=-1, keepdims=True)
                o = jnp.dot(p.astype(BF16), vp, preferred_element_type=F32)
                outs.append((o / l, m + jnp.log(l)))
            o_pair = jnp.where(head0, outs[0][0], outs[1][0])
            lse_pair = jnp.where(head0, outs[0][1], outs[1][1])
            o_ref[0, pl.ds(q0, Q_BLOCK), c0:c0 + LANES] = o_pair.astype(BF16)
            lse_ref[0, pl.ds(q0, Q_BLOCK), c0:c0 + LANES] = lse_pair
        return carry

    lax.fori_loop(0, n_blocks, block, 0)


def _band_attn(q, k, v, batch, seq, dilation):
    seq_len = seq // dilation
    n_res = min(dilation, 4)
    width = dilation * GROUP_WIDTH
    view = lambda a: a.reshape(batch, seq_len, width)
    blk = (1, seq_len, n_res * GROUP_WIDTH)
    idx = lambda b, r: (b, 0, r)
    o, lse = pl.pallas_call(
        functools.partial(_band_attn_kernel, seq_len=seq_len, n_res=n_res),
        grid=(batch, dilation // n_res),
        in_specs=[pl.BlockSpec(blk, idx)] * 3,
        out_specs=[pl.BlockSpec(blk, idx)] * 2,
        out_shape=[jax.ShapeDtypeStruct((batch, seq_len, width), BF16),
                   jax.ShapeDtypeStruct((batch, seq_len, width), F32)],
        compiler_params=_params(2),
        name=f"band_attn_d{dilation}",
    )(view(q), view(k), view(v))
    t = batch * seq
    return o.reshape(t, GROUP_WIDTH), lse.reshape(t, GROUP_WIDTH)


def _log_sigmoid(x):
    return jnp.minimum(x, 0.0) - jnp.log1p(jnp.exp(-jnp.abs(x)))


def _retention_kernel(decf_ref, decb_ref, q_ref, k_ref, v_ref, gate_ref, gret_ref, o_ref,
                      sf_ref, sb_ref, *, seq):
    c = RET_CHUNK
    n_chunks = seq // c
    pair_w = 2 * RET_QK_DIM
    lg_f = _log_sigmoid(decf_ref[0])
    lg_b = _log_sigmoid(decb_ref[0])
    lgf_row = lg_f[0:1, :]
    lgb_row = lg_b[0:1, :]
    pos = lax.broadcasted_iota(jnp.int32, (c, pair_w), 0).astype(F32)
    zeta_f = jnp.exp(lgf_row * (c - 1.0 - pos))
    zeta_b = jnp.exp(lgb_row * pos)
    bcast = lambda lg, h, shape: jnp.broadcast_to(
        lg[0:1, h * RET_QK_DIM:h * RET_QK_DIM + 1], shape)
    xi_f = [jnp.exp(bcast(lg_f, h, (c, RET_V_DIM)) * (pos + 1.0)) for h in range(2)]
    xi_b = [jnp.exp(bcast(lg_b, h, (c, RET_V_DIM)) * (c - pos)) for h in range(2)]
    st_shape = (pair_w, 2 * RET_V_DIM)
    row_is_h0 = lax.broadcasted_iota(jnp.int32, st_shape, 0) < RET_QK_DIM
    chunk_decay_f = jnp.exp(
        jnp.where(row_is_h0, bcast(lg_f, 0, st_shape), bcast(lg_f, 1, st_shape)) * float(c))
    chunk_decay_b = jnp.exp(
        jnp.where(row_is_h0, bcast(lg_b, 0, st_shape), bcast(lg_b, 1, st_shape)) * float(c))

    def chunk_kv(n, zeta):
        r0 = pl.multiple_of(n * c, c)
        kz = (k_ref[0, pl.ds(r0, c), :].astype(F32) * zeta).astype(BF16)
        return lax.dot_general(kz, v_ref[0, pl.ds(r0, c), :], (((0,), (0,)), ((), ())),
                               preferred_element_type=F32)

    def fwd_state(n, state):
        sf_ref[n] = state.astype(BF16)
        return state * chunk_decay_f + chunk_kv(n, zeta_f)

    def bwd_state(j, state):
        n = n_chunks - 1 - j
        sb_ref[n] = state.astype(BF16)
        return state * chunk_decay_b + chunk_kv(n, zeta_b)

    zero_state = jnp.zeros((pair_w, 2 * RET_V_DIM), F32)
    lax.fori_loop(0, n_chunks, fwd_state, zero_state)
    lax.fori_loop(0, n_chunks, bwd_state, zero_state)

    ci = lax.broadcasted_iota(jnp.int32, (c, c), 0)
    mi = lax.broadcasted_iota(jnp.int32, (c, c), 1)
    delta = (ci - mi).astype(F32)
    causal = ci >= mi
    lane = lax.broadcasted_iota(jnp.int32, (c, pair_w), 1)
    head0 = lane < RET_QK_DIM
    decay_in = []
    for h in range(2):
        decay_in.append(jnp.where(
            causal, jnp.exp(bcast(lg_f, h, (c, c)) * jnp.maximum(delta, 0.0)),
            jnp.exp(bcast(lg_b, h, (c, c)) * jnp.maximum(-delta, 0.0))))

    def out_chunk(n, carry):
        r0 = pl.multiple_of(n * c, c)
        qp = q_ref[0, pl.ds(r0, c), :]
        kp = k_ref[0, pl.ds(r0, c), :]
        for h in range(2):
            vs = slice(h * RET_V_DIM, (h + 1) * RET_V_DIM)
            mask_h = head0 if h == 0 else jnp.logical_not(head0)
            qh = jnp.where(mask_h, qp, jnp.zeros_like(qp))
            s = lax.dot_general(qh, kp, (((1,), (1,)), ((), ())), preferred_element_type=F32)
            s = (s * decay_in[h]).astype(BF16)
            y = jnp.dot(s, v_ref[0, pl.ds(r0, c), vs], preferred_element_type=F32)
            y = y + jnp.dot(qh, sf_ref[n, :, vs], preferred_element_type=F32) * xi_f[h]
            y = y + jnp.dot(qh, sb_ref[n, :, vs], preferred_element_type=F32) * xi_b[h]
            mu = jnp.mean(y, axis=-1, keepdims=True)
            yc = y - mu
            var = jnp.mean(yc * yc, axis=-1, keepdims=True)
            yn = yc * lax.rsqrt(var + EPS) * gret_ref[:, vs]
            gate = gate_ref[0, pl.ds(r0, c), vs].astype(F32)
            o_ref[0, pl.ds(r0, c), vs] = (yn * (gate * _sigmoid(gate))).astype(BF16)
        return carry

    lax.fori_loop(0, n_chunks, out_chunk, 0)


def _retention(qr, kr, vr, gr, dec_f, dec_b, g_ret, batch, seq):
    pairs = RET_HEADS // 2
    n_chunks = seq // RET_CHUNK
    v3 = lambda a: a.reshape(batch, seq, a.shape[-1])
    qk_spec = pl.BlockSpec((1, seq, 2 * RET_QK_DIM), lambda b, p: (b, 0, p))
    v_spec = pl.BlockSpec((1, seq, 2 * RET_V_DIM), lambda b, p: (b, 0, p))
    dec_spec = pl.BlockSpec((1, SUBLANES, LANES), lambda b, p: (p, 0, 0))
    out = pl.pallas_call(
        functools.partial(_retention_kernel, seq=seq),
        grid=(batch, pairs),
        in_specs=[dec_spec, dec_spec, qk_spec, qk_spec, v_spec, v_spec,
                  pl.BlockSpec((1, 2 * RET_V_DIM), lambda b, p: (0, p))],
        out_specs=v_spec,
        out_shape=jax.ShapeDtypeStruct((batch, seq, RET_V_WIDTH), BF16),
        scratch_shapes=[pltpu.VMEM((n_chunks, 2 * RET_QK_DIM, 2 * RET_V_DIM), BF16),
                        pltpu.VMEM((n_chunks, 2 * RET_QK_DIM, 2 * RET_V_DIM), BF16)],
        compiler_params=_params(2),
        name="retention",
    )(dec_f, dec_b, v3(qr), v3(kr), v3(vr), v3(gr), g_ret)
    return out.reshape(batch * seq, RET_V_WIDTH)


def _mem_kv_kernel(mem_ref, g_ref, w_ref, k_ref, v_ref):
    n = _rms(mem_ref[0], g_ref[...]).astype(BF16)
    kv = jnp.dot(n, w_ref[...], preferred_element_type=F32)
    k_ref[0] = kv[:, :MEM_WIDTH].astype(BF16)
    v_ref[0] = kv[:, MEM_WIDTH:].astype(BF16)


def _mem_kv(mem, g_mem, w_mem_kv):
    batch, mem_len, _ = mem.shape
    out = jax.ShapeDtypeStruct((batch, mem_len, MEM_WIDTH), BF16)
    spec = pl.BlockSpec((1, mem_len, MEM_WIDTH), lambda b: (b, 0, 0))
    return pl.pallas_call(
        _mem_kv_kernel,
        grid=(batch,),
        in_specs=[pl.BlockSpec((1, mem_len, D_MODEL), lambda b: (b, 0, 0)),
                  _resident((1, D_MODEL)), _resident(w_mem_kv.shape)],
        out_specs=[spec, spec],
        out_shape=[out, out],
        compiler_params=_params(1),
        name="mem_kv",
    )(mem, g_mem, w_mem_kv)


def _merge_kernel(x_ref, g_ref, o0_ref, o1_ref, o2_ref, l0_ref, l1_ref, l2_ref, yr_ref, qm_ref,
                  km_ref, vm_ref, wg_ref, bg_ref, wpa_ref, wpr_ref, wpm_ref, wo_ref, h_ref):
    x = x_ref[...]
    n = _rms(x, g_ref[...]).astype(BF16)

    lses = (l0_ref[...], l1_ref[...], l2_ref[...])
    top = jnp.maximum(jnp.maximum(lses[0], lses[1]), lses[2])
    es = [jnp.exp(l - top) for l in lses]
    inv = 1.0 / (es[0] + es[1] + es[2])
    pa = None
    for g, o_ref in enumerate((o0_ref, o1_ref, o2_ref)):
        y_g = (o_ref[...].astype(F32) * (es[g] * inv)).astype(BF16)
        part = jnp.dot(y_g, wpa_ref[g * GROUP_WIDTH:(g + 1) * GROUP_WIDTH, :],
                       preferred_element_type=F32)
        pa = part if pa is None else pa + part

    def gate(i):
        z = jnp.dot(n, wg_ref[:, i * D_MODEL:(i + 1) * D_MODEL], preferred_element_type=F32)
        return _sigmoid(z + bg_ref[:, i * D_MODEL:(i + 1) * D_MODEL])

    merged = gate(0) * pa
    merged = merged + gate(1) * jnp.dot(yr_ref[...], wpr_ref[...], preferred_element_type=F32)

    ym = []
    for hd in range(MEM_HEADS):
        cs = slice(hd * MEM_HEAD_DIM, (hd + 1) * MEM_HEAD_DIM)
        s = lax.dot_general(qm_ref[:, cs], km_ref[0, :, cs], (((1,), (1,)), ((), ())),
                            preferred_element_type=F32) * (MEM_HEAD_DIM ** -0.5)
        p = jnp.exp(s - jnp.max(s, axis=-1, keepdims=True))
        p = p / jnp.sum(p, axis=-1, keepdims=True)
        ym.append(jnp.dot(p.astype(BF16), vm_ref[0, :, cs],
                          preferred_element_type=F32).astype(BF16))
    ym = jnp.concatenate(ym, axis=-1)
    merged = merged + gate(2) * jnp.dot(ym, wpm_ref[...], preferred_element_type=F32)

    h_ref[...] = x + jnp.dot(merged.astype(BF16), wo_ref[...], preferred_element_type=F32)


def _merge(x2, g_mix, os_, lses, y_r, q_m, k_m, v_m, w_gate, b_gate, w_pa, w_pr, w_pm, w_out,
           seq):
    t = x2.shape[0]
    tm = ROW_TILE
    tiles_per_seq = seq // tm
    row = lambda i: (i, 0)
    mem_len = k_m.shape[1]
    mem_spec = pl.BlockSpec((1, mem_len, MEM_WIDTH), lambda i: (i // tiles_per_seq, 0, 0))
    return pl.pallas_call(
        _merge_kernel,
        grid=(t // tm,),
        in_specs=[
            pl.BlockSpec((tm, D_MODEL), row), _resident((1, D_MODEL)),
            *[pl.BlockSpec((tm, GROUP_WIDTH), row)] * 6,
            pl.BlockSpec((tm, RET_V_WIDTH), row),
            pl.BlockSpec((tm, MEM_WIDTH), row),
            mem_spec, mem_spec,
            _resident(w_gate.shape), _resident(b_gate.shape), _resident(w_pa.shape),
            _resident(w_pr.shape), _resident(w_pm.shape), _resident(w_out.shape),
        ],
        out_specs=pl.BlockSpec((tm, D_MODEL), row),
        out_shape=jax.ShapeDtypeStruct((t, D_MODEL), F32),
        compiler_params=_params(1),
        name="merge",
    )(x2, g_mix, *os_, *lses, y_r, q_m, k_m, v_m, w_gate, b_gate, w_pa, w_pr, w_pm, w_out)


FF_CHUNK = MXU_WIDTH
HALO = SUBLANES


def _ffn_kernel(h_ref, prev_ref, next_ref, gffn_ref, wup_ref, cw_ref, cb_ref, wdown_ref,
                gfin_ref, out_ref, *, tiles_per_seq):
    i = pl.program_id(0)
    tm = h_ref.shape[0]
    h = h_ref[...]
    g = gffn_ref[...]
    has_prev = (i % tiles_per_seq) != 0
    has_next = (i % tiles_per_seq) != tiles_per_seq - 1
    n_prev = jnp.where(has_prev, _rms(prev_ref[...], g), 0.0)
    n_next = jnp.where(has_next, _rms(next_ref[...], g), 0.0)
    n_ext = jnp.concatenate([n_prev, _rms(h, g), n_next], axis=0).astype(BF16)
    rows = tm + 2 * HALO

    def conv(u, c0):
        up = pltpu.roll(u, 1, 0)[HALO:HALO + tm]
        un = pltpu.roll(u, rows - 1, 0)[HALO:HALO + tm]
        uc = u[HALO:HALO + tm]
        sl = slice(c0, c0 + FF_CHUNK)
        return (up * cw_ref[0:1, sl] + uc * cw_ref[1:2, sl] + un * cw_ref[2:3, sl]
                + cb_ref[:, sl])

    acc = None
    for c0 in range(0, D_FF, FF_CHUNK):
        ua = jnp.dot(n_ext, wup_ref[:, c0:c0 + FF_CHUNK], preferred_element_type=F32)
        ub = jnp.dot(n_ext, wup_ref[:, D_FF + c0:D_FF + c0 + FF_CHUNK],
                     preferred_element_type=F32)
        a = conv(ua, c0)
        b = conv(ub, D_FF + c0)
        act = (a * _sigmoid(a) * b).astype(BF16)
        part = jnp.dot(act, wdown_ref[c0:c0 + FF_CHUNK, :], preferred_element_type=F32)
        acc = part if acc is None else acc + part
    out_ref[...] = _rms(h + acc, gfin_ref[...])


def _ffn(h, g_ffn, w_up, conv_w, conv_b, w_down, g_final, seq):
    t = h.shape[0]
    tm = ROW_TILE
    tiles_per_seq = seq // tm
    halo_blocks = tm // HALO
    last_block = t // HALO - 1
    row = lambda i: (i, 0)
    return pl.pallas_call(
        functools.partial(_ffn_kernel, tiles_per_seq=tiles_per_seq),
        grid=(t // tm,),
        in_specs=[
            pl.BlockSpec((tm, D_MODEL), row),
            pl.BlockSpec((HALO, D_MODEL), lambda i: (jnp.maximum(i * halo_blocks - 1, 0), 0)),
            pl.BlockSpec((HALO, D_MODEL),
                         lambda i: (jnp.minimum((i + 1) * halo_blocks, last_block), 0)),
            _resident((1, D_MODEL)), _resident(w_up.shape), _resident(conv_w.shape),
            _resident(conv_b.shape), _resident(w_down.shape), _resident((1, D_MODEL)),
        ],
        out_specs=pl.BlockSpec((tm, D_MODEL), row),
        out_shape=jax.ShapeDtypeStruct((t, D_MODEL), F32),
        compiler_params=_params(1),
        name="ffn",
    )(h, h, h, g_ffn, w_up, conv_w, conv_b, w_down, g_final)


def _rotary_tables(seq):
    inv = ROPE_THETA ** (-jnp.arange(0, HEAD_DIM, 2, dtype=F32) / HEAD_DIM)
    ang = jnp.arange(seq, dtype=F32)[:, None] * inv[None, :]
    cos, sin = jnp.cos(ang), jnp.sin(ang)
    reps = LANES // HEAD_DIM
    cos_t = jnp.tile(jnp.concatenate([cos, cos], axis=-1), (1, reps))
    sin_t = jnp.tile(jnp.concatenate([-sin, sin], axis=-1), (1, reps))
    return cos_t, sin_t


def _pair_lanes(v):
    pairs = v.reshape(RET_HEADS // 2, 2, 1)
    lanes = jnp.broadcast_to(pairs, (RET_HEADS // 2, 2, RET_QK_DIM)).reshape(RET_HEADS // 2, 1, LANES)
    return jnp.broadcast_to(lanes, (RET_HEADS // 2, SUBLANES, LANES)).astype(F32)


def _layer(h2, mem, batch, seq, g_mix, w_in, w_mem_kv, g_mem, decay_fwd, decay_bwd, g_ret,
           w_proj_attn, w_proj_ret, w_proj_mem, w_gate, b_gate, w_out):
    bf = lambda w: w.astype(BF16)
    row = lambda v: v.reshape(1, -1).astype(F32)
    cos_t, sin_t = _rotary_tables(seq)
    proj = _in_proj(h2, row(g_mix), cos_t, sin_t, bf(w_in), seq)
    qa, ka, va = proj[0:3], proj[3:6], proj[6:9]
    q_r, k_r, v_r, g_r, q_m = proj[9:]

    os_, lses = [], []
    for g, (_, dilation) in enumerate(ATTN_GROUPS):
        o, lse = _band_attn(qa[g], ka[g], va[g], batch, seq, dilation)
        os_.append(o)
        lses.append(lse)

    y_r = _retention(q_r, k_r, v_r, g_r, _pair_lanes(decay_fwd), _pair_lanes(decay_bwd),
                     row(g_ret), batch, seq)
    k_m, v_m = _mem_kv(mem, row(g_mem), bf(w_mem_kv))
    return _merge(h2, row(g_mix), os_, lses, y_r, q_m, k_m, v_m, bf(w_gate), row(b_gate),
                  bf(w_proj_attn), bf(w_proj_ret), bf(w_proj_mem), bf(w_out), seq)


def kernel(x, mem, g_mix, w_in, w_mem_kv, g_mem, ret_decay_fwd, ret_decay_bwd, g_ret,
           w_proj_attn, w_proj_ret, w_proj_mem, w_gate, b_gate, w_out,
           g_ffn, w_up, conv_w, conv_b, w_down, g_final):
    batch, seq, d = x.shape
    depth = w_in.shape[0]
    assert d == D_MODEL and depth == 1 and seq % ROW_TILE == 0
    h2 = x.reshape(batch * seq, d)
    l = 0
    h2 = _layer(h2, mem, batch, seq, g_mix[l], w_in[l], w_mem_kv[l], g_mem[l],
                ret_decay_fwd[l], ret_decay_bwd[l], g_ret[l], w_proj_attn[l], w_proj_ret[l],
                w_proj_mem[l], w_gate[l], b_gate[l], w_out[l])
    out = _ffn(h2, g_ffn[l].reshape(1, -1), w_up[l].astype(BF16), conv_w[l],
               conv_b[l].reshape(1, -1), w_down[l].astype(BF16), g_final.reshape(1, -1), seq)
    return out.reshape(batch, seq, d)
```

```python
import functools
import math

import jax
import jax.numpy as jnp
from jax import lax
from jax.experimental import pallas as pl
from jax.experimental.pallas import tpu as pltpu

D_MODEL = 1024
HEAD_DIM = 64
ATTN_GROUPS = ((128, 1), (512, 4), (2048, 16))
GROUP_WIDTH = 4 * HEAD_DIM
ATTN_WIDTH = 3 * GROUP_WIDTH
BAND_HALF = 64
RET_HEADS = 6
RET_QK_DIM = 64
RET_V_DIM = 128
RET_QK_WIDTH = RET_HEADS * RET_QK_DIM
RET_V_WIDTH = RET_HEADS * RET_V_DIM
MEM_HEADS = 4
MEM_HEAD_DIM = 128
MEM_WIDTH = MEM_HEADS * MEM_HEAD_DIM
D_FF = 2816
ROPE_THETA = 10000.0
EPS = 1e-6
NEG_INF = -1e30

LANES = 128
SUBLANES = 8
MXU_WIDTH = 256
VMEM_LIMIT = 56 * 1024 * 1024

ROW_TILE = 512
RET_CHUNK = 128
Q_BLOCK = 128
K_SPAN = Q_BLOCK + 2 * BAND_HALF
STATE_UNROLL = 4
MIX_UNROLL = 4
NORM_UNROLL = 4
ATTN_PAIRS_IN_FLIGHT = 8

BF16 = jnp.bfloat16
F32 = jnp.float32


def _params(n_grid_axes):
    return pltpu.CompilerParams(
        dimension_semantics=("arbitrary",) * n_grid_axes, vmem_limit_bytes=VMEM_LIMIT)


def _resident(shape):
    nd = len(shape)
    return pl.BlockSpec(shape, lambda *_: (0,) * nd, pipeline_mode=pl.Buffered(1))


def _rms(x, g):
    return x * lax.rsqrt(jnp.mean(x * x, axis=-1, keepdims=True) + EPS) * g


def _sigmoid(x):
    return 1.0 / (1.0 + jnp.exp(-x))


_DILATIONS = tuple(d for _, d in ATTN_GROUPS)
_IN_SEGMENTS = (
    *[(GROUP_WIDTH, True, HEAD_DIM ** -0.5, d) for d in _DILATIONS],
    *[(GROUP_WIDTH, True, 1.0, d) for d in _DILATIONS],
    *[(GROUP_WIDTH, False, 1.0, d) for d in _DILATIONS],
    (RET_QK_WIDTH, True, 1.0, 1),
    (RET_QK_WIDTH, True, RET_QK_DIM ** -0.5, 1),
    (RET_V_WIDTH, False, 1.0, 1),
    (RET_V_WIDTH, False, 1.0, 1),
    (MEM_WIDTH, False, 1.0, 1),
)


def _in_proj_kernel(x_ref, g_ref, cos_ref, sin_ref, w_ref, *refs):
    out_refs, stage_ref = refs[:-1], refs[-1]
    tm = x_ref.shape[0]
    n = _rms(x_ref[...], g_ref[...]).astype(BF16)
    cos = cos_ref[...]
    sin = sin_ref[...]
    lane = lax.broadcasted_iota(jnp.int32, cos.shape, 1)
    low_half = (lane & (HEAD_DIM // 2)) == 0

    def rotate(a):
        partner = jnp.where(low_half, pltpu.roll(a, LANES - HEAD_DIM // 2, 1),
                            pltpu.roll(a, HEAD_DIM // 2, 1))
        return a * cos + partner * sin

    col = 0
    for out_ref, (width, rotary, scale, dil) in zip(out_refs, _IN_SEGMENTS):
        for c in range(0, width, MXU_WIDTH):
            cw = min(MXU_WIDTH, width - c)
            acc = jnp.dot(n, w_ref[:, col + c:col + c + cw], preferred_element_type=F32)
            for j in range(0, cw, LANES):
                a = acc[:, j:j + LANES]
                if rotary:
                    a = rotate(a)
                if scale != 1.0:
                    a = a * scale
                if dil == 1:
                    out_ref[:, c + j:c + j + LANES] = a.astype(BF16)
                else:
                    stage_ref[j // LANES] = a
            if dil > 1:
                assert width == cw == stage_ref.shape[0] * LANES
                for r in range(dil):
                    for jb in range(width // LANES):
                        rows = stage_ref[jb, pl.ds(r, tm // dil, stride=dil), :]
                        out_ref[:, r * width + jb * LANES:r * width + (jb + 1) * LANES] = (
                            rows.astype(BF16))
        col += width


def _in_proj(x2, g_mix, cos_t, sin_t, w_in, seq):
    t = x2.shape[0]
    tm = ROW_TILE
    tiles_per_seq = seq // tm
    row = lambda i: (i, 0)
    pos = lambda i: (i % tiles_per_seq, 0)
    out_shape = [jax.ShapeDtypeStruct((t // d, d * w), BF16) for (w, _, _, d) in _IN_SEGMENTS]
    out_specs = [pl.BlockSpec((tm // d, d * w), row) for (w, _, _, d) in _IN_SEGMENTS]
    return pl.pallas_call(
        _in_proj_kernel,
        grid=(t // tm,),
        in_specs=[
            pl.BlockSpec((tm, D_MODEL), row),
            _resident((1, D_MODEL)),
            pl.BlockSpec((tm, LANES), pos),
            pl.BlockSpec((tm, LANES), pos),
            _resident(w_in.shape),
        ],
        out_specs=out_specs,
        out_shape=out_shape,
        scratch_shapes=[pltpu.VMEM((GROUP_WIDTH // LANES, tm, LANES), F32)],
        compiler_params=_params(1),
        name="in_proj",
    )(x2, g_mix, cos_t, sin_t, w_in)


def _band_attn_kernel(q_ref, k_ref, v_ref, o_ref, lse_ref, *, seq_len, n_res):
    n_blocks = seq_len // Q_BLOCK
    qi = lax.broadcasted_iota(jnp.int32, (Q_BLOCK, K_SPAN), 0)
    kj = lax.broadcasted_iota(jnp.int32, (Q_BLOCK, K_SPAN), 1)
    rel = qi - kj
    lane = lax.broadcasted_iota(jnp.int32, (Q_BLOCK, LANES), 1)
    head0 = lane < HEAD_DIM

    def block(i, carry):
        q0 = pl.multiple_of(i * Q_BLOCK, Q_BLOCK)
        k0 = pl.multiple_of(jnp.clip(i * Q_BLOCK - BAND_HALF, 0, seq_len - K_SPAN), BAND_HALF)
        valid = jnp.abs(rel + (q0 - k0)) <= BAND_HALF
        for pair in range(n_res * 2):
            c0 = pair * LANES
            qp = q_ref[0, pl.ds(q0, Q_BLOCK), c0:c0 + LANES]
            kp = k_ref[0, pl.ds(k0, K_SPAN), c0:c0 + LANES]
            vp = v_ref[0, pl.ds(k0, K_SPAN), c0:c0 + LANES]
            v_ones = jnp.concatenate([vp, jnp.ones_like(vp)], axis=-1)
            outs = []
            for h in range(2):
                mask_h = head0 if h == 0 else jnp.logical_not(head0)
                qh = jnp.where(mask_h, qp, jnp.zeros_like(qp))
                s = lax.dot_general(qh, kp, (((1,), (1,)), ((), ())),
                                    preferred_element_type=F32)
                s = jnp.where(valid, s, NEG_INF)
                m = jnp.max(s, axis=-1, keepdims=True)
                p = jnp.exp(s - m).astype(BF16)
                ol = jnp.dot(p, v_ones, preferred_element_type=F32)
                o, l = ol[:, :LANES], ol[:, LANES:]
                outs.append((o / l, m + jnp.log(l)))
            o_pair = jnp.where(head0, outs[0][0], outs[1][0])
            lse_pair = jnp.where(head0, outs[0][1], outs[1][1])
            o_ref[0, pl.ds(q0, Q_BLOCK), c0:c0 + LANES] = o_pair.astype(BF16)
            lse_ref[0, pl.ds(q0, Q_BLOCK), c0:c0 + LANES] = lse_pair
        return carry

    unroll = min(n_blocks, max(1, ATTN_PAIRS_IN_FLIGHT // (2 * n_res)))
    lax.fori_loop(0, n_blocks, block, 0, unroll=unroll)


def _band_attn(q, k, v, batch, seq, dilation):
    seq_len = seq // dilation
    n_res = min(dilation, 4)
    width = dilation * GROUP_WIDTH
    view = lambda a: a.reshape(batch, seq_len, width)
    blk = (1, seq_len, n_res * GROUP_WIDTH)
    idx = lambda b, r: (b, 0, r)
    o, lse = pl.pallas_call(
        functools.partial(_band_attn_kernel, seq_len=seq_len, n_res=n_res),
        grid=(batch, dilation // n_res),
        in_specs=[pl.BlockSpec(blk, idx)] * 3,
        out_specs=[pl.BlockSpec(blk, idx)] * 2,
        out_shape=[jax.ShapeDtypeStruct((batch, seq_len, width), BF16),
                   jax.ShapeDtypeStruct((batch, seq_len, width), F32)],
        compiler_params=_params(2),
        name=f"band_attn_d{dilation}",
    )(view(q), view(k), view(v))
    return o.reshape(batch * seq_len, width), lse.reshape(batch * seq_len, width)


def _log_sigmoid(x):
    return jnp.minimum(x, 0.0) - jnp.log1p(jnp.exp(-jnp.abs(x)))


def _retention_kernel(decf_ref, decb_ref, q_ref, k_ref, v_ref, gate_ref, gret_ref, o_ref,
                      st_ref, y_ref, *, seq):
    c = RET_CHUNK
    n_chunks = seq // c
    pair_w = 2 * RET_QK_DIM
    dv = RET_V_DIM
    lg_f = _log_sigmoid(decf_ref[0])
    lg_b = _log_sigmoid(decb_ref[0])
    lgf_row = lg_f[0:1, :]
    lgb_row = lg_b[0:1, :]
    pos = lax.broadcasted_iota(jnp.int32, (c, pair_w), 0).astype(F32)
    zeta_f = jnp.exp(lgf_row * (c - 1.0 - pos))
    zeta_b = jnp.exp(lgb_row * pos)
    bcast = lambda lg, h, shape: jnp.broadcast_to(
        lg[0:1, h * RET_QK_DIM:h * RET_QK_DIM + 1], shape)
    xi = [jnp.concatenate([jnp.exp(bcast(lg_f, h, (c, dv)) * (pos + 1.0)),
                           jnp.exp(bcast(lg_b, h, (c, dv)) * (c - pos))], axis=-1)
          for h in range(2)]
    st_shape = (pair_w, 2 * RET_V_DIM)
    row_is_h0 = lax.broadcasted_iota(jnp.int32, st_shape, 0) < RET_QK_DIM
    chunk_decay_f = jnp.exp(
        jnp.where(row_is_h0, bcast(lg_f, 0, st_shape), bcast(lg_f, 1, st_shape)) * float(c))
    chunk_decay_b = jnp.exp(
        jnp.where(row_is_h0, bcast(lg_b, 0, st_shape), bcast(lg_b, 1, st_shape)) * float(c))

    def chunk_kv(n, zeta):
        r0 = pl.multiple_of(n * c, c)
        kz = (k_ref[0, pl.ds(r0, c), :].astype(F32) * zeta).astype(BF16)
        return lax.dot_general(kz, v_ref[0, pl.ds(r0, c), :], (((0,), (0,)), ((), ())),
                               preferred_element_type=F32)

    def state_step(j, carry):
        st_f, st_b = carry
        jb = n_chunks - 1 - j
        for h in range(2):
            hs = slice(h * dv, (h + 1) * dv)
            st_ref[j, :, (2 * h) * dv:(2 * h + 1) * dv] = st_f[:, hs].astype(BF16)
            st_ref[jb, :, (2 * h + 1) * dv:(2 * h + 2) * dv] = st_b[:, hs].astype(BF16)
        return (st_f * chunk_decay_f + chunk_kv(j, zeta_f),
                st_b * chunk_decay_b + chunk_kv(jb, zeta_b))

    zero_state = jnp.zeros((pair_w, 2 * RET_V_DIM), F32)
    lax.fori_loop(0, n_chunks, state_step, (zero_state, zero_state), unroll=STATE_UNROLL)

    ci = lax.broadcasted_iota(jnp.int32, (c, c), 0)
    mi = lax.broadcasted_iota(jnp.int32, (c, c), 1)
    delta = (ci - mi).astype(F32)
    causal = ci >= mi
    lane = lax.broadcasted_iota(jnp.int32, (c, pair_w), 1)
    head0 = lane < RET_QK_DIM
    decay_in = []
    for h in range(2):
        decay_in.append(jnp.where(
            causal, jnp.exp(bcast(lg_f, h, (c, c)) * jnp.maximum(delta, 0.0)),
            jnp.exp(bcast(lg_b, h, (c, c)) * jnp.maximum(-delta, 0.0))))

    def mix_chunk(n, carry):
        r0 = pl.multiple_of(n * c, c)
        qp = q_ref[0, pl.ds(r0, c), :]
        kp = k_ref[0, pl.ds(r0, c), :]
        for h in range(2):
            vs = slice(h * dv, (h + 1) * dv)
            mask_h = head0 if h == 0 else jnp.logical_not(head0)
            qh = jnp.where(mask_h, qp, jnp.zeros_like(qp))
            s = lax.dot_general(qh, kp, (((1,), (1,)), ((), ())), preferred_element_type=F32)
            s = (s * decay_in[h]).astype(BF16)
            y = jnp.dot(s, v_ref[0, pl.ds(r0, c), vs], preferred_element_type=F32)
            cross = jnp.dot(qh, st_ref[n, :, 2 * h * dv:(2 * h + 2) * dv],
                            preferred_element_type=F32) * xi[h]
            y_ref[pl.ds(r0, c), vs] = y + cross[:, :dv] + cross[:, dv:]
        return carry

    lax.fori_loop(0, n_chunks, mix_chunk, 0, unroll=MIX_UNROLL)

    def norm_chunk(n, carry):
        r0 = pl.multiple_of(n * c, c)
        for h in range(2):
            vs = slice(h * dv, (h + 1) * dv)
            y = y_ref[pl.ds(r0, c), vs]
            mu = jnp.mean(y, axis=-1, keepdims=True)
            yc = y - mu
            var = jnp.mean(yc * yc, axis=-1, keepdims=True)
            yn = yc * lax.rsqrt(var + EPS) * gret_ref[:, vs]
            gate = gate_ref[0, pl.ds(r0, c), vs].astype(F32)
            o_ref[0, pl.ds(r0, c), vs] = (yn * (gate * _sigmoid(gate))).astype(BF16)
        return carry

    lax.fori_loop(0, n_chunks, norm_chunk, 0, unroll=NORM_UNROLL)


def _retention(qr, kr, vr, gr, dec_f, dec_b, g_ret, batch, seq):
    pairs = RET_HEADS // 2
    n_chunks = seq // RET_CHUNK
    v3 = lambda a: a.reshape(batch, seq, a.shape[-1])
    qk_spec = pl.BlockSpec((1, seq, 2 * RET_QK_DIM), lambda b, p: (b, 0, p))
    v_spec = pl.BlockSpec((1, seq, 2 * RET_V_DIM), lambda b, p: (b, 0, p))
    dec_spec = pl.BlockSpec((1, SUBLANES, LANES), lambda b, p: (p, 0, 0))
    out = pl.pallas_call(
        functools.partial(_retention_kernel, seq=seq),
        grid=(batch, pairs),
        in_specs=[dec_spec, dec_spec, qk_spec, qk_spec, v_spec, v_spec,
                  pl.BlockSpec((1, 2 * RET_V_DIM), lambda b, p: (0, p))],
        out_specs=v_spec,
        out_shape=jax.ShapeDtypeStruct((batch, seq, RET_V_WIDTH), BF16),
        scratch_shapes=[pltpu.VMEM((n_chunks, 2 * RET_QK_DIM, 4 * RET_V_DIM), BF16),
                        pltpu.VMEM((seq, 2 * RET_V_DIM), F32)],
        compiler_params=_params(2),
        name="retention",
    )(dec_f, dec_b, v3(qr), v3(kr), v3(vr), v3(gr), g_ret)
    return out.reshape(batch * seq, RET_V_WIDTH)


def _mem_kv_kernel(mem_ref, g_ref, w_ref, k_ref, v_ref):
    n = _rms(mem_ref[0], g_ref[...]).astype(BF16)
    kv = jnp.dot(n, w_ref[...], preferred_element_type=F32)
    k_ref[0] = kv[:, :MEM_WIDTH].astype(BF16)
    v_ref[0] = kv[:, MEM_WIDTH:].astype(BF16)


def _mem_kv(mem, g_mem, w_mem_kv):
    batch, mem_len, _ = mem.shape
    out = jax.ShapeDtypeStruct((batch, mem_len, MEM_WIDTH), BF16)
    spec = pl.BlockSpec((1, mem_len, MEM_WIDTH), lambda b: (b, 0, 0))
    return pl.pallas_call(
        _mem_kv_kernel,
        grid=(batch,),
        in_specs=[pl.BlockSpec((1, mem_len, D_MODEL), lambda b: (b, 0, 0)),
                  _resident((1, D_MODEL)), _resident(w_mem_kv.shape)],
        out_specs=[spec, spec],
        out_shape=[out, out],
        compiler_params=_params(1),
        name="mem_kv",
    )(mem, g_mem, w_mem_kv)


def _merge_kernel(x_ref, g_ref, o0_ref, o1_ref, o2_ref, l0_ref, l1_ref, l2_ref, yr_ref, qm_ref,
                  km_ref, vm_ref, wg_ref, bg_ref, wpa_ref, wpr_ref, wpm_ref, wo_ref, h_ref,
                  *stage_refs):
    x = x_ref[...]
    tm = x.shape[0]
    n = _rms(x, g_ref[...]).astype(BF16)

    def token_rows(ref, dil, stage_ref):
        if dil == 1:
            return ref[...].astype(F32)
        n_lane_blocks = GROUP_WIDTH // LANES
        for r in range(dil):
            for jb in range(n_lane_blocks):
                c0 = r * GROUP_WIDTH + jb * LANES
                stage_ref[jb, pl.ds(r, tm // dil, stride=dil), :] = (
                    ref[:, c0:c0 + LANES].astype(F32))
        return jnp.concatenate([stage_ref[jb] for jb in range(n_lane_blocks)], axis=-1)

    stages = iter(stage_refs)
    outs, lses = [], []
    for o_ref, l_ref, dil in zip((o0_ref, o1_ref, o2_ref), (l0_ref, l1_ref, l2_ref), _DILATIONS):
        outs.append(token_rows(o_ref, dil, None if dil == 1 else next(stages)))
        lses.append(token_rows(l_ref, dil, None if dil == 1 else next(stages)))
    top = jnp.maximum(jnp.maximum(lses[0], lses[1]), lses[2])
    es = [jnp.exp(l - top) for l in lses]
    inv = 1.0 / (es[0] + es[1] + es[2])
    pa = None
    for g in range(len(outs)):
        y_g = (outs[g] * (es[g] * inv)).astype(BF16)
        part = jnp.dot(y_g, wpa_ref[g * GROUP_WIDTH:(g + 1) * GROUP_WIDTH, :],
                       preferred_element_type=F32)
        pa = part if pa is None else pa + part

    def gate(i):
        z = jnp.dot(n, wg_ref[:, i * D_MODEL:(i + 1) * D_MODEL], preferred_element_type=F32)
        return _sigmoid(z + bg_ref[:, i * D_MODEL:(i + 1) * D_MODEL])

    merged = gate(0) * pa
    merged = merged + gate(1) * jnp.dot(yr_ref[...], wpr_ref[...], preferred_element_type=F32)

    ym = []
    for hd in range(MEM_HEADS):
        cs = slice(hd * MEM_HEAD_DIM, (hd + 1) * MEM_HEAD_DIM)
        s = lax.dot_general(qm_ref[:, cs], km_ref[0, :, cs], (((1,), (1,)), ((), ())),
                            preferred_element_type=F32) * (MEM_HEAD_DIM ** -0.5)
        p = jnp.exp(s - jnp.max(s, axis=-1, keepdims=True))
        p = p / jnp.sum(p, axis=-1, keepdims=True)
        ym.append(jnp.dot(p.astype(BF16), vm_ref[0, :, cs],
                          preferred_element_type=F32).astype(BF16))
    ym = jnp.concatenate(ym, axis=-1)
    merged = merged + gate(2) * jnp.dot(ym, wpm_ref[...], preferred_element_type=F32)

    h_ref[...] = x + jnp.dot(merged.astype(BF16), wo_ref[...], preferred_element_type=F32)


def _merge(x2, g_mix, os_, lses, y_r, q_m, k_m, v_m, w_gate, b_gate, w_pa, w_pr, w_pm, w_out,
           seq):
    t = x2.shape[0]
    tm = ROW_TILE
    tiles_per_seq = seq // tm
    row = lambda i: (i, 0)
    mem_len = k_m.shape[1]
    mem_spec = pl.BlockSpec((1, mem_len, MEM_WIDTH), lambda i: (i // tiles_per_seq, 0, 0))
    group_specs = [pl.BlockSpec((tm // d, d * GROUP_WIDTH), row) for d in _DILATIONS]
    n_stages = 2 * sum(d > 1 for d in _DILATIONS)
    return pl.pallas_call(
        _merge_kernel,
        grid=(t // tm,),
        in_specs=[
            pl.BlockSpec((tm, D_MODEL), row), _resident((1, D_MODEL)),
            *group_specs, *group_specs,
            pl.BlockSpec((tm, RET_V_WIDTH), row),
            pl.BlockSpec((tm, MEM_WIDTH), row),
            mem_spec, mem_spec,
            _resident(w_gate.shape), _resident(b_gate.shape), _resident(w_pa.shape),
            _resident(w_pr.shape), _resident(w_pm.shape), _resident(w_out.shape),
        ],
        out_specs=pl.BlockSpec((tm, D_MODEL), row),
        out_shape=jax.ShapeDtypeStruct((t, D_MODEL), F32),
        scratch_shapes=[pltpu.VMEM((GROUP_WIDTH // LANES, tm, LANES), F32)] * n_stages,
        compiler_params=_params(1),
        name="merge",
    )(x2, g_mix, *os_, *lses, y_r, q_m, k_m, v_m, w_gate, b_gate, w_pa, w_pr, w_pm, w_out)


FF_CHUNK = MXU_WIDTH
HALO = SUBLANES


def _ffn_kernel(h_ref, prev_ref, next_ref, gffn_ref, wup_ref, cw_ref, cb_ref, wdown_ref,
                gfin_ref, out_ref, n_ref, u_ref, act_ref, *, tiles_per_seq):
    i = pl.program_id(0)
    tm = h_ref.shape[0]
    g = gffn_ref[...]
    has_prev = (i % tiles_per_seq) != 0
    has_next = (i % tiles_per_seq) != tiles_per_seq - 1
    n_prev = jnp.where(has_prev, _rms(prev_ref[...], g), 0.0)
    n_next = jnp.where(has_next, _rms(next_ref[...], g), 0.0)
    n_ref[...] = jnp.concatenate([n_prev, _rms(h_ref[...], g), n_next], axis=0).astype(BF16)

    def conv(slot, c0):
        sl = slice(c0, c0 + FF_CHUNK)
        taps = [u_ref[slot, pl.ds(HALO - 1 + k, tm), :] * cw_ref[k:k + 1, sl] for k in range(3)]
        return taps[0] + taps[1] + taps[2] + cb_ref[:, sl]

    for c0 in range(0, D_FF, FF_CHUNK):
        u_ref[0] = jnp.dot(n_ref[...], wup_ref[:, c0:c0 + FF_CHUNK], preferred_element_type=F32)
        u_ref[1] = jnp.dot(n_ref[...], wup_ref[:, D_FF + c0:D_FF + c0 + FF_CHUNK],
                           preferred_element_type=F32)
        a = conv(0, c0)
        b = conv(1, D_FF + c0)
        act_ref[:, c0:c0 + FF_CHUNK] = (a * _sigmoid(a) * b).astype(BF16)
    y = jnp.dot(act_ref[...], wdown_ref[...], preferred_element_type=F32)
    out_ref[...] = _rms(h_ref[...] + y, gfin_ref[...])


def _ffn(h, g_ffn, w_up, conv_w, conv_b, w_down, g_final, seq):
    t = h.shape[0]
    tm = ROW_TILE
    tiles_per_seq = seq // tm
    halo_blocks = tm // HALO
    last_block = t // HALO - 1
    row = lambda i: (i, 0)
    return pl.pallas_call(
        functools.partial(_ffn_kernel, tiles_per_seq=tiles_per_seq),
        grid=(t // tm,),
        in_specs=[
            pl.BlockSpec((tm, D_MODEL), row),
            pl.BlockSpec((HALO, D_MODEL), lambda i: (jnp.maximum(i * halo_blocks - 1, 0), 0)),
            pl.BlockSpec((HALO, D_MODEL),
                         lambda i: (jnp.minimum((i + 1) * halo_blocks, last_block), 0)),
            _resident((1, D_MODEL)), _resident(w_up.shape), _resident(conv_w.shape),
            _resident(conv_b.shape), _resident(w_down.shape), _resident((1, D_MODEL)),
        ],
        out_specs=pl.BlockSpec((tm, D_MODEL), row),
        out_shape=jax.ShapeDtypeStruct((t, D_MODEL), F32),
        scratch_shapes=[pltpu.VMEM((tm + 2 * HALO, D_MODEL), BF16),
                        pltpu.VMEM((2, tm + 2 * HALO, FF_CHUNK), F32),
                        pltpu.VMEM((tm, D_FF), BF16)],
        compiler_params=_params(1),
        name="ffn",
    )(h, h, h, g_ffn, w_up, conv_w, conv_b, w_down, g_final)


def _rotary_tables(seq):
    inv = ROPE_THETA ** (-jnp.arange(0, HEAD_DIM, 2, dtype=F32) / HEAD_DIM)
    ang = jnp.arange(seq, dtype=F32)[:, None] * inv[None, :]
    cos, sin = jnp.cos(ang), jnp.sin(ang)
    reps = LANES // HEAD_DIM
    cos_t = jnp.tile(jnp.concatenate([cos, cos], axis=-1), (1, reps))
    sin_t = jnp.tile(jnp.concatenate([-sin, sin], axis=-1), (1, reps))
    return cos_t, sin_t


def _pair_lanes(v):
    pairs = v.reshape(RET_HEADS // 2, 2, 1)
    lanes = jnp.broadcast_to(pairs, (RET_HEADS // 2, 2, RET_QK_DIM)).reshape(RET_HEADS // 2, 1, LANES)
    return jnp.broadcast_to(lanes, (RET_HEADS // 2, SUBLANES, LANES)).astype(F32)


def _layer(h2, mem, batch, seq, g_mix, w_in, w_mem_kv, g_mem, decay_fwd, decay_bwd, g_ret,
           w_proj_attn, w_proj_ret, w_proj_mem, w_gate, b_gate, w_out):
    bf = lambda w: w.astype(BF16)
    row = lambda v: v.reshape(1, -1).astype(F32)
    cos_t, sin_t = _rotary_tables(seq)
    proj = _in_proj(h2, row(g_mix), cos_t, sin_t, bf(w_in), seq)
    qa, ka, va = proj[0:3], proj[3:6], proj[6:9]
    q_r, k_r, v_r, g_r, q_m = proj[9:]

    os_, lses = [], []
    for g, (_, dilation) in enumerate(ATTN_GROUPS):
        o, lse = _band_attn(qa[g], ka[g], va[g], batch, seq, dilation)
        os_.append(o)
        lses.append(lse)

    y_r = _retention(q_r, k_r, v_r, g_r, _pair_lanes(decay_fwd), _pair_lanes(decay_bwd),
                     row(g_ret), batch, seq)
    k_m, v_m = _mem_kv(mem, row(g_mem), bf(w_mem_kv))
    return _merge(h2, row(g_mix), os_, lses, y_r, q_m, k_m, v_m, bf(w_gate), row(b_gate),
                  bf(w_proj_attn), bf(w_proj_ret), bf(w_proj_mem), bf(w_out), seq)


def kernel(x, mem, g_mix, w_in, w_mem_kv, g_mem, ret_decay_fwd, ret_decay_bwd, g_ret,
           w_proj_attn, w_proj_ret, w_proj_mem, w_gate, b_gate, w_out,
           g_ffn, w_up, conv_w, conv_b, w_down, g_final):
    batch, seq, d = x.shape
    depth = w_in.shape[0]
    assert d == D_MODEL and depth == 1 and seq % ROW_TILE == 0
    h2 = x.reshape(batch * seq, d)
    l = 0
    h2 = _layer(h2, mem, batch, seq, g_mix[l], w_in[l], w_mem_kv[l], g_mem[l],
                ret_decay_fwd[l], ret_decay_bwd[l], g_ret[l], w_proj_attn[l], w_proj_ret[l],
                w_proj_mem[l], w_gate[l], b_gate[l], w_out[l])
    out = _ffn(h2, g_ffn[l].reshape(1, -1), w_up[l].astype(BF16), conv_w[l],
               conv_b[l].reshape(1, -1), w_down[l].astype(BF16), g_final.reshape(1, -1), seq)
    return out.reshape(batch, seq, d)
```

```python
import functools
import math

import jax
import jax.numpy as jnp
from jax import lax
from jax.experimental import pallas as pl
from jax.experimental.pallas import tpu as pltpu

D_MODEL = 1024
HEAD_DIM = 64
ATTN_GROUPS = ((128, 1), (512, 4), (2048, 16))
GROUP_WIDTH = 4 * HEAD_DIM
ATTN_WIDTH = 3 * GROUP_WIDTH
BAND_HALF = 64
RET_HEADS = 6
RET_QK_DIM = 64
RET_V_DIM = 128
RET_QK_WIDTH = RET_HEADS * RET_QK_DIM
RET_V_WIDTH = RET_HEADS * RET_V_DIM
MEM_HEADS = 4
MEM_HEAD_DIM = 128
MEM_WIDTH = MEM_HEADS * MEM_HEAD_DIM
D_FF = 2816
ROPE_THETA = 10000.0
EPS = 1e-6
NEG_INF = -1e30
LOG2_E = math.log2(math.e)

LANES = 128
SUBLANES = 8
MXU_WIDTH = 256
VMEM_LIMIT = 56 * 1024 * 1024

ROW_TILE = 512
RET_CHUNK = 128
Q_BLOCK = 128
K_SPAN = Q_BLOCK + 2 * BAND_HALF
STATE_UNROLL = 4
MIX_UNROLL = 4
NORM_UNROLL = 4
ATTN_PAIRS_IN_FLIGHT = 8

BF16 = jnp.bfloat16
F32 = jnp.float32


def _params(n_grid_axes):
    return pltpu.CompilerParams(
        dimension_semantics=("arbitrary",) * n_grid_axes, vmem_limit_bytes=VMEM_LIMIT)


def _resident(shape):
    nd = len(shape)
    return pl.BlockSpec(shape, lambda *_: (0,) * nd, pipeline_mode=pl.Buffered(1))


def _rms(x, g):
    return x * lax.rsqrt(jnp.mean(x * x, axis=-1, keepdims=True) + EPS) * g


def _sigmoid(x):
    return 1.0 / (1.0 + jnp.exp(-x))


_DILATIONS = tuple(d for _, d in ATTN_GROUPS)
_IN_SEGMENTS = (
    *[(GROUP_WIDTH, True, HEAD_DIM ** -0.5 * LOG2_E, d) for d in _DILATIONS],
    *[(GROUP_WIDTH, True, 1.0, d) for d in _DILATIONS],
    *[(GROUP_WIDTH, False, 1.0, d) for d in _DILATIONS],
    (RET_QK_WIDTH, True, 1.0, 1),
    (RET_QK_WIDTH, True, RET_QK_DIM ** -0.5, 1),
    (RET_V_WIDTH, False, 1.0, 1),
    (RET_V_WIDTH, False, 1.0, 1),
    (MEM_WIDTH, False, 1.0, 1),
)


def _in_proj_kernel(x_ref, g_ref, cos_ref, sin_ref, w_ref, *refs):
    out_refs, stage_ref = refs[:-1], refs[-1]
    tm = x_ref.shape[0]
    n = _rms(x_ref[...], g_ref[...]).astype(BF16)
    cos = cos_ref[...]
    sin = sin_ref[...]
    lane = lax.broadcasted_iota(jnp.int32, cos.shape, 1)
    low_half = (lane & (HEAD_DIM // 2)) == 0

    def rotate(a):
        partner = jnp.where(low_half, pltpu.roll(a, LANES - HEAD_DIM // 2, 1),
                            pltpu.roll(a, HEAD_DIM // 2, 1))
        return a * cos + partner * sin

    col = 0
    for out_ref, (width, rotary, scale, dil) in zip(out_refs, _IN_SEGMENTS):
        for c in range(0, width, MXU_WIDTH):
            cw = min(MXU_WIDTH, width - c)
            acc = jnp.dot(n, w_ref[:, col + c:col + c + cw], preferred_element_type=F32)
            for j in range(0, cw, LANES):
                a = acc[:, j:j + LANES]
                if rotary:
                    a = rotate(a)
                if scale != 1.0:
                    a = a * scale
                if dil == 1:
                    out_ref[:, c + j:c + j + LANES] = a.astype(BF16)
                else:
                    stage_ref[j // LANES] = a
            if dil > 1:
                assert width == cw == stage_ref.shape[0] * LANES
                for r in range(dil):
                    for jb in range(width // LANES):
                        rows = stage_ref[jb, pl.ds(r, tm // dil, stride=dil), :]
                        out_ref[:, r * width + jb * LANES:r * width + (jb + 1) * LANES] = (
                            rows.astype(BF16))
        col += width


def _in_proj(x2, g_mix, cos_t, sin_t, w_in, seq):
    t = x2.shape[0]
    tm = ROW_TILE
    tiles_per_seq = seq // tm
    row = lambda i: (i, 0)
    pos = lambda i: (i % tiles_per_seq, 0)
    out_shape = [jax.ShapeDtypeStruct((t // d, d * w), BF16) for (w, _, _, d) in _IN_SEGMENTS]
    out_specs = [pl.BlockSpec((tm // d, d * w), row) for (w, _, _, d) in _IN_SEGMENTS]
    return pl.pallas_call(
        _in_proj_kernel,
        grid=(t // tm,),
        in_specs=[
            pl.BlockSpec((tm, D_MODEL), row),
            _resident((1, D_MODEL)),
            pl.BlockSpec((tm, LANES), pos),
            pl.BlockSpec((tm, LANES), pos),
            _resident(w_in.shape),
        ],
        out_specs=out_specs,
        out_shape=out_shape,
        scratch_shapes=[pltpu.VMEM((GROUP_WIDTH // LANES, tm, LANES), F32)],
        compiler_params=_params(1),
        name="in_proj",
    )(x2, g_mix, cos_t, sin_t, w_in)


def _band_attn_kernel(q_ref, k_ref, v_ref, o_ref, lse_ref, *, seq_len, n_res):
    n_blocks = seq_len // Q_BLOCK
    qi = lax.broadcasted_iota(jnp.int32, (Q_BLOCK, K_SPAN), 0)
    kj = lax.broadcasted_iota(jnp.int32, (Q_BLOCK, K_SPAN), 1)
    rel = qi - kj
    lane = lax.broadcasted_iota(jnp.int32, (Q_BLOCK, LANES), 1)
    head0 = lane < HEAD_DIM

    def band_bias(q0, k0):
        return jnp.where(jnp.abs(rel + (q0 - k0)) <= BAND_HALF, 0.0, NEG_INF)

    def block(q0, k0, bias):
        for pair in range(n_res * 2):
            c0 = pair * LANES
            qp = q_ref[0, pl.ds(q0, Q_BLOCK), c0:c0 + LANES]
            kp = k_ref[0, pl.ds(k0, K_SPAN), c0:c0 + LANES]
            vp = v_ref[0, pl.ds(k0, K_SPAN), c0:c0 + LANES]
            v_ones = jnp.concatenate([vp, jnp.ones_like(vp)], axis=-1)
            outs = []
            for h in range(2):
                mask_h = head0 if h == 0 else jnp.logical_not(head0)
                qh = jnp.where(mask_h, qp, jnp.zeros_like(qp))
                s = lax.dot_general(qh, kp, (((1,), (1,)), ((), ())),
                                    preferred_element_type=F32) + bias
                m = jnp.max(s, axis=-1, keepdims=True)
                p = jnp.exp2(s - m).astype(BF16)
                ol = jnp.dot(p, v_ones, preferred_element_type=F32)
                o, l = ol[:, :LANES], ol[:, LANES:]
                outs.append((o / l, m + jnp.log2(l)))
            o_pair = jnp.where(head0, outs[0][0], outs[1][0])
            lse_pair = jnp.where(head0, outs[0][1], outs[1][1])
            o_ref[0, pl.ds(q0, Q_BLOCK), c0:c0 + LANES] = o_pair.astype(BF16)
            lse_ref[0, pl.ds(q0, Q_BLOCK), c0:c0 + LANES] = lse_pair

    last_q0 = seq_len - Q_BLOCK
    last_k0 = seq_len - K_SPAN
    block(0, 0, band_bias(0, 0))
    if n_blocks > 2:
        mid_bias = band_bias(BAND_HALF, 0)

        def interior(i, carry):
            q0 = pl.multiple_of(i * Q_BLOCK, Q_BLOCK)
            k0 = pl.multiple_of(i * Q_BLOCK - BAND_HALF, BAND_HALF)
            block(q0, k0, mid_bias)
            return carry

        unroll = max(1, ATTN_PAIRS_IN_FLIGHT // (2 * n_res))
        lax.fori_loop(1, n_blocks - 1, interior, 0, unroll=unroll)
    block(last_q0, last_k0, band_bias(last_q0, last_k0))


def _band_attn(q, k, v, batch, seq, dilation):
    seq_len = seq // dilation
    n_res = min(dilation, 4)
    width = dilation * GROUP_WIDTH
    view = lambda a: a.reshape(batch, seq_len, width)
    blk = (1, seq_len, n_res * GROUP_WIDTH)
    idx = lambda b, r: (b, 0, r)
    o, lse = pl.pallas_call(
        functools.partial(_band_attn_kernel, seq_len=seq_len, n_res=n_res),
        grid=(batch, dilation // n_res),
        in_specs=[pl.BlockSpec(blk, idx)] * 3,
        out_specs=[pl.BlockSpec(blk, idx)] * 2,
        out_shape=[jax.ShapeDtypeStruct((batch, seq_len, width), BF16),
                   jax.ShapeDtypeStruct((batch, seq_len, width), F32)],
        compiler_params=_params(2),
        name=f"band_attn_d{dilation}",
    )(view(q), view(k), view(v))
    return o.reshape(batch * seq_len, width), lse.reshape(batch * seq_len, width)


def _log_sigmoid(x):
    return jnp.minimum(x, 0.0) - jnp.log1p(jnp.exp(-jnp.abs(x)))


def _retention_kernel(decf_ref, decb_ref, q_ref, k_ref, v_ref, gate_ref, gret_ref, o_ref,
                      st_ref, s_ref, y_ref, *, seq):
    c = RET_CHUNK
    n_chunks = seq // c
    pair_w = 2 * RET_QK_DIM
    dv = RET_V_DIM
    lg_f = _log_sigmoid(decf_ref[0])
    lg_b = _log_sigmoid(decb_ref[0])
    lgf_row = lg_f[0:1, :]
    lgb_row = lg_b[0:1, :]
    pos = lax.broadcasted_iota(jnp.int32, (c, pair_w), 0).astype(F32)
    zeta_f = jnp.exp(lgf_row * (c - 1.0 - pos))
    zeta_b = jnp.exp(lgb_row * pos)
    bcast = lambda lg, h, shape: jnp.broadcast_to(
        lg[0:1, h * RET_QK_DIM:h * RET_QK_DIM + 1], shape)
    xi = [jnp.concatenate([jnp.exp(bcast(lg_f, h, (c, dv)) * (pos + 1.0)),
                           jnp.exp(bcast(lg_b, h, (c, dv)) * (c - pos))], axis=-1)
          for h in range(2)]
    st_shape = (pair_w, 2 * RET_V_DIM)
    row_is_h0 = lax.broadcasted_iota(jnp.int32, st_shape, 0) < RET_QK_DIM
    chunk_decay_f = jnp.exp(
        jnp.where(row_is_h0, bcast(lg_f, 0, st_shape), bcast(lg_f, 1, st_shape)) * float(c))
    chunk_decay_b = jnp.exp(
        jnp.where(row_is_h0, bcast(lg_b, 0, st_shape), bcast(lg_b, 1, st_shape)) * float(c))

    def row0(n):
        return n * c if isinstance(n, int) else pl.multiple_of(n * c, c)

    def chunk_kv(n, zeta):
        r0 = row0(n)
        kz = (k_ref[0, pl.ds(r0, c), :].astype(F32) * zeta).astype(BF16)
        return lax.dot_general(kz, v_ref[0, pl.ds(r0, c), :], (((0,), (0,)), ((), ())),
                               preferred_element_type=F32)

    ci = lax.broadcasted_iota(jnp.int32, (c, c), 0)
    mi = lax.broadcasted_iota(jnp.int32, (c, c), 1)
    delta = (ci - mi).astype(F32)
    causal = ci >= mi
    lane = lax.broadcasted_iota(jnp.int32, (c, pair_w), 1)
    head0 = lane < RET_QK_DIM
    decay_in = []
    for h in range(2):
        decay_in.append(jnp.where(
            causal, jnp.exp(bcast(lg_f, h, (c, c)) * jnp.maximum(delta, 0.0)),
            jnp.exp(bcast(lg_b, h, (c, c)) * jnp.maximum(-delta, 0.0))))

    def head_query(qp, h):
        mask_h = head0 if h == 0 else jnp.logical_not(head0)
        return jnp.where(mask_h, qp, jnp.zeros_like(qp))

    def score_chunk(n):
        r0 = row0(n)
        qp = q_ref[0, pl.ds(r0, c), :]
        kp = k_ref[0, pl.ds(r0, c), :]
        for h in range(2):
            s = lax.dot_general(head_query(qp, h), kp, (((1,), (1,)), ((), ())),
                                preferred_element_type=F32)
            s_ref[pl.ds(r0, c), h * c:(h + 1) * c] = (s * decay_in[h]).astype(BF16)

    def state_step(j, carry):
        st_f, st_b = carry
        jb = n_chunks - 1 - j
        for h in range(2):
            hs = slice(h * dv, (h + 1) * dv)
            st_ref[j, :, (2 * h) * dv:(2 * h + 1) * dv] = st_f[:, hs].astype(BF16)
            st_ref[jb, :, (2 * h + 1) * dv:(2 * h + 2) * dv] = st_b[:, hs].astype(BF16)
        st_f = st_f * chunk_decay_f + chunk_kv(j, zeta_f)
        st_b = st_b * chunk_decay_b + chunk_kv(jb, zeta_b)
        score_chunk(j)
        return st_f, st_b

    zero_state = jnp.zeros((pair_w, 2 * RET_V_DIM), F32)
    lax.fori_loop(0, n_chunks, state_step, (zero_state, zero_state), unroll=STATE_UNROLL)

    def mix_chunk(n):
        r0 = row0(n)
        qp = q_ref[0, pl.ds(r0, c), :]
        for h in range(2):
            vs = slice(h * dv, (h + 1) * dv)
            y = jnp.dot(s_ref[pl.ds(r0, c), h * c:(h + 1) * c], v_ref[0, pl.ds(r0, c), vs],
                        preferred_element_type=F32)
            cross = jnp.dot(head_query(qp, h), st_ref[n, :, 2 * h * dv:(2 * h + 2) * dv],
                            preferred_element_type=F32) * xi[h]
            y_ref[pl.ds(r0, c), vs] = y + cross[:, :dv] + cross[:, dv:]

    def norm_chunk(n):
        r0 = row0(n)
        for h in range(2):
            vs = slice(h * dv, (h + 1) * dv)
            y = y_ref[pl.ds(r0, c), vs]
            mu = jnp.mean(y, axis=-1, keepdims=True)
            yc = y - mu
            var = jnp.mean(yc * yc, axis=-1, keepdims=True)
            yn = yc * lax.rsqrt(var + EPS) * gret_ref[:, vs]
            gate = gate_ref[0, pl.ds(r0, c), vs].astype(F32)
            o_ref[0, pl.ds(r0, c), vs] = (yn * (gate * _sigmoid(gate))).astype(BF16)

    n_batches = n_chunks // MIX_UNROLL

    def mix_norm(i, carry):
        for u in range(MIX_UNROLL):
            mix_chunk(i * MIX_UNROLL + u)
            norm_chunk((i - 1) * MIX_UNROLL + u)
        return carry

    for u in range(MIX_UNROLL):
        mix_chunk(u)
    lax.fori_loop(1, n_batches, mix_norm, 0)
    for u in range(MIX_UNROLL):
        norm_chunk((n_batches - 1) * MIX_UNROLL + u)


def _retention(qr, kr, vr, gr, dec_f, dec_b, g_ret, batch, seq):
    pairs = RET_HEADS // 2
    n_chunks = seq // RET_CHUNK
    v3 = lambda a: a.reshape(batch, seq, a.shape[-1])
    qk_spec = pl.BlockSpec((1, seq, 2 * RET_QK_DIM), lambda b, p: (b, 0, p))
    v_spec = pl.BlockSpec((1, seq, 2 * RET_V_DIM), lambda b, p: (b, 0, p))
    dec_spec = pl.BlockSpec((1, SUBLANES, LANES), lambda b, p: (p, 0, 0))
    out = pl.pallas_call(
        functools.partial(_retention_kernel, seq=seq),
        grid=(batch, pairs),
        in_specs=[dec_spec, dec_spec, qk_spec, qk_spec, v_spec, v_spec,
                  pl.BlockSpec((1, 2 * RET_V_DIM), lambda b, p: (0, p))],
        out_specs=v_spec,
        out_shape=jax.ShapeDtypeStruct((batch, seq, RET_V_WIDTH), BF16),
        scratch_shapes=[pltpu.VMEM((n_chunks, 2 * RET_QK_DIM, 4 * RET_V_DIM), BF16),
                        pltpu.VMEM((seq, 2 * RET_CHUNK), BF16),
                        pltpu.VMEM((seq, 2 * RET_V_DIM), F32)],
        compiler_params=_params(2),
        name="retention",
    )(dec_f, dec_b, v3(qr), v3(kr), v3(vr), v3(gr), g_ret)
    return out.reshape(batch * seq, RET_V_WIDTH)


def _mem_kv_kernel(mem_ref, g_ref, w_ref, k_ref, v_ref):
    n = _rms(mem_ref[0], g_ref[...]).astype(BF16)
    kv = jnp.dot(n, w_ref[...], preferred_element_type=F32)
    k_ref[0] = kv[:, :MEM_WIDTH].astype(BF16)
    v_ref[0] = kv[:, MEM_WIDTH:].astype(BF16)


def _mem_kv(mem, g_mem, w_mem_kv):
    batch, mem_len, _ = mem.shape
    out = jax.ShapeDtypeStruct((batch, mem_len, MEM_WIDTH), BF16)
    spec = pl.BlockSpec((1, mem_len, MEM_WIDTH), lambda b: (b, 0, 0))
    return pl.pallas_call(
        _mem_kv_kernel,
        grid=(batch,),
        in_specs=[pl.BlockSpec((1, mem_len, D_MODEL), lambda b: (b, 0, 0)),
                  _resident((1, D_MODEL)), _resident(w_mem_kv.shape)],
        out_specs=[spec, spec],
        out_shape=[out, out],
        compiler_params=_params(1),
        name="mem_kv",
    )(mem, g_mem, w_mem_kv)


def _merge_kernel(x_ref, g_ref, o0_ref, o1_ref, o2_ref, l0_ref, l1_ref, l2_ref, yr_ref, qm_ref,
                  km_ref, vm_ref, wg_ref, bg_ref, wpa_ref, wpr_ref, wpm_ref, wo_ref, h_ref,
                  *stage_refs):
    x = x_ref[...]
    tm = x.shape[0]
    n = _rms(x, g_ref[...]).astype(BF16)

    def token_rows(ref, dil, stage_ref):
        if dil == 1:
            return ref[...].astype(F32)
        n_lane_blocks = GROUP_WIDTH // LANES
        for r in range(dil):
            for jb in range(n_lane_blocks):
                c0 = r * GROUP_WIDTH + jb * LANES
                stage_ref[jb, pl.ds(r, tm // dil, stride=dil), :] = (
                    ref[:, c0:c0 + LANES].astype(F32))
        return jnp.concatenate([stage_ref[jb] for jb in range(n_lane_blocks)], axis=-1)

    stages = iter(stage_refs)
    outs, lses = [], []
    for o_ref, l_ref, dil in zip((o0_ref, o1_ref, o2_ref), (l0_ref, l1_ref, l2_ref), _DILATIONS):
        outs.append(token_rows(o_ref, dil, None if dil == 1 else next(stages)))
        lses.append(token_rows(l_ref, dil, None if dil == 1 else next(stages)))
    top = jnp.maximum(jnp.maximum(lses[0], lses[1]), lses[2])
    es = [jnp.exp2(l - top) for l in lses]
    inv = 1.0 / (es[0] + es[1] + es[2])
    pa = None
    for g in range(len(outs)):
        y_g = (outs[g] * (es[g] * inv)).astype(BF16)
        part = jnp.dot(y_g, wpa_ref[g * GROUP_WIDTH:(g + 1) * GROUP_WIDTH, :],
                       preferred_element_type=F32)
        pa = part if pa is None else pa + part

    def gate(i):
        z = jnp.dot(n, wg_ref[:, i * D_MODEL:(i + 1) * D_MODEL], preferred_element_type=F32)
        return _sigmoid(z + bg_ref[:, i * D_MODEL:(i + 1) * D_MODEL])

    merged = gate(0) * pa
    merged = merged + gate(1) * jnp.dot(yr_ref[...], wpr_ref[...], preferred_element_type=F32)

    ym = []
    for hd in range(MEM_HEADS):
        cs = slice(hd * MEM_HEAD_DIM, (hd + 1) * MEM_HEAD_DIM)
        s = lax.dot_general(qm_ref[:, cs], km_ref[0, :, cs], (((1,), (1,)), ((), ())),
                            preferred_element_type=F32) * (MEM_HEAD_DIM ** -0.5)
        p = jnp.exp(s - jnp.max(s, axis=-1, keepdims=True))
        p = p / jnp.sum(p, axis=-1, keepdims=True)
        ym.append(jnp.dot(p.astype(BF16), vm_ref[0, :, cs],
                          preferred_element_type=F32).astype(BF16))
    ym = jnp.concatenate(ym, axis=-1)
    merged = merged + gate(2) * jnp.dot(ym, wpm_ref[...], preferred_element_type=F32)

    h_ref[...] = x + jnp.dot(merged.astype(BF16), wo_ref[...], preferred_element_type=F32)


def _merge(x2, g_mix, os_, lses, y_r, q_m, k_m, v_m, w_gate, b_gate, w_pa, w_pr, w_pm, w_out,
           seq):
    t = x2.shape[0]
    tm = ROW_TILE
    tiles_per_seq = seq // tm
    row = lambda i: (i, 0)
    mem_len = k_m.shape[1]
    mem_spec = pl.BlockSpec((1, mem_len, MEM_WIDTH), lambda i: (i // tiles_per_seq, 0, 0))
    group_specs = [pl.BlockSpec((tm // d, d * GROUP_WIDTH), row) for d in _DILATIONS]
    n_stages = 2 * sum(d > 1 for d in _DILATIONS)
    return pl.pallas_call(
        _merge_kernel,
        grid=(t // tm,),
        in_specs=[
            pl.BlockSpec((tm, D_MODEL), row), _resident((1, D_MODEL)),
            *group_specs, *group_specs,
            pl.BlockSpec((tm, RET_V_WIDTH), row),
            pl.BlockSpec((tm, MEM_WIDTH), row),
            mem_spec, mem_spec,
            _resident(w_gate.shape), _resident(b_gate.shape), _resident(w_pa.shape),
            _resident(w_pr.shape), _resident(w_pm.shape), _resident(w_out.shape),
        ],
        out_specs=pl.BlockSpec((tm, D_MODEL), row),
        out_shape=jax.ShapeDtypeStruct((t, D_MODEL), F32),
        scratch_shapes=[pltpu.VMEM((GROUP_WIDTH // LANES, tm, LANES), F32)] * n_stages,
        compiler_params=_params(1),
        name="merge",
    )(x2, g_mix, *os_, *lses, y_r, q_m, k_m, v_m, w_gate, b_gate, w_pa, w_pr, w_pm, w_out)


FF_CHUNK = MXU_WIDTH
FF_GROUP = 4 * FF_CHUNK
HALO = SUBLANES


def _ffn_kernel(h_ref, prev_ref, next_ref, gffn_ref, wup_ref, cw_ref, cb_ref, wdown_ref,
                gfin_ref, out_ref, n_ref, u_ref, act_ref, *, tiles_per_seq):
    i = pl.program_id(0)
    tm = h_ref.shape[0]
    g = gffn_ref[...]
    has_prev = (i % tiles_per_seq) != 0
    has_next = (i % tiles_per_seq) != tiles_per_seq - 1
    n_prev = jnp.where(has_prev, _rms(prev_ref[...], g), 0.0)
    n_next = jnp.where(has_next, _rms(next_ref[...], g), 0.0)
    n_ref[...] = jnp.concatenate([n_prev, _rms(h_ref[...], g), n_next], axis=0).astype(BF16)

    def conv(slot, c0):
        sl = slice(c0, c0 + FF_CHUNK)
        taps = [u_ref[slot, pl.ds(HALO - 1 + k, tm), :] * cw_ref[k:k + 1, sl] for k in range(3)]
        return taps[0] + taps[1] + taps[2] + cb_ref[:, sl]

    def up(c0):
        sa = 2 * (c0 // FF_CHUNK)
        u_ref[sa] = jnp.dot(n_ref[...], wup_ref[:, c0:c0 + FF_CHUNK],
                            preferred_element_type=F32)
        u_ref[sa + 1] = jnp.dot(n_ref[...], wup_ref[:, D_FF + c0:D_FF + c0 + FF_CHUNK],
                                preferred_element_type=F32)

    def gate(c0):
        sa = 2 * (c0 // FF_CHUNK)
        a = conv(sa, c0)
        b = conv(sa + 1, D_FF + c0)
        act_ref[:, c0:c0 + FF_CHUNK] = (a * _sigmoid(a) * b).astype(BF16)

    groups = [(g0, min(g0 + FF_GROUP, D_FF)) for g0 in range(0, D_FF, FF_GROUP)]
    y = h_ref[...]
    for step in range(len(groups) + 2):
        if step < len(groups):
            for c0 in range(*groups[step], FF_CHUNK):
                up(c0)
        if 2 <= step:
            g0, g1 = groups[step - 2]
            y = y + jnp.dot(act_ref[:, g0:g1], wdown_ref[g0:g1, :],
                            preferred_element_type=F32)
        if 1 <= step <= len(groups):
            for c0 in range(*groups[step - 1], FF_CHUNK):
                gate(c0)
    out_ref[...] = _rms(y, gfin_ref[...])


def _ffn(h, g_ffn, w_up, conv_w, conv_b, w_down, g_final, seq):
    t = h.shape[0]
    tm = ROW_TILE
    tiles_per_seq = seq // tm
    halo_blocks = tm // HALO
    last_block = t // HALO - 1
    row = lambda i: (i, 0)
    return pl.pallas_call(
        functools.partial(_ffn_kernel, tiles_per_seq=tiles_per_seq),
        grid=(t // tm,),
        in_specs=[
            pl.BlockSpec((tm, D_MODEL), row),
            pl.BlockSpec((HALO, D_MODEL), lambda i: (jnp.maximum(i * halo_blocks - 1, 0), 0)),
            pl.BlockSpec((HALO, D_MODEL),
                         lambda i: (jnp.minimum((i + 1) * halo_blocks, last_block), 0)),
            _resident((1, D_MODEL)), _resident(w_up.shape), _resident(conv_w.shape),
            _resident(conv_b.shape), _resident(w_down.shape), _resident((1, D_MODEL)),
        ],
        out_specs=pl.BlockSpec((tm, D_MODEL), row),
        out_shape=jax.ShapeDtypeStruct((t, D_MODEL), F32),
        scratch_shapes=[pltpu.VMEM((tm + 2 * HALO, D_MODEL), BF16),
                        pltpu.VMEM((2 * (D_FF // FF_CHUNK), tm + 2 * HALO, FF_CHUNK), F32),
                        pltpu.VMEM((tm, D_FF), BF16)],
        compiler_params=_params(1),
        name="ffn",
    )(h, h, h, g_ffn, w_up, conv_w, conv_b, w_down, g_final)


def _rotary_tables(seq):
    inv = ROPE_THETA ** (-jnp.arange(0, HEAD_DIM, 2, dtype=F32) / HEAD_DIM)
    ang = jnp.arange(seq, dtype=F32)[:, None] * inv[None, :]
    cos, sin = jnp.cos(ang), jnp.sin(ang)
    reps = LANES // HEAD_DIM
    cos_t = jnp.tile(jnp.concatenate([cos, cos], axis=-1), (1, reps))
    sin_t = jnp.tile(jnp.concatenate([-sin, sin], axis=-1), (1, reps))
    return cos_t, sin_t


def _pair_lanes(v):
    pairs = v.reshape(RET_HEADS // 2, 2, 1)
    lanes = jnp.broadcast_to(pairs, (RET_HEADS // 2, 2, RET_QK_DIM)).reshape(RET_HEADS // 2, 1, LANES)
    return jnp.broadcast_to(lanes, (RET_HEADS // 2, SUBLANES, LANES)).astype(F32)


def _layer(h2, mem, batch, seq, g_mix, w_in, w_mem_kv, g_mem, decay_fwd, decay_bwd, g_ret,
           w_proj_attn, w_proj_ret, w_proj_mem, w_gate, b_gate, w_out):
    bf = lambda w: w.astype(BF16)
    row = lambda v: v.reshape(1, -1).astype(F32)
    cos_t, sin_t = _rotary_tables(seq)
    proj = _in_proj(h2, row(g_mix), cos_t, sin_t, bf(w_in), seq)
    qa, ka, va = proj[0:3], proj[3:6], proj[6:9]
    q_r, k_r, v_r, g_r, q_m = proj[9:]

    os_, lses = [], []
    for g, (_, dilation) in enumerate(ATTN_GROUPS):
        o, lse = _band_attn(qa[g], ka[g], va[g], batch, seq, dilation)
        os_.append(o)
        lses.append(lse)

    y_r = _retention(q_r, k_r, v_r, g_r, _pair_lanes(decay_fwd), _pair_lanes(decay_bwd),
                     row(g_ret), batch, seq)
    k_m, v_m = _mem_kv(mem, row(g_mem), bf(w_mem_kv))
    return _merge(h2, row(g_mix), os_, lses, y_r, q_m, k_m, v_m, bf(w_gate), row(b_gate),
                  bf(w_proj_attn), bf(w_proj_ret), bf(w_proj_mem), bf(w_out), seq)


def kernel(x, mem, g_mix, w_in, w_mem_kv, g_mem, ret_decay_fwd, ret_decay_bwd, g_ret,
           w_proj_attn, w_proj_ret, w_proj_mem, w_gate, b_gate, w_out,
           g_ffn, w_up, conv_w, conv_b, w_down, g_final):
    batch, seq, d = x.shape
    depth = w_in.shape[0]
    assert d == D_MODEL and depth == 1 and seq % ROW_TILE == 0
    h2 = x.reshape(batch * seq, d)
    l = 0
    h2 = _layer(h2, mem, batch, seq, g_mix[l], w_in[l], w_mem_kv[l], g_mem[l],
                ret_decay_fwd[l], ret_decay_bwd[l], g_ret[l], w_proj_attn[l], w_proj_ret[l],
                w_proj_mem[l], w_gate[l], b_gate[l], w_out[l])
    out = _ffn(h2, g_ffn[l].reshape(1, -1), w_up[l].astype(BF16), conv_w[l],
               conv_b[l].reshape(1, -1), w_down[l].astype(BF16), g_final.reshape(1, -1), seq)
    return out.reshape(batch, seq, d)
```

```python
import functools
import math

import jax
import jax.numpy as jnp
from jax import lax
from jax.experimental import pallas as pl
from jax.experimental.pallas import tpu as pltpu

D_MODEL = 1024
HEAD_DIM = 64
ATTN_GROUPS = ((128, 1), (512, 4), (2048, 16))
GROUP_WIDTH = 4 * HEAD_DIM
ATTN_WIDTH = 3 * GROUP_WIDTH
BAND_HALF = 64
RET_HEADS = 6
RET_QK_DIM = 64
RET_V_DIM = 128
RET_QK_WIDTH = RET_HEADS * RET_QK_DIM
RET_V_WIDTH = RET_HEADS * RET_V_DIM
MEM_HEADS = 4
MEM_HEAD_DIM = 128
MEM_WIDTH = MEM_HEADS * MEM_HEAD_DIM
D_FF = 2816
ROPE_THETA = 10000.0
EPS = 1e-6
NEG_INF = -1e30
LOG2_E = math.log2(math.e)

LANES = 128
SUBLANES = 8
MXU_WIDTH = 256
VMEM_LIMIT = 56 * 1024 * 1024

ROW_TILE = 512
WIDE_ROW_TILE = 1024
RET_CHUNK = 256
Q_BLOCK = 128
K_SPAN = Q_BLOCK + 2 * BAND_HALF
STATE_UNROLL = 4
MIX_UNROLL = 4
ATTN_PAIRS_IN_FLIGHT = 8

BF16 = jnp.bfloat16
F32 = jnp.float32


def _params(n_grid_axes):
    return pltpu.CompilerParams(
        dimension_semantics=("arbitrary",) * n_grid_axes, vmem_limit_bytes=VMEM_LIMIT)


def _resident(shape):
    nd = len(shape)
    return pl.BlockSpec(shape, lambda *_: (0,) * nd, pipeline_mode=pl.Buffered(1))


def _rms(x, g):
    return x * lax.rsqrt(jnp.mean(x * x, axis=-1, keepdims=True) + EPS) * g


def _sigmoid(x):
    return 1.0 / (1.0 + jnp.exp(-x))


_DILATIONS = tuple(d for _, d in ATTN_GROUPS)
RET_PAIRS = RET_HEADS // 2
_IN_SEGMENTS = (
    *[(GROUP_WIDTH, True, HEAD_DIM ** -0.5 * LOG2_E, d, 1) for d in _DILATIONS],
    *[(GROUP_WIDTH, True, 1.0, d, 1) for d in _DILATIONS],
    *[(GROUP_WIDTH, False, 1.0, d, 1) for d in _DILATIONS],
    (RET_QK_WIDTH, True, 1.0, 1, RET_PAIRS),
    (RET_QK_WIDTH, True, RET_QK_DIM ** -0.5, 1, RET_PAIRS),
    (RET_V_WIDTH, False, 1.0, 1, RET_PAIRS),
    (RET_V_WIDTH, False, 1.0, 1, RET_PAIRS),
    (MEM_WIDTH, False, 1.0, 1, 1),
)


def _in_proj_kernel(x_ref, g_ref, cos_ref, sin_ref, w_ref, *refs):
    out_refs, stage_ref = refs[:-1], refs[-1]
    tm = x_ref.shape[0]
    n = _rms(x_ref[...], g_ref[...]).astype(BF16)
    cos = cos_ref[...]
    sin = sin_ref[...]
    lane = lax.broadcasted_iota(jnp.int32, cos.shape, 1)
    low_half = (lane & (HEAD_DIM // 2)) == 0

    def rotate(a):
        partner = jnp.where(low_half, pltpu.roll(a, LANES - HEAD_DIM // 2, 1),
                            pltpu.roll(a, HEAD_DIM // 2, 1))
        return a * cos + partner * sin

    col = 0
    for out_ref, (width, rotary, scale, dil, parts) in zip(out_refs, _IN_SEGMENTS):
        for c in range(0, width, MXU_WIDTH):
            cw = min(MXU_WIDTH, width - c)
            acc = jnp.dot(n, w_ref[:, col + c:col + c + cw], preferred_element_type=F32)
            for j in range(0, cw, LANES):
                a = acc[:, j:j + LANES]
                if rotary:
                    a = rotate(a)
                if scale != 1.0:
                    a = a * scale
                if parts > 1:
                    part, off = divmod(c + j, width // parts)
                    out_ref[part, :, off:off + LANES] = a.astype(BF16)
                elif dil == 1:
                    out_ref[:, c + j:c + j + LANES] = a.astype(BF16)
                else:
                    stage_ref[j // LANES] = a
            if dil > 1:
                assert width == cw == stage_ref.shape[0] * LANES
                for r in range(dil):
                    for jb in range(width // LANES):
                        rows = stage_ref[jb, pl.ds(r, tm // dil, stride=dil), :]
                        out_ref[:, r * width + jb * LANES:r * width + (jb + 1) * LANES] = (
                            rows.astype(BF16))
        col += width


def _in_proj(x2, g_mix, cos_t, sin_t, w_in, seq):
    t = x2.shape[0]
    tm = WIDE_ROW_TILE
    tiles_per_seq = seq // tm
    row = lambda i: (i, 0)
    pos = lambda i: (i % tiles_per_seq, 0)
    out_shape, out_specs = [], []
    for (w, _, _, d, parts) in _IN_SEGMENTS:
        if parts > 1:
            out_shape.append(jax.ShapeDtypeStruct((parts, t, w // parts), BF16))
            out_specs.append(pl.BlockSpec((parts, tm, w // parts), lambda i: (0, i, 0)))
        else:
            out_shape.append(jax.ShapeDtypeStruct((t // d, d * w), BF16))
            out_specs.append(pl.BlockSpec((tm // d, d * w), row))
    return pl.pallas_call(
        _in_proj_kernel,
        grid=(t // tm,),
        in_specs=[
            pl.BlockSpec((tm, D_MODEL), row),
            _resident((1, D_MODEL)),
            pl.BlockSpec((tm, LANES), pos),
            pl.BlockSpec((tm, LANES), pos),
            _resident(w_in.shape),
        ],
        out_specs=out_specs,
        out_shape=out_shape,
        scratch_shapes=[pltpu.VMEM((GROUP_WIDTH // LANES, tm, LANES), F32)],
        compiler_params=_params(1),
        name="in_proj",
    )(x2, g_mix, cos_t, sin_t, w_in)


def _band_attn_kernel(q_ref, k_ref, v_ref, o_ref, lse_ref, *, seq_len, n_res):
    n_blocks = seq_len // Q_BLOCK
    qi = lax.broadcasted_iota(jnp.int32, (Q_BLOCK, K_SPAN), 0)
    kj = lax.broadcasted_iota(jnp.int32, (Q_BLOCK, K_SPAN), 1)
    rel = qi - kj
    lane = lax.broadcasted_iota(jnp.int32, (Q_BLOCK, LANES), 1)
    head0 = lane < HEAD_DIM

    def band_bias(q0, k0):
        return jnp.where(jnp.abs(rel + (q0 - k0)) <= BAND_HALF, 0.0, NEG_INF)

    def block(q0, k0, bias):
        for pair in range(n_res * 2):
            c0 = pair * LANES
            qp = q_ref[0, pl.ds(q0, Q_BLOCK), c0:c0 + LANES]
            kp = k_ref[0, pl.ds(k0, K_SPAN), c0:c0 + LANES]
            vp = v_ref[0, pl.ds(k0, K_SPAN), c0:c0 + LANES]
            v_ones = jnp.concatenate([vp, jnp.ones_like(vp)], axis=-1)
            outs = []
            for h in range(2):
                mask_h = head0 if h == 0 else jnp.logical_not(head0)
                qh = jnp.where(mask_h, qp, jnp.zeros_like(qp))
                s = lax.dot_general(qh, kp, (((1,), (1,)), ((), ())),
                                    preferred_element_type=F32) + bias
                m = jnp.max(s, axis=-1, keepdims=True)
                p = jnp.exp2(s - m).astype(BF16)
                ol = jnp.dot(p, v_ones, preferred_element_type=F32)
                o, l = ol[:, :LANES], ol[:, LANES:]
                outs.append((o / l, m + jnp.log2(l)))
            o_pair = jnp.where(head0, outs[0][0], outs[1][0])
            lse_pair = jnp.where(head0, outs[0][1], outs[1][1])
            o_ref[0, pl.ds(q0, Q_BLOCK), c0:c0 + LANES] = o_pair.astype(BF16)
            lse_ref[0, pl.ds(q0, Q_BLOCK), c0:c0 + LANES] = lse_pair

    last_q0 = seq_len - Q_BLOCK
    last_k0 = seq_len - K_SPAN
    block(0, 0, band_bias(0, 0))
    if n_blocks > 2:
        mid_bias = band_bias(BAND_HALF, 0)

        def interior(i, carry):
            q0 = pl.multiple_of(i * Q_BLOCK, Q_BLOCK)
            k0 = pl.multiple_of(i * Q_BLOCK - BAND_HALF, BAND_HALF)
            block(q0, k0, mid_bias)
            return carry

        unroll = max(1, ATTN_PAIRS_IN_FLIGHT // (2 * n_res))
        lax.fori_loop(1, n_blocks - 1, interior, 0, unroll=unroll)
    block(last_q0, last_k0, band_bias(last_q0, last_k0))


def _band_attn(q, k, v, batch, seq, dilation):
    seq_len = seq // dilation
    n_res = min(dilation, 4)
    width = dilation * GROUP_WIDTH
    view = lambda a: a.reshape(batch, seq_len, width)
    blk = (1, seq_len, n_res * GROUP_WIDTH)
    idx = lambda b, r: (b, 0, r)
    o, lse = pl.pallas_call(
        functools.partial(_band_attn_kernel, seq_len=seq_len, n_res=n_res),
        grid=(batch, dilation // n_res),
        in_specs=[pl.BlockSpec(blk, idx)] * 3,
        out_specs=[pl.BlockSpec(blk, idx)] * 2,
        out_shape=[jax.ShapeDtypeStruct((batch, seq_len, width), BF16),
                   jax.ShapeDtypeStruct((batch, seq_len, width), F32)],
        compiler_params=_params(2),
        name=f"band_attn_d{dilation}",
    )(view(q), view(k), view(v))
    return o.reshape(batch * seq_len, width), lse.reshape(batch * seq_len, width)


def _log_sigmoid(x):
    return jnp.minimum(x, 0.0) - jnp.log1p(jnp.exp(-jnp.abs(x)))


def _retention_kernel(decf_ref, decb_ref, q_ref, k_ref, v_ref, gate_ref, gret_ref, o_ref,
                      st_ref, s_ref, y_ref, zeta_ref, cdec_ref, xi_ref, din_ref, *, seq):
    c = RET_CHUNK
    n_chunks = seq // c
    pair_w = 2 * RET_QK_DIM
    dv = RET_V_DIM
    st_shape = (pair_w, 2 * RET_V_DIM)

    @pl.when(pl.program_id(1) == 0)
    def _():
        lg_f = _log_sigmoid(decf_ref[0])
        lg_b = _log_sigmoid(decb_ref[0])
        pos = lax.broadcasted_iota(jnp.int32, (c, pair_w), 0).astype(F32)
        zeta_ref[0] = jnp.exp(lg_f[0:1, :] * (c - 1.0 - pos))
        zeta_ref[1] = jnp.exp(lg_b[0:1, :] * pos)
        bcast = lambda lg, h, shape: jnp.broadcast_to(
            lg[0:1, h * RET_QK_DIM:h * RET_QK_DIM + 1], shape)
        row_is_h0 = lax.broadcasted_iota(jnp.int32, st_shape, 0) < RET_QK_DIM
        for d, lg in enumerate((lg_f, lg_b)):
            cdec_ref[d] = jnp.exp(
                jnp.where(row_is_h0, bcast(lg, 0, st_shape), bcast(lg, 1, st_shape)) * float(c))
        ci = lax.broadcasted_iota(jnp.int32, (c, c), 0)
        mi = lax.broadcasted_iota(jnp.int32, (c, c), 1)
        delta = (ci - mi).astype(F32)
        for h in range(2):
            xi_ref[h] = jnp.concatenate([jnp.exp(bcast(lg_f, h, (c, dv)) * (pos + 1.0)),
                                         jnp.exp(bcast(lg_b, h, (c, dv)) * (c - pos))], axis=-1)
            din_ref[h] = jnp.where(
                ci >= mi, jnp.exp(bcast(lg_f, h, (c, c)) * jnp.maximum(delta, 0.0)),
                jnp.exp(bcast(lg_b, h, (c, c)) * jnp.maximum(-delta, 0.0)))

    def row0(n):
        return n * c if isinstance(n, int) else pl.multiple_of(n * c, c)

    def chunk_kv(n, direction):
        r0 = row0(n)
        kz = (k_ref[0, pl.ds(r0, c), :].astype(F32) * zeta_ref[direction]).astype(BF16)
        return lax.dot_general(kz, v_ref[0, pl.ds(r0, c), :], (((0,), (0,)), ((), ())),
                               preferred_element_type=F32)

    lane = lax.broadcasted_iota(jnp.int32, (c, pair_w), 1)
    head0 = lane < RET_QK_DIM

    def head_query(qp, h):
        mask_h = head0 if h == 0 else jnp.logical_not(head0)
        return jnp.where(mask_h, qp, jnp.zeros_like(qp))

    def score_chunk(n):
        r0 = row0(n)
        qp = q_ref[0, pl.ds(r0, c), :]
        kp = k_ref[0, pl.ds(r0, c), :]
        for h in range(2):
            s = lax.dot_general(head_query(qp, h), kp, (((1,), (1,)), ((), ())),
                                preferred_element_type=F32)
            s_ref[pl.ds(r0, c), h * c:(h + 1) * c] = (s * din_ref[h]).astype(BF16)

    def state_step(j, carry):
        st_f, st_b = carry
        jb = n_chunks - 1 - j
        for h in range(2):
            hs = slice(h * dv, (h + 1) * dv)
            st_ref[j, :, (2 * h) * dv:(2 * h + 1) * dv] = st_f[:, hs].astype(BF16)
            st_ref[jb, :, (2 * h + 1) * dv:(2 * h + 2) * dv] = st_b[:, hs].astype(BF16)
        st_f = st_f * cdec_ref[0] + chunk_kv(j, 0)
        st_b = st_b * cdec_ref[1] + chunk_kv(jb, 1)
        score_chunk(j)
        return st_f, st_b

    zero_state = jnp.zeros((pair_w, 2 * RET_V_DIM), F32)
    lax.fori_loop(0, n_chunks, state_step, (zero_state, zero_state), unroll=STATE_UNROLL)

    def mix_chunk(n):
        r0 = row0(n)
        qp = q_ref[0, pl.ds(r0, c), :]
        for h in range(2):
            vs = slice(h * dv, (h + 1) * dv)
            y = jnp.dot(s_ref[pl.ds(r0, c), h * c:(h + 1) * c], v_ref[0, pl.ds(r0, c), vs],
                        preferred_element_type=F32)
            cross = jnp.dot(head_query(qp, h), st_ref[n, :, 2 * h * dv:(2 * h + 2) * dv],
                            preferred_element_type=F32) * xi_ref[h]
            y_ref[pl.ds(r0, c), vs] = y + cross[:, :dv] + cross[:, dv:]

    def norm_chunk(n):
        r0 = row0(n)
        for h in range(2):
            vs = slice(h * dv, (h + 1) * dv)
            y = y_ref[pl.ds(r0, c), vs]
            mu = jnp.mean(y, axis=-1, keepdims=True)
            yc = y - mu
            var = jnp.mean(yc * yc, axis=-1, keepdims=True)
            yn = yc * lax.rsqrt(var + EPS) * gret_ref[:, vs]
            gate = gate_ref[0, pl.ds(r0, c), vs].astype(F32)
            o_ref[0, pl.ds(r0, c), vs] = (yn * (gate * _sigmoid(gate))).astype(BF16)

    n_batches = n_chunks // MIX_UNROLL

    def mix_norm(i, carry):
        for u in range(MIX_UNROLL):
            mix_chunk(i * MIX_UNROLL + u)
            norm_chunk((i - 1) * MIX_UNROLL + u)
        return carry

    for u in range(MIX_UNROLL):
        mix_chunk(u)
    lax.fori_loop(1, n_batches, mix_norm, 0)
    for u in range(MIX_UNROLL):
        norm_chunk((n_batches - 1) * MIX_UNROLL + u)


def _retention(qr, kr, vr, gr, dec_f, dec_b, g_ret, batch, seq):
    pairs = RET_PAIRS
    n_chunks = seq // RET_CHUNK
    v3 = lambda a: a.reshape(pairs * batch, seq, a.shape[-1])
    slab = lambda p, b: (p * batch + b, 0, 0)
    qk_spec = pl.BlockSpec((1, seq, 2 * RET_QK_DIM), slab)
    v_spec = pl.BlockSpec((1, seq, 2 * RET_V_DIM), slab)
    dec_spec = pl.BlockSpec((1, SUBLANES, LANES), lambda p, b: (p, 0, 0))
    out = pl.pallas_call(
        functools.partial(_retention_kernel, seq=seq),
        grid=(pairs, batch),
        in_specs=[dec_spec, dec_spec, qk_spec, qk_spec, v_spec, v_spec,
                  pl.BlockSpec((1, 2 * RET_V_DIM), lambda p, b: (0, p))],
        out_specs=v_spec,
        out_shape=jax.ShapeDtypeStruct((pairs * batch, seq, 2 * RET_V_DIM), BF16),
        scratch_shapes=[pltpu.VMEM((n_chunks, 2 * RET_QK_DIM, 4 * RET_V_DIM), BF16),
                        pltpu.VMEM((seq, 2 * RET_CHUNK), BF16),
                        pltpu.VMEM((seq, 2 * RET_V_DIM), F32),
                        pltpu.VMEM((2, RET_CHUNK, 2 * RET_QK_DIM), F32),
                        pltpu.VMEM((2, 2 * RET_QK_DIM, 2 * RET_V_DIM), F32),
                        pltpu.VMEM((2, RET_CHUNK, 2 * RET_V_DIM), F32),
                        pltpu.VMEM((2, RET_CHUNK, RET_CHUNK), F32)],
        compiler_params=_params(2),
        name="retention",
    )(dec_f, dec_b, v3(qr), v3(kr), v3(vr), v3(gr), g_ret)
    return out.reshape(pairs, batch * seq, 2 * RET_V_DIM)


def _mem_kv_kernel(mem_ref, g_ref, w_ref, k_ref, v_ref):
    n = _rms(mem_ref[0], g_ref[...]).astype(BF16)
    kv = jnp.dot(n, w_ref[...], preferred_element_type=F32)
    k_ref[0] = kv[:, :MEM_WIDTH].astype(BF16)
    v_ref[0] = kv[:, MEM_WIDTH:].astype(BF16)


def _mem_kv(mem, g_mem, w_mem_kv):
    batch, mem_len, _ = mem.shape
    out = jax.ShapeDtypeStruct((batch, mem_len, MEM_WIDTH), BF16)
    spec = pl.BlockSpec((1, mem_len, MEM_WIDTH), lambda b: (b, 0, 0))
    return pl.pallas_call(
        _mem_kv_kernel,
        grid=(batch,),
        in_specs=[pl.BlockSpec((1, mem_len, D_MODEL), lambda b: (b, 0, 0)),
                  _resident((1, D_MODEL)), _resident(w_mem_kv.shape)],
        out_specs=[spec, spec],
        out_shape=[out, out],
        compiler_params=_params(1),
        name="mem_kv",
    )(mem, g_mem, w_mem_kv)


def _merge_kernel(x_ref, g_ref, o0_ref, o1_ref, o2_ref, l0_ref, l1_ref, l2_ref, yr_ref, qm_ref,
                  km_ref, vm_ref, wg_ref, bg_ref, wpa_ref, wpr_ref, wpm_ref, wo_ref, h_ref,
                  *stage_refs):
    x = x_ref[...]
    tm = x.shape[0]
    n = _rms(x, g_ref[...]).astype(BF16)
    gates = [_sigmoid(jnp.dot(n, wg_ref[:, i * D_MODEL:(i + 1) * D_MODEL],
                              preferred_element_type=F32)
                      + bg_ref[:, i * D_MODEL:(i + 1) * D_MODEL]) for i in range(3)]

    def token_rows(ref, dil, stage_ref):
        if dil == 1:
            return ref[...].astype(F32)
        n_lane_blocks = GROUP_WIDTH // LANES
        for r in range(dil):
            for jb in range(n_lane_blocks):
                c0 = r * GROUP_WIDTH + jb * LANES
                stage_ref[jb, pl.ds(r, tm // dil, stride=dil), :] = (
                    ref[:, c0:c0 + LANES].astype(F32))
        return jnp.concatenate([stage_ref[jb] for jb in range(n_lane_blocks)], axis=-1)

    y_r = jnp.concatenate([yr_ref[p] for p in range(RET_PAIRS)], axis=-1)
    pr = jnp.dot(y_r, wpr_ref[...], preferred_element_type=F32)

    ym = []
    for hd in range(MEM_HEADS):
        cs = slice(hd * MEM_HEAD_DIM, (hd + 1) * MEM_HEAD_DIM)
        s = lax.dot_general(qm_ref[:, cs], km_ref[0, :, cs], (((1,), (1,)), ((), ())),
                            preferred_element_type=F32) * (MEM_HEAD_DIM ** -0.5 * LOG2_E)
        p = jnp.exp2(s - jnp.max(s, axis=-1, keepdims=True)).astype(BF16)
        vm = vm_ref[0, :, cs]
        ol = jnp.dot(p, jnp.concatenate([vm, jnp.ones_like(vm)], axis=-1),
                     preferred_element_type=F32)
        ym.append((ol[:, :MEM_HEAD_DIM] / ol[:, MEM_HEAD_DIM:]).astype(BF16))
    pm = jnp.dot(jnp.concatenate(ym, axis=-1), wpm_ref[...], preferred_element_type=F32)

    stages = iter(stage_refs)
    outs, lses = [], []
    for o_ref, l_ref, dil in zip((o0_ref, o1_ref, o2_ref), (l0_ref, l1_ref, l2_ref), _DILATIONS):
        outs.append(token_rows(o_ref, dil, None if dil == 1 else next(stages)))
        lses.append(token_rows(l_ref, dil, None if dil == 1 else next(stages)))
    top = jnp.maximum(jnp.maximum(lses[0], lses[1]), lses[2])
    es = [jnp.exp2(l - top) for l in lses]
    inv = 1.0 / (es[0] + es[1] + es[2])
    pa = None
    for g in range(len(outs)):
        y_g = (outs[g] * (es[g] * inv)).astype(BF16)
        part = jnp.dot(y_g, wpa_ref[g * GROUP_WIDTH:(g + 1) * GROUP_WIDTH, :],
                       preferred_element_type=F32)
        pa = part if pa is None else pa + part

    merged = (gates[0] * pa + gates[1] * pr + gates[2] * pm).astype(BF16)
    h_ref[...] = x + jnp.dot(merged, wo_ref[...], preferred_element_type=F32)


def _merge(x2, g_mix, os_, lses, y_r, q_m, k_m, v_m, w_gate, b_gate, w_pa, w_pr, w_pm, w_out,
           seq):
    t = x2.shape[0]
    tm = ROW_TILE
    tiles_per_seq = seq // tm
    row = lambda i: (i, 0)
    mem_len = k_m.shape[1]
    mem_spec = pl.BlockSpec((1, mem_len, MEM_WIDTH), lambda i: (i // tiles_per_seq, 0, 0))
    group_specs = [pl.BlockSpec((tm // d, d * GROUP_WIDTH), row) for d in _DILATIONS]
    n_stages = 2 * sum(d > 1 for d in _DILATIONS)
    return pl.pallas_call(
        _merge_kernel,
        grid=(t // tm,),
        in_specs=[
            pl.BlockSpec((tm, D_MODEL), row), _resident((1, D_MODEL)),
            *group_specs, *group_specs,
            pl.BlockSpec((RET_PAIRS, tm, 2 * RET_V_DIM), lambda i: (0, i, 0)),
            pl.BlockSpec((tm, MEM_WIDTH), row),
            mem_spec, mem_spec,
            _resident(w_gate.shape), _resident(b_gate.shape), _resident(w_pa.shape),
            _resident(w_pr.shape), _resident(w_pm.shape), _resident(w_out.shape),
        ],
        out_specs=pl.BlockSpec((tm, D_MODEL), row),
        out_shape=jax.ShapeDtypeStruct((t, D_MODEL), F32),
        scratch_shapes=[pltpu.VMEM((GROUP_WIDTH // LANES, tm, LANES), F32)] * n_stages,
        compiler_params=_params(1),
        name="merge",
    )(x2, g_mix, *os_, *lses, y_r, q_m, k_m, v_m, w_gate, b_gate, w_pa, w_pr, w_pm, w_out)


FF_CHUNK = MXU_WIDTH
FF_GROUP = 4 * FF_CHUNK
HALO = SUBLANES


def _ffn_kernel(h_ref, prev_ref, next_ref, gffn_ref, wup_ref, cw_ref, cb_ref, wdown_ref,
                gfin_ref, out_ref, n_ref, u_ref, act_ref, *, tiles_per_seq):
    i = pl.program_id(0)
    tm = h_ref.shape[0]
    g = gffn_ref[...]
    has_prev = (i % tiles_per_seq) != 0
    has_next = (i % tiles_per_seq) != tiles_per_seq - 1
    n_prev = jnp.where(has_prev, _rms(prev_ref[...], g), 0.0)
    n_next = jnp.where(has_next, _rms(next_ref[...], g), 0.0)
    n_ref[...] = jnp.concatenate([n_prev, _rms(h_ref[...], g), n_next], axis=0).astype(BF16)

    def conv(slot, c0):
        sl = slice(c0, c0 + FF_CHUNK)
        taps = [u_ref[slot, pl.ds(HALO - 1 + k, tm), :] * cw_ref[k:k + 1, sl] for k in range(3)]
        return taps[0] + taps[1] + taps[2] + cb_ref[:, sl]

    def up(c0):
        sa = 2 * (c0 // FF_CHUNK)
        u_ref[sa] = jnp.dot(n_ref[...], wup_ref[:, c0:c0 + FF_CHUNK],
                            preferred_element_type=F32)
        u_ref[sa + 1] = jnp.dot(n_ref[...], wup_ref[:, D_FF + c0:D_FF + c0 + FF_CHUNK],
                                preferred_element_type=F32)

    def gate(c0):
        sa = 2 * (c0 // FF_CHUNK)
        a = conv(sa, c0)
        b = conv(sa + 1, D_FF + c0)
        act_ref[:, c0:c0 + FF_CHUNK] = (a * _sigmoid(a) * b).astype(BF16)

    groups = [(g0, min(g0 + FF_GROUP, D_FF)) for g0 in range(0, D_FF, FF_GROUP)]
    y = h_ref[...]
    for step in range(len(groups) + 2):
        if step < len(groups):
            for c0 in range(*groups[step], FF_CHUNK):
                up(c0)
        if 2 <= step:
            g0, g1 = groups[step - 2]
            y = y + jnp.dot(act_ref[:, g0:g1], wdown_ref[g0:g1, :],
                            preferred_element_type=F32)
        if 1 <= step <= len(groups):
            for c0 in range(*groups[step - 1], FF_CHUNK):
                gate(c0)
    out_ref[...] = _rms(y, gfin_ref[...])


def _ffn(h, g_ffn, w_up, conv_w, conv_b, w_down, g_final, seq):
    t = h.shape[0]
    tm = ROW_TILE
    tiles_per_seq = seq // tm
    halo_blocks = tm // HALO
    last_block = t // HALO - 1
    row = lambda i: (i, 0)
    return pl.pallas_call(
        functools.partial(_ffn_kernel, tiles_per_seq=tiles_per_seq),
        grid=(t // tm,),
        in_specs=[
            pl.BlockSpec((tm, D_MODEL), row),
            pl.BlockSpec((HALO, D_MODEL), lambda i: (jnp.maximum(i * halo_blocks - 1, 0), 0)),
            pl.BlockSpec((HALO, D_MODEL),
                         lambda i: (jnp.minimum((i + 1) * halo_blocks, last_block), 0)),
            _resident((1, D_MODEL)), _resident(w_up.shape), _resident(conv_w.shape),
            _resident(conv_b.shape), _resident(w_down.shape), _resident((1, D_MODEL)),
        ],
        out_specs=pl.BlockSpec((tm, D_MODEL), row),
        out_shape=jax.ShapeDtypeStruct((t, D_MODEL), F32),
        scratch_shapes=[pltpu.VMEM((tm + 2 * HALO, D_MODEL), BF16),
                        pltpu.VMEM((2 * (D_FF // FF_CHUNK), tm + 2 * HALO, FF_CHUNK), F32),
                        pltpu.VMEM((tm, D_FF), BF16)],
        compiler_params=_params(1),
        name="ffn",
    )(h, h, h, g_ffn, w_up, conv_w, conv_b, w_down, g_final)


def _rotary_tables(seq):
    inv = ROPE_THETA ** (-jnp.arange(0, HEAD_DIM, 2, dtype=F32) / HEAD_DIM)
    ang = jnp.arange(seq, dtype=F32)[:, None] * inv[None, :]
    cos, sin = jnp.cos(ang), jnp.sin(ang)
    reps = LANES // HEAD_DIM
    cos_t = jnp.tile(jnp.concatenate([cos, cos], axis=-1), (1, reps))
    sin_t = jnp.tile(jnp.concatenate([-sin, sin], axis=-1), (1, reps))
    return cos_t, sin_t


def _pair_lanes(v):
    pairs = v.reshape(RET_HEADS // 2, 2, 1)
    lanes = jnp.broadcast_to(pairs, (RET_HEADS // 2, 2, RET_QK_DIM)).reshape(RET_HEADS // 2, 1, LANES)
    return jnp.broadcast_to(lanes, (RET_HEADS // 2, SUBLANES, LANES)).astype(F32)


def _layer(h2, mem, batch, seq, g_mix, w_in, w_mem_kv, g_mem, decay_fwd, decay_bwd, g_ret,
           w_proj_attn, w_proj_ret, w_proj_mem, w_gate, b_gate, w_out):
    bf = lambda w: w.astype(BF16)
    row = lambda v: v.reshape(1, -1).astype(F32)
    cos_t, sin_t = _rotary_tables(seq)
    proj = _in_proj(h2, row(g_mix), cos_t, sin_t, bf(w_in), seq)
    qa, ka, va = proj[0:3], proj[3:6], proj[6:9]
    q_r, k_r, v_r, g_r, q_m = proj[9:]

    os_, lses = [], []
    for g, (_, dilation) in enumerate(ATTN_GROUPS):
        o, lse = _band_attn(qa[g], ka[g], va[g], batch, seq, dilation)
        os_.append(o)
        lses.append(lse)

    y_r = _retention(q_r, k_r, v_r, g_r, _pair_lanes(decay_fwd), _pair_lanes(decay_bwd),
                     row(g_ret), batch, seq)
    k_m, v_m = _mem_kv(mem, row(g_mem), bf(w_mem_kv))
    return _merge(h2, row(g_mix), os_, lses, y_r, q_m, k_m, v_m, bf(w_gate), row(b_gate),
                  bf(w_proj_attn), bf(w_proj_ret), bf(w_proj_mem), bf(w_out), seq)


def kernel(x, mem, g_mix, w_in, w_mem_kv, g_mem, ret_decay_fwd, ret_decay_bwd, g_ret,
           w_proj_attn, w_proj_ret, w_proj_mem, w_gate, b_gate, w_out,
           g_ffn, w_up, conv_w, conv_b, w_down, g_final):
    batch, seq, d = x.shape
    depth = w_in.shape[0]
    assert d == D_MODEL and depth == 1 and seq % ROW_TILE == 0 and seq % WIDE_ROW_TILE == 0
    h2 = x.reshape(batch * seq, d)
    l = 0
    h2 = _layer(h2, mem, batch, seq, g_mix[l], w_in[l], w_mem_kv[l], g_mem[l],
                ret_decay_fwd[l], ret_decay_bwd[l], g_ret[l], w_proj_attn[l], w_proj_ret[l],
                w_proj_mem[l], w_gate[l], b_gate[l], w_out[l])
    out = _ffn(h2, g_ffn[l].reshape(1, -1), w_up[l].astype(BF16), conv_w[l],
               conv_b[l].reshape(1, -1), w_down[l].astype(BF16), g_final.reshape(1, -1), seq)
    return out.reshape(batch, seq, d)
```

```python
import functools
import math

import jax
import jax.numpy as jnp
from jax import lax
from jax.experimental import pallas as pl
from jax.experimental.pallas import tpu as pltpu

D_MODEL = 1024
HEAD_DIM = 64
ATTN_GROUPS = ((128, 1), (512, 4), (2048, 16))
GROUP_WIDTH = 4 * HEAD_DIM
ATTN_WIDTH = 3 * GROUP_WIDTH
BAND_HALF = 64
RET_HEADS = 6
RET_QK_DIM = 64
RET_V_DIM = 128
RET_QK_WIDTH = RET_HEADS * RET_QK_DIM
RET_V_WIDTH = RET_HEADS * RET_V_DIM
MEM_HEADS = 4
MEM_HEAD_DIM = 128
MEM_WIDTH = MEM_HEADS * MEM_HEAD_DIM
D_FF = 2816
ROPE_THETA = 10000.0
EPS = 1e-6
NEG_INF = -1e30
LOG2_E = math.log2(math.e)

LANES = 128
SUBLANES = 8
MXU_WIDTH = 256
VMEM_LIMIT = 56 * 1024 * 1024

ROW_TILE = 512
WIDE_ROW_TILE = 1024
RET_CHUNK = 256
Q_BLOCK = 128
K_SPAN = Q_BLOCK + 2 * BAND_HALF
STATE_UNROLL = 4
MIX_UNROLL = 4
ATTN_PAIRS_IN_FLIGHT = 8

BF16 = jnp.bfloat16
F32 = jnp.float32


def _params(n_grid_axes):
    return pltpu.CompilerParams(
        dimension_semantics=("arbitrary",) * n_grid_axes, vmem_limit_bytes=VMEM_LIMIT)


def _resident(shape):
    nd = len(shape)
    return pl.BlockSpec(shape, lambda *_: (0,) * nd, pipeline_mode=pl.Buffered(1))


def _rms(x, g):
    return x * lax.rsqrt(jnp.mean(x * x, axis=-1, keepdims=True) + EPS) * g


def _qk_lane_is_head0(lane):
    return (lane & (HEAD_DIM // 2)) == 0


def _sigmoid(x):
    return 0.5 * jnp.tanh(0.5 * x) + 0.5


_DILATIONS = tuple(d for _, d in ATTN_GROUPS)
RET_PAIRS = RET_HEADS // 2
_IN_SEGMENTS = (
    *[(GROUP_WIDTH, True, HEAD_DIM ** -0.5 * LOG2_E, d, 1) for d in _DILATIONS],
    *[(GROUP_WIDTH, True, 1.0, d, 1) for d in _DILATIONS],
    *[(GROUP_WIDTH, False, 1.0, d, 1) for d in _DILATIONS],
    (RET_QK_WIDTH, True, 1.0, 1, RET_PAIRS),
    (RET_QK_WIDTH, True, RET_QK_DIM ** -0.5, 1, RET_PAIRS),
    (RET_V_WIDTH, False, 1.0, 1, RET_PAIRS),
    (RET_V_WIDTH, False, 1.0, 1, RET_PAIRS),
    (MEM_WIDTH, False, 1.0, 1, 1),
)


def _in_proj_kernel(x_ref, g_ref, cos_ref, sin_ref, w_ref, *refs):
    out_refs, stage_ref = refs[:-1], refs[-1]
    tm = x_ref.shape[0]
    n = _rms(x_ref[...], g_ref[...]).astype(BF16)
    cos = cos_ref[...]
    sin = sin_ref[...]

    def rotate(a):
        return a * cos + pltpu.roll(a, LANES // 2, 1) * sin

    col = 0
    for out_ref, (width, rotary, scale, dil, parts) in zip(out_refs, _IN_SEGMENTS):
        for c in range(0, width, MXU_WIDTH):
            cw = min(MXU_WIDTH, width - c)
            acc = jnp.dot(n, w_ref[:, col + c:col + c + cw], preferred_element_type=F32)
            for j in range(0, cw, LANES):
                a = acc[:, j:j + LANES]
                if rotary:
                    a = rotate(a)
                if scale != 1.0:
                    a = a * scale
                if parts > 1:
                    part, off = divmod(c + j, width // parts)
                    out_ref[part, :, off:off + LANES] = a.astype(BF16)
                elif dil == 1:
                    out_ref[:, c + j:c + j + LANES] = a.astype(BF16)
                else:
                    stage_ref[j // LANES] = a
            if dil > 1:
                assert width == cw == stage_ref.shape[0] * LANES
                for r in range(dil):
                    for jb in range(width // LANES):
                        rows = stage_ref[jb, pl.ds(r, tm // dil, stride=dil), :]
                        out_ref[:, r * width + jb * LANES:r * width + (jb + 1) * LANES] = (
                            rows.astype(BF16))
        col += width


def _in_proj(x2, g_mix, cos_t, sin_t, w_in, seq):
    t = x2.shape[0]
    tm = WIDE_ROW_TILE
    tiles_per_seq = seq // tm
    row = lambda i: (i, 0)
    pos = lambda i: (i % tiles_per_seq, 0)
    out_shape, out_specs = [], []
    for (w, _, _, d, parts) in _IN_SEGMENTS:
        if parts > 1:
            out_shape.append(jax.ShapeDtypeStruct((parts, t, w // parts), BF16))
            out_specs.append(pl.BlockSpec((parts, tm, w // parts), lambda i: (0, i, 0)))
        else:
            out_shape.append(jax.ShapeDtypeStruct((t // d, d * w), BF16))
            out_specs.append(pl.BlockSpec((tm // d, d * w), row))
    return pl.pallas_call(
        _in_proj_kernel,
        grid=(t // tm,),
        in_specs=[
            pl.BlockSpec((tm, D_MODEL), row),
            _resident((1, D_MODEL)),
            pl.BlockSpec((tm, LANES), pos),
            pl.BlockSpec((tm, LANES), pos),
            _resident(w_in.shape),
        ],
        out_specs=out_specs,
        out_shape=out_shape,
        scratch_shapes=[pltpu.VMEM((GROUP_WIDTH // LANES, tm, LANES), F32)],
        compiler_params=_params(1),
        name="in_proj",
    )(x2, g_mix, cos_t, sin_t, w_in)


def _band_attn_kernel(q_ref, k_ref, v_ref, o_ref, lse_ref, *, seq_len, n_res):
    n_blocks = seq_len // Q_BLOCK
    qi = lax.broadcasted_iota(jnp.int32, (Q_BLOCK, K_SPAN), 0)
    kj = lax.broadcasted_iota(jnp.int32, (Q_BLOCK, K_SPAN), 1)
    rel = qi - kj
    lane = lax.broadcasted_iota(jnp.int32, (Q_BLOCK, LANES), 1)
    head0 = lane < HEAD_DIM
    qk_head0 = _qk_lane_is_head0(lane)

    def band_bias(q0, k0):
        return jnp.where(jnp.abs(rel + (q0 - k0)) <= BAND_HALF, 0.0, NEG_INF)

    def block(q0, k0, bias):
        for pair in range(n_res * 2):
            c0 = pair * LANES
            qp = q_ref[0, pl.ds(q0, Q_BLOCK), c0:c0 + LANES]
            kp = k_ref[0, pl.ds(k0, K_SPAN), c0:c0 + LANES]
            vp = v_ref[0, pl.ds(k0, K_SPAN), c0:c0 + LANES]
            v_ones = jnp.concatenate([vp, jnp.ones_like(vp)], axis=-1)
            outs = []
            for h in range(2):
                mask_h = qk_head0 if h == 0 else jnp.logical_not(qk_head0)
                qh = jnp.where(mask_h, qp, jnp.zeros_like(qp))
                s = lax.dot_general(qh, kp, (((1,), (1,)), ((), ())),
                                    preferred_element_type=F32) + bias
                m = jnp.max(s, axis=-1, keepdims=True)
                p = jnp.exp2(s - m).astype(BF16)
                ol = jnp.dot(p, v_ones, preferred_element_type=F32)
                o, l = ol[:, :LANES], ol[:, LANES:]
                outs.append((o / l, m + jnp.log2(l)))
            o_pair = jnp.where(head0, outs[0][0], outs[1][0])
            lse_pair = jnp.where(head0, outs[0][1], outs[1][1])
            o_ref[0, pl.ds(q0, Q_BLOCK), c0:c0 + LANES] = o_pair.astype(BF16)
            lse_ref[0, pl.ds(q0, Q_BLOCK), c0:c0 + LANES] = lse_pair

    last_q0 = seq_len - Q_BLOCK
    last_k0 = seq_len - K_SPAN
    block(0, 0, band_bias(0, 0))
    if n_blocks > 2:
        mid_bias = band_bias(BAND_HALF, 0)

        def interior(i, carry):
            q0 = pl.multiple_of(i * Q_BLOCK, Q_BLOCK)
            k0 = pl.multiple_of(i * Q_BLOCK - BAND_HALF, BAND_HALF)
            block(q0, k0, mid_bias)
            return carry

        unroll = max(1, ATTN_PAIRS_IN_FLIGHT // (2 * n_res))
        lax.fori_loop(1, n_blocks - 1, interior, 0, unroll=unroll)
    block(last_q0, last_k0, band_bias(last_q0, last_k0))


def _band_attn(q, k, v, batch, seq, dilation):
    seq_len = seq // dilation
    n_res = min(dilation, 4)
    width = dilation * GROUP_WIDTH
    view = lambda a: a.reshape(batch, seq_len, width)
    blk = (1, seq_len, n_res * GROUP_WIDTH)
    idx = lambda b, r: (b, 0, r)
    o, lse = pl.pallas_call(
        functools.partial(_band_attn_kernel, seq_len=seq_len, n_res=n_res),
        grid=(batch, dilation // n_res),
        in_specs=[pl.BlockSpec(blk, idx)] * 3,
        out_specs=[pl.BlockSpec(blk, idx)] * 2,
        out_shape=[jax.ShapeDtypeStruct((batch, seq_len, width), BF16),
                   jax.ShapeDtypeStruct((batch, seq_len, width), F32)],
        compiler_params=_params(2),
        name=f"band_attn_d{dilation}",
    )(view(q), view(k), view(v))
    return o.reshape(batch * seq_len, width), lse.reshape(batch * seq_len, width)


def _log_sigmoid(x):
    return jnp.minimum(x, 0.0) - jnp.log1p(jnp.exp(-jnp.abs(x)))


def _retention_kernel(decf_ref, decb_ref, q_ref, k_ref, v_ref, gate_ref, gret_ref, o_ref,
                      st_ref, s_ref, y_ref, zeta_ref, cdec_ref, xi_ref, din_ref, *, seq):
    c = RET_CHUNK
    n_chunks = seq // c
    pair_w = 2 * RET_QK_DIM
    dv = RET_V_DIM
    st_shape = (pair_w, 2 * RET_V_DIM)

    @pl.when(pl.program_id(1) == 0)
    def _():
        lg_f = _log_sigmoid(decf_ref[0])
        lg_b = _log_sigmoid(decb_ref[0])
        pos = lax.broadcasted_iota(jnp.int32, (c, pair_w), 0).astype(F32)
        zeta_ref[0] = jnp.exp(lg_f[0:1, :] * (c - 1.0 - pos))
        zeta_ref[1] = jnp.exp(lg_b[0:1, :] * pos)
        bcast = lambda lg, h, shape: jnp.broadcast_to(
            lg[0:1, h * (RET_QK_DIM // 2):h * (RET_QK_DIM // 2) + 1], shape)
        row_is_h0 = _qk_lane_is_head0(lax.broadcasted_iota(jnp.int32, st_shape, 0))
        for d, lg in enumerate((lg_f, lg_b)):
            cdec_ref[d] = jnp.exp(
                jnp.where(row_is_h0, bcast(lg, 0, st_shape), bcast(lg, 1, st_shape)) * float(c))
        ci = lax.broadcasted_iota(jnp.int32, (c, c), 0)
        mi = lax.broadcasted_iota(jnp.int32, (c, c), 1)
        delta = (ci - mi).astype(F32)
        for h in range(2):
            xi_ref[h] = jnp.concatenate([jnp.exp(bcast(lg_f, h, (c, dv)) * (pos + 1.0)),
                                         jnp.exp(bcast(lg_b, h, (c, dv)) * (c - pos))], axis=-1)
            din_ref[h] = jnp.where(
                ci >= mi, jnp.exp(bcast(lg_f, h, (c, c)) * jnp.maximum(delta, 0.0)),
                jnp.exp(bcast(lg_b, h, (c, c)) * jnp.maximum(-delta, 0.0)))

    def row0(n):
        return n * c if isinstance(n, int) else pl.multiple_of(n * c, c)

    def chunk_kv(n, direction):
        r0 = row0(n)
        kz = (k_ref[0, pl.ds(r0, c), :].astype(F32) * zeta_ref[direction]).astype(BF16)
        return lax.dot_general(kz, v_ref[0, pl.ds(r0, c), :], (((0,), (0,)), ((), ())),
                               preferred_element_type=F32)

    lane = lax.broadcasted_iota(jnp.int32, (c, pair_w), 1)
    head0 = _qk_lane_is_head0(lane)

    def head_query(qp, h):
        mask_h = head0 if h == 0 else jnp.logical_not(head0)
        return jnp.where(mask_h, qp, jnp.zeros_like(qp))

    def score_chunk(n):
        r0 = row0(n)
        qp = q_ref[0, pl.ds(r0, c), :]
        kp = k_ref[0, pl.ds(r0, c), :]
        for h in range(2):
            s = lax.dot_general(head_query(qp, h), kp, (((1,), (1,)), ((), ())),
                                preferred_element_type=F32)
            s_ref[pl.ds(r0, c), h * c:(h + 1) * c] = (s * din_ref[h]).astype(BF16)

    def state_step(j, carry):
        st_f, st_b = carry
        jb = n_chunks - 1 - j
        for h in range(2):
            hs = slice(h * dv, (h + 1) * dv)
            st_ref[j, :, (2 * h) * dv:(2 * h + 1) * dv] = st_f[:, hs].astype(BF16)
            st_ref[jb, :, (2 * h + 1) * dv:(2 * h + 2) * dv] = st_b[:, hs].astype(BF16)
        st_f = st_f * cdec_ref[0] + chunk_kv(j, 0)
        st_b = st_b * cdec_ref[1] + chunk_kv(jb, 1)
        score_chunk(j)
        return st_f, st_b

    zero_state = jnp.zeros((pair_w, 2 * RET_V_DIM), F32)
    lax.fori_loop(0, n_chunks, state_step, (zero_state, zero_state), unroll=STATE_UNROLL)

    def mix_chunk(n):
        r0 = row0(n)
        qp = q_ref[0, pl.ds(r0, c), :]
        for h in range(2):
            vs = slice(h * dv, (h + 1) * dv)
            y = jnp.dot(s_ref[pl.ds(r0, c), h * c:(h + 1) * c], v_ref[0, pl.ds(r0, c), vs],
                        preferred_element_type=F32)
            cross = jnp.dot(head_query(qp, h), st_ref[n, :, 2 * h * dv:(2 * h + 2) * dv],
                            preferred_element_type=F32) * xi_ref[h]
            y_ref[pl.ds(r0, c), vs] = y + cross[:, :dv] + cross[:, dv:]

    def norm_chunk(n):
        r0 = row0(n)
        for h in range(2):
            vs = slice(h * dv, (h + 1) * dv)
            y = y_ref[pl.ds(r0, c), vs]
            mu = jnp.mean(y, axis=-1, keepdims=True)
            yc = y - mu
            var = jnp.mean(yc * yc, axis=-1, keepdims=True)
            yn = yc * lax.rsqrt(var + EPS) * gret_ref[:, vs]
            gate = gate_ref[0, pl.ds(r0, c), vs].astype(F32)
            o_ref[0, pl.ds(r0, c), vs] = (yn * (gate * _sigmoid(gate))).astype(BF16)

    n_batches = n_chunks // MIX_UNROLL

    def mix_norm(i, carry):
        for u in range(MIX_UNROLL):
            mix_chunk(i * MIX_UNROLL + u)
            norm_chunk((i - 1) * MIX_UNROLL + u)
        return carry

    for u in range(MIX_UNROLL):
        mix_chunk(u)
    lax.fori_loop(1, n_batches, mix_norm, 0)
    for u in range(MIX_UNROLL):
        norm_chunk((n_batches - 1) * MIX_UNROLL + u)


def _retention(qr, kr, vr, gr, dec_f, dec_b, g_ret, batch, seq):
    pairs = RET_PAIRS
    n_chunks = seq // RET_CHUNK
    v3 = lambda a: a.reshape(pairs * batch, seq, a.shape[-1])
    slab = lambda p, b: (p * batch + b, 0, 0)
    qk_spec = pl.BlockSpec((1, seq, 2 * RET_QK_DIM), slab)
    v_spec = pl.BlockSpec((1, seq, 2 * RET_V_DIM), slab)
    dec_spec = pl.BlockSpec((1, SUBLANES, LANES), lambda p, b: (p, 0, 0))
    out = pl.pallas_call(
        functools.partial(_retention_kernel, seq=seq),
        grid=(pairs, batch),
        in_specs=[dec_spec, dec_spec, qk_spec, qk_spec, v_spec, v_spec,
                  pl.BlockSpec((1, 2 * RET_V_DIM), lambda p, b: (0, p))],
        out_specs=v_spec,
        out_shape=jax.ShapeDtypeStruct((pairs * batch, seq, 2 * RET_V_DIM), BF16),
        scratch_shapes=[pltpu.VMEM((n_chunks, 2 * RET_QK_DIM, 4 * RET_V_DIM), BF16),
                        pltpu.VMEM((seq, 2 * RET_CHUNK), BF16),
                        pltpu.VMEM((seq, 2 * RET_V_DIM), F32),
                        pltpu.VMEM((2, RET_CHUNK, 2 * RET_QK_DIM), F32),
                        pltpu.VMEM((2, 2 * RET_QK_DIM, 2 * RET_V_DIM), F32),
                        pltpu.VMEM((2, RET_CHUNK, 2 * RET_V_DIM), F32),
                        pltpu.VMEM((2, RET_CHUNK, RET_CHUNK), F32)],
        compiler_params=_params(2),
        name="retention",
    )(dec_f, dec_b, v3(qr), v3(kr), v3(vr), v3(gr), g_ret)
    return out.reshape(pairs, batch * seq, 2 * RET_V_DIM)


def _mem_kv_kernel(mem_ref, g_ref, w_ref, k_ref, v_ref):
    n = _rms(mem_ref[0], g_ref[...]).astype(BF16)
    kv = jnp.dot(n, w_ref[...], preferred_element_type=F32)
    k_ref[0] = kv[:, :MEM_WIDTH].astype(BF16)
    v_ref[0] = kv[:, MEM_WIDTH:].astype(BF16)


def _mem_kv(mem, g_mem, w_mem_kv):
    batch, mem_len, _ = mem.shape
    out = jax.ShapeDtypeStruct((batch, mem_len, MEM_WIDTH), BF16)
    spec = pl.BlockSpec((1, mem_len, MEM_WIDTH), lambda b: (b, 0, 0))
    return pl.pallas_call(
        _mem_kv_kernel,
        grid=(batch,),
        in_specs=[pl.BlockSpec((1, mem_len, D_MODEL), lambda b: (b, 0, 0)),
                  _resident((1, D_MODEL)), _resident(w_mem_kv.shape)],
        out_specs=[spec, spec],
        out_shape=[out, out],
        compiler_params=_params(1),
        name="mem_kv",
    )(mem, g_mem, w_mem_kv)


def _merge_kernel(x_ref, g_ref, o0_ref, o1_ref, o2_ref, l0_ref, l1_ref, l2_ref, yr_ref, qm_ref,
                  km_ref, vm_ref, wg_ref, bg_ref, wpa_ref, wpr_ref, wpm_ref, wo_ref, h_ref,
                  *stage_refs):
    x = x_ref[...]
    tm = x.shape[0]
    n = _rms(x, g_ref[...]).astype(BF16)
    gates = [_sigmoid(jnp.dot(n, wg_ref[:, i * D_MODEL:(i + 1) * D_MODEL],
                              preferred_element_type=F32)
                      + bg_ref[:, i * D_MODEL:(i + 1) * D_MODEL]) for i in range(3)]

    def token_rows(ref, dil, stage_ref):
        if dil == 1:
            return ref[...].astype(F32)
        n_lane_blocks = GROUP_WIDTH // LANES
        for r in range(dil):
            for jb in range(n_lane_blocks):
                c0 = r * GROUP_WIDTH + jb * LANES
                stage_ref[jb, pl.ds(r, tm // dil, stride=dil), :] = (
                    ref[:, c0:c0 + LANES].astype(F32))
        return jnp.concatenate([stage_ref[jb] for jb in range(n_lane_blocks)], axis=-1)

    y_r = jnp.concatenate([yr_ref[p] for p in range(RET_PAIRS)], axis=-1)
    pr = jnp.dot(y_r, wpr_ref[...], preferred_element_type=F32)

    ym = []
    for hd in range(MEM_HEADS):
        cs = slice(hd * MEM_HEAD_DIM, (hd + 1) * MEM_HEAD_DIM)
        s = lax.dot_general(qm_ref[:, cs], km_ref[0, :, cs], (((1,), (1,)), ((), ())),
                            preferred_element_type=F32) * (MEM_HEAD_DIM ** -0.5 * LOG2_E)
        p = jnp.exp2(s - jnp.max(s, axis=-1, keepdims=True)).astype(BF16)
        vm = vm_ref[0, :, cs]
        ol = jnp.dot(p, jnp.concatenate([vm, jnp.ones_like(vm)], axis=-1),
                     preferred_element_type=F32)
        ym.append((ol[:, :MEM_HEAD_DIM] / ol[:, MEM_HEAD_DIM:]).astype(BF16))
    pm = jnp.dot(jnp.concatenate(ym, axis=-1), wpm_ref[...], preferred_element_type=F32)

    stages = iter(stage_refs)
    outs, lses = [], []
    for o_ref, l_ref, dil in zip((o0_ref, o1_ref, o2_ref), (l0_ref, l1_ref, l2_ref), _DILATIONS):
        outs.append(token_rows(o_ref, dil, None if dil == 1 else next(stages)))
        lses.append(token_rows(l_ref, dil, None if dil == 1 else next(stages)))
    top = jnp.maximum(jnp.maximum(lses[0], lses[1]), lses[2])
    es = [jnp.exp2(l - top) for l in lses]
    inv = 1.0 / (es[0] + es[1] + es[2])
    pa = None
    for g in range(len(outs)):
        y_g = (outs[g] * (es[g] * inv)).astype(BF16)
        part = jnp.dot(y_g, wpa_ref[g * GROUP_WIDTH:(g + 1) * GROUP_WIDTH, :],
                       preferred_element_type=F32)
        pa = part if pa is None else pa + part

    merged = (gates[0] * pa + gates[1] * pr + gates[2] * pm).astype(BF16)
    h_ref[...] = x + jnp.dot(merged, wo_ref[...], preferred_element_type=F32)


def _merge(x2, g_mix, os_, lses, y_r, q_m, k_m, v_m, w_gate, b_gate, w_pa, w_pr, w_pm, w_out,
           seq):
    t = x2.shape[0]
    tm = ROW_TILE
    tiles_per_seq = seq // tm
    row = lambda i: (i, 0)
    mem_len = k_m.shape[1]
    mem_spec = pl.BlockSpec((1, mem_len, MEM_WIDTH), lambda i: (i // tiles_per_seq, 0, 0))
    group_specs = [pl.BlockSpec((tm // d, d * GROUP_WIDTH), row) for d in _DILATIONS]
    n_stages = 2 * sum(d > 1 for d in _DILATIONS)
    return pl.pallas_call(
        _merge_kernel,
        grid=(t // tm,),
        in_specs=[
            pl.BlockSpec((tm, D_MODEL), row), _resident((1, D_MODEL)),
            *group_specs, *group_specs,
            pl.BlockSpec((RET_PAIRS, tm, 2 * RET_V_DIM), lambda i: (0, i, 0)),
            pl.BlockSpec((tm, MEM_WIDTH), row),
            mem_spec, mem_spec,
            _resident(w_gate.shape), _resident(b_gate.shape), _resident(w_pa.shape),
            _resident(w_pr.shape), _resident(w_pm.shape), _resident(w_out.shape),
        ],
        out_specs=pl.BlockSpec((tm, D_MODEL), row),
        out_shape=jax.ShapeDtypeStruct((t, D_MODEL), F32),
        scratch_shapes=[pltpu.VMEM((GROUP_WIDTH // LANES, tm, LANES), F32)] * n_stages,
        compiler_params=_params(1),
        name="merge",
    )(x2, g_mix, *os_, *lses, y_r, q_m, k_m, v_m, w_gate, b_gate, w_pa, w_pr, w_pm, w_out)


FF_CHUNK = MXU_WIDTH
FF_GROUP_CHUNKS = (4, 4, 3)
HALO = SUBLANES


N_FF_CHUNKS = D_FF // FF_CHUNK


def _ffn_kernel(h_ref, prev_ref, next_ref, gffn_ref, wup_ref, cw_ref, cb_ref, wdown_ref,
                gfin_ref, out_ref, n_ref, *scratch, tiles_per_seq):
    u_refs, act_refs, y_ref = scratch[:N_FF_CHUNKS], scratch[N_FF_CHUNKS:-1], scratch[-1]
    i = pl.program_id(0)
    tm = h_ref.shape[0]
    g = gffn_ref[...]
    has_prev = (i % tiles_per_seq) != 0
    has_next = (i % tiles_per_seq) != tiles_per_seq - 1
    n_prev = jnp.where(has_prev, _rms(prev_ref[...], g), 0.0)
    n_next = jnp.where(has_next, _rms(next_ref[...], g), 0.0)
    n_ref[...] = jnp.concatenate([n_prev, _rms(h_ref[...], g), n_next], axis=0).astype(BF16)

    half_rows = tm // 2
    lane_blocks = FF_CHUNK // LANES

    def conv(u_ref, ab, c0, parity):
        parts = []
        for jb in range(lane_blocks):
            sl = slice(c0 + jb * LANES, c0 + (jb + 1) * LANES)
            taps = [u_ref[ab, jb, pl.ds(HALO - 1 + parity + k, half_rows, stride=2), :]
                    * cw_ref[k:k + 1, sl] for k in range(3)]
            parts.append(taps[0] + taps[1] + taps[2] + cb_ref[:, sl])
        return jnp.concatenate(parts, axis=-1)

    def up(chunk):
        c0 = chunk * FF_CHUNK
        for ab, col in enumerate((c0, D_FF + c0)):
            u = jnp.dot(n_ref[...], wup_ref[:, col:col + FF_CHUNK], preferred_element_type=F32)
            for jb in range(lane_blocks):
                u_refs[chunk][ab, jb] = u[:, jb * LANES:(jb + 1) * LANES]

    def gate(chunk, act_ref, col):
        c0 = chunk * FF_CHUNK
        for parity in range(2):
            a = conv(u_refs[chunk], 0, c0, parity)
            b = conv(u_refs[chunk], 1, D_FF + c0, parity)
            act_ref[parity * half_rows:(parity + 1) * half_rows, col:col + FF_CHUNK] = (
                (a * _sigmoid(a) * b).astype(BF16))

    def down(gi):
        grp = groups[gi]
        return jnp.dot(act_refs[gi][...], wdown_ref[grp[0] * FF_CHUNK:(grp[-1] + 1) * FF_CHUNK, :],
                       preferred_element_type=F32)

    assert sum(FF_GROUP_CHUNKS) == N_FF_CHUNKS
    first = [sum(FF_GROUP_CHUNKS[:g]) for g in range(len(FF_GROUP_CHUNKS))]
    groups = [range(f, f + n) for f, n in zip(first, FF_GROUP_CHUNKS)]
    y = None
    for step in range(len(groups) + 2):
        if step < len(groups):
            for chunk in groups[step]:
                up(chunk)
        if 2 <= step:
            part = down(step - 2)
            y = part if y is None else y + part
        if 1 <= step <= len(groups):
            for j, chunk in enumerate(groups[step - 1]):
                gate(chunk, act_refs[step - 1], j * FF_CHUNK)
    for parity in range(2):
        for jb in range(D_MODEL // LANES):
            y_ref[jb, pl.ds(parity, half_rows, stride=2), :] = (
                y[parity * half_rows:(parity + 1) * half_rows, jb * LANES:(jb + 1) * LANES])
    y_tok = jnp.concatenate([y_ref[jb] for jb in range(D_MODEL // LANES)], axis=-1)
    out_ref[...] = _rms(h_ref[...] + y_tok, gfin_ref[...])


def _ffn(h, g_ffn, w_up, conv_w, conv_b, w_down, g_final, seq):
    t = h.shape[0]
    tm = ROW_TILE
    tiles_per_seq = seq // tm
    halo_blocks = tm // HALO
    last_block = t // HALO - 1
    row = lambda i: (i, 0)
    return pl.pallas_call(
        functools.partial(_ffn_kernel, tiles_per_seq=tiles_per_seq),
        grid=(t // tm,),
        in_specs=[
            pl.BlockSpec((tm, D_MODEL), row),
            pl.BlockSpec((HALO, D_MODEL), lambda i: (jnp.maximum(i * halo_blocks - 1, 0), 0)),
            pl.BlockSpec((HALO, D_MODEL),
                         lambda i: (jnp.minimum((i + 1) * halo_blocks, last_block), 0)),
            _resident((1, D_MODEL)), _resident(w_up.shape), _resident(conv_w.shape),
            _resident(conv_b.shape), _resident(w_down.shape), _resident((1, D_MODEL)),
        ],
        out_specs=pl.BlockSpec((tm, D_MODEL), row),
        out_shape=jax.ShapeDtypeStruct((t, D_MODEL), F32),
        scratch_shapes=[pltpu.VMEM((tm + 2 * HALO, D_MODEL), BF16),
                        *[pltpu.VMEM((2, FF_CHUNK // LANES, tm + 2 * HALO, LANES), F32)]
                        * N_FF_CHUNKS,
                        *[pltpu.VMEM((tm, n * FF_CHUNK), BF16) for n in FF_GROUP_CHUNKS],
                        pltpu.VMEM((D_MODEL // LANES, tm, LANES), F32)],
        compiler_params=_params(1),
        name="ffn",
    )(h, h, h, g_ffn, w_up, conv_w, conv_b, w_down, g_final)


def _rotary_tables(seq):
    inv = ROPE_THETA ** (-jnp.arange(0, HEAD_DIM, 2, dtype=F32) / HEAD_DIM)
    ang = jnp.arange(seq, dtype=F32)[:, None] * inv[None, :]
    cos, sin = jnp.cos(ang), jnp.sin(ang)
    cos_t = jnp.concatenate([cos, cos, cos, cos], axis=-1)
    sin_t = jnp.concatenate([-sin, -sin, sin, sin], axis=-1)
    return cos_t, sin_t


def _pair_lanes_columns(w):
    rows, cols = w.shape
    half = HEAD_DIM // 2
    w = w.reshape(rows, cols // (2 * HEAD_DIM), 2, 2, half)
    return w.transpose(0, 1, 3, 2, 4).reshape(rows, cols)


def _pair_lanes(v):
    half = RET_QK_DIM // 2
    pairs = v.reshape(RET_PAIRS, 1, 2, 1)
    lanes = jnp.broadcast_to(pairs, (RET_PAIRS, 2, 2, half)).reshape(RET_PAIRS, 1, LANES)
    return jnp.broadcast_to(lanes, (RET_PAIRS, SUBLANES, LANES)).astype(F32)


def _layer(h2, mem, batch, seq, g_mix, w_in, w_mem_kv, g_mem, decay_fwd, decay_bwd, g_ret,
           w_proj_attn, w_proj_ret, w_proj_mem, w_gate, b_gate, w_out):
    bf = lambda w: w.astype(BF16)
    row = lambda v: v.reshape(1, -1).astype(F32)
    cos_t, sin_t = _rotary_tables(seq)
    a_qk, r_qk = 2 * ATTN_WIDTH, 3 * ATTN_WIDTH + 2 * RET_QK_WIDTH
    w_in = jnp.concatenate([
        _pair_lanes_columns(w_in[:, :a_qk]), w_in[:, a_qk:3 * ATTN_WIDTH],
        _pair_lanes_columns(w_in[:, 3 * ATTN_WIDTH:r_qk]), w_in[:, r_qk:]], axis=1)
    proj = _in_proj(h2, row(g_mix), cos_t, sin_t, bf(w_in), seq)
    qa, ka, va = proj[0:3], proj[3:6], proj[6:9]
    q_r, k_r, v_r, g_r, q_m = proj[9:]

    os_, lses = [], []
    for g, (_, dilation) in enumerate(ATTN_GROUPS):
        o, lse = _band_attn(qa[g], ka[g], va[g], batch, seq, dilation)
        os_.append(o)
        lses.append(lse)

    y_r = _retention(q_r, k_r, v_r, g_r, _pair_lanes(decay_fwd), _pair_lanes(decay_bwd),
                     row(g_ret), batch, seq)
    k_m, v_m = _mem_kv(mem, row(g_mem), bf(w_mem_kv))
    return _merge(h2, row(g_mix), os_, lses, y_r, q_m, k_m, v_m, bf(w_gate), row(b_gate),
                  bf(w_proj_attn), bf(w_proj_ret), bf(w_proj_mem), bf(w_out), seq)


def kernel(x, mem, g_mix, w_in, w_mem_kv, g_mem, ret_decay_fwd, ret_decay_bwd, g_ret,
           w_proj_attn, w_proj_ret, w_proj_mem, w_gate, b_gate, w_out,
           g_ffn, w_up, conv_w, conv_b, w_down, g_final):
    batch, seq, d = x.shape
    depth = w_in.shape[0]
    assert d == D_MODEL and depth == 1 and seq % ROW_TILE == 0 and seq % WIDE_ROW_TILE == 0
    h2 = x.reshape(batch * seq, d)
    l = 0
    h2 = _layer(h2, mem, batch, seq, g_mix[l], w_in[l], w_mem_kv[l], g_mem[l],
                ret_decay_fwd[l], ret_decay_bwd[l], g_ret[l], w_proj_attn[l], w_proj_ret[l],
                w_proj_mem[l], w_gate[l], b_gate[l], w_out[l])
    out = _ffn(h2, g_ffn[l].reshape(1, -1), w_up[l].astype(BF16), conv_w[l],
               conv_b[l].reshape(1, -1), w_down[l].astype(BF16), g_final.reshape(1, -1), seq)
    return out.reshape(batch, seq, d)
```

```python
import functools
import math

import jax
import jax.numpy as jnp
from jax import lax
from jax.experimental import pallas as pl
from jax.experimental.pallas import tpu as pltpu

D_MODEL = 1024
HEAD_DIM = 64
ATTN_GROUPS = ((128, 1), (512, 4), (2048, 16))
GROUP_WIDTH = 4 * HEAD_DIM
ATTN_WIDTH = 3 * GROUP_WIDTH
BAND_HALF = 64
RET_HEADS = 6
RET_QK_DIM = 64
RET_V_DIM = 128
RET_QK_WIDTH = RET_HEADS * RET_QK_DIM
RET_V_WIDTH = RET_HEADS * RET_V_DIM
MEM_HEADS = 4
MEM_HEAD_DIM = 128
MEM_WIDTH = MEM_HEADS * MEM_HEAD_DIM
D_FF = 2816
ROPE_THETA = 10000.0
EPS = 1e-6
NEG_INF = -1e30
LOG2_E = math.log2(math.e)

LANES = 128
SUBLANES = 8
MXU_WIDTH = 256
VMEM_LIMIT = 56 * 1024 * 1024

ROW_TILE = 512
WIDE_ROW_TILE = 1024
RET_CHUNK = 256
Q_BLOCK = 128
K_SPAN = Q_BLOCK + 2 * BAND_HALF
STATE_UNROLL = 4
MIX_UNROLL = 4
ATTN_PAIRS_IN_FLIGHT = 8

BF16 = jnp.bfloat16
F32 = jnp.float32


def _params(n_grid_axes):
    return pltpu.CompilerParams(
        dimension_semantics=("arbitrary",) * n_grid_axes, vmem_limit_bytes=VMEM_LIMIT)


def _resident(shape):
    nd = len(shape)
    return pl.BlockSpec(shape, lambda *_: (0,) * nd, pipeline_mode=pl.Buffered(1))


def _rms(x, g):
    return x * lax.rsqrt(jnp.mean(x * x, axis=-1, keepdims=True) + EPS) * g


def _qk_lane_is_head0(lane):
    return lane < HEAD_DIM


def _sigmoid(x):
    return 0.5 * jnp.tanh(0.5 * x) + 0.5


_DILATIONS = tuple(d for _, d in ATTN_GROUPS)
RET_PAIRS = RET_HEADS // 2
_IN_SEGMENTS = (
    *[(GROUP_WIDTH, True, HEAD_DIM ** -0.5 * LOG2_E, d, 1) for d in _DILATIONS],
    *[(GROUP_WIDTH, True, 1.0, d, 1) for d in _DILATIONS],
    *[(GROUP_WIDTH, False, 1.0, d, 1) for d in _DILATIONS],
    (RET_QK_WIDTH, True, 1.0, 1, RET_PAIRS),
    (RET_QK_WIDTH, True, RET_QK_DIM ** -0.5, 1, RET_PAIRS),
    (RET_V_WIDTH, False, 1.0, 1, RET_PAIRS),
    (RET_V_WIDTH, False, 1.0, 1, RET_PAIRS),
    (MEM_WIDTH, False, 1.0, 1, 1),
)


def _in_proj_kernel(x_ref, g_ref, cos_ref, sin_ref, w_ref, *refs):
    out_refs, stage_ref = refs[:-1], refs[-1]
    tm = x_ref.shape[0]
    n = _rms(x_ref[...], g_ref[...]).astype(BF16)
    cos = cos_ref[...]
    sin = sin_ref[...]
    lane = lax.broadcasted_iota(jnp.int32, cos.shape, 1)
    low_half = (lane & (HEAD_DIM // 2)) == 0

    def rotate(a):
        partner = jnp.where(low_half, pltpu.roll(a, LANES - HEAD_DIM // 2, 1),
                            pltpu.roll(a, HEAD_DIM // 2, 1))
        return a * cos + partner * sin

    col = 0
    for out_ref, (width, rotary, scale, dil, parts) in zip(out_refs, _IN_SEGMENTS):
        for c in range(0, width, MXU_WIDTH):
            cw = min(MXU_WIDTH, width - c)
            acc = jnp.dot(n, w_ref[:, col + c:col + c + cw], preferred_element_type=F32)
            for j in range(0, cw, LANES):
                a = acc[:, j:j + LANES]
                if rotary:
                    a = rotate(a)
                if scale != 1.0:
                    a = a * scale
                if parts > 1:
                    part, off = divmod(c + j, width // parts)
                    out_ref[part, :, off:off + LANES] = a.astype(BF16)
                elif dil == 1:
                    out_ref[:, c + j:c + j + LANES] = a.astype(BF16)
                else:
                    stage_ref[j // LANES] = a
            if dil > 1:
                assert width == cw == stage_ref.shape[0] * LANES
                for r in range(dil):
                    for jb in range(width // LANES):
                        rows = stage_ref[jb, pl.ds(r, tm // dil, stride=dil), :]
                        out_ref[:, r * width + jb * LANES:r * width + (jb + 1) * LANES] = (
                            rows.astype(BF16))
        col += width


def _in_proj(x2, g_mix, cos_t, sin_t, w_in, seq):
    t = x2.shape[0]
    tm = WIDE_ROW_TILE
    tiles_per_seq = seq // tm
    row = lambda i: (i, 0)
    pos = lambda i: (i % tiles_per_seq, 0)
    out_shape, out_specs = [], []
    for (w, _, _, d, parts) in _IN_SEGMENTS:
        if parts > 1:
            out_shape.append(jax.ShapeDtypeStruct((parts, t, w // parts), BF16))
            out_specs.append(pl.BlockSpec((parts, tm, w // parts), lambda i: (0, i, 0)))
        else:
            out_shape.append(jax.ShapeDtypeStruct((t // d, d * w), BF16))
            out_specs.append(pl.BlockSpec((tm // d, d * w), row))
    return pl.pallas_call(
        _in_proj_kernel,
        grid=(t // tm,),
        in_specs=[
            pl.BlockSpec((tm, D_MODEL), row),
            _resident((1, D_MODEL)),
            pl.BlockSpec((tm, LANES), pos),
            pl.BlockSpec((tm, LANES), pos),
            _resident(w_in.shape),
        ],
        out_specs=out_specs,
        out_shape=out_shape,
        scratch_shapes=[pltpu.VMEM((GROUP_WIDTH // LANES, tm, LANES), F32)],
        compiler_params=_params(1),
        name="in_proj",
    )(x2, g_mix, cos_t, sin_t, w_in)


def _band_attn_kernel(q_ref, k_ref, v_ref, o_ref, lse_ref, *, seq_len, n_res):
    n_blocks = seq_len // Q_BLOCK
    qi = lax.broadcasted_iota(jnp.int32, (Q_BLOCK, K_SPAN), 0)
    kj = lax.broadcasted_iota(jnp.int32, (Q_BLOCK, K_SPAN), 1)
    rel = qi - kj
    lane = lax.broadcasted_iota(jnp.int32, (Q_BLOCK, LANES), 1)
    head0 = lane < HEAD_DIM
    qk_head0 = _qk_lane_is_head0(lane)

    def band_bias(q0, k0):
        return jnp.where(jnp.abs(rel + (q0 - k0)) <= BAND_HALF, 0.0, NEG_INF)

    def block(q0, k0, bias):
        for pair in range(n_res * 2):
            c0 = pair * LANES
            qp = q_ref[0, pl.ds(q0, Q_BLOCK), c0:c0 + LANES]
            kp = k_ref[0, pl.ds(k0, K_SPAN), c0:c0 + LANES]
            vp = v_ref[0, pl.ds(k0, K_SPAN), c0:c0 + LANES]
            v_ones = jnp.concatenate([vp, jnp.ones_like(vp)], axis=-1)
            outs = []
            for h in range(2):
                mask_h = qk_head0 if h == 0 else jnp.logical_not(qk_head0)
                qh = jnp.where(mask_h, qp, jnp.zeros_like(qp))
                s = lax.dot_general(qh, kp, (((1,), (1,)), ((), ())),
                                    preferred_element_type=F32) + bias
                m = jnp.max(s, axis=-1, keepdims=True)
                p = jnp.exp2(s - m).astype(BF16)
                ol = jnp.dot(p, v_ones, preferred_element_type=F32)
                o, l = ol[:, :LANES], ol[:, LANES:]
                outs.append((o / l, m + jnp.log2(l)))
            o_pair = jnp.where(head0, outs[0][0], outs[1][0])
            lse_pair = jnp.where(head0, outs[0][1], outs[1][1])
            o_ref[0, pl.ds(q0, Q_BLOCK), c0:c0 + LANES] = o_pair.astype(BF16)
            lse_ref[0, pl.ds(q0, Q_BLOCK), c0:c0 + LANES] = lse_pair

    last_q0 = seq_len - Q_BLOCK
    last_k0 = seq_len - K_SPAN
    block(0, 0, band_bias(0, 0))
    if n_blocks > 2:
        mid_bias = band_bias(BAND_HALF, 0)

        def interior(i, carry):
            q0 = pl.multiple_of(i * Q_BLOCK, Q_BLOCK)
            k0 = pl.multiple_of(i * Q_BLOCK - BAND_HALF, BAND_HALF)
            block(q0, k0, mid_bias)
            return carry

        unroll = max(1, ATTN_PAIRS_IN_FLIGHT // (2 * n_res))
        lax.fori_loop(1, n_blocks - 1, interior, 0, unroll=unroll)
    block(last_q0, last_k0, band_bias(last_q0, last_k0))


def _band_attn(q, k, v, batch, seq, dilation):
    seq_len = seq // dilation
    n_res = min(dilation, 4)
    width = dilation * GROUP_WIDTH
    view = lambda a: a.reshape(batch, seq_len, width)
    blk = (1, seq_len, n_res * GROUP_WIDTH)
    idx = lambda b, r: (b, 0, r)
    o, lse = pl.pallas_call(
        functools.partial(_band_attn_kernel, seq_len=seq_len, n_res=n_res),
        grid=(batch, dilation // n_res),
        in_specs=[pl.BlockSpec(blk, idx)] * 3,
        out_specs=[pl.BlockSpec(blk, idx)] * 2,
        out_shape=[jax.ShapeDtypeStruct((batch, seq_len, width), BF16),
                   jax.ShapeDtypeStruct((batch, seq_len, width), F32)],
        compiler_params=_params(2),
        name=f"band_attn_d{dilation}",
    )(view(q), view(k), view(v))
    return o.reshape(batch * seq_len, width), lse.reshape(batch * seq_len, width)


def _log_sigmoid(x):
    return jnp.minimum(x, 0.0) - jnp.log1p(jnp.exp(-jnp.abs(x)))


def _retention_kernel(decf_ref, decb_ref, q_ref, k_ref, v_ref, gate_ref, gret_ref, o_ref,
                      st_ref, s_ref, y_ref, zeta_ref, cdec_ref, xi_ref, din_ref, *, seq):
    c = RET_CHUNK
    n_chunks = seq // c
    pair_w = 2 * RET_QK_DIM
    dv = RET_V_DIM
    st_shape = (pair_w, 2 * RET_V_DIM)

    @pl.when(pl.program_id(1) == 0)
    def _():
        lg_f = _log_sigmoid(decf_ref[0])
        lg_b = _log_sigmoid(decb_ref[0])
        pos = lax.broadcasted_iota(jnp.int32, (c, pair_w), 0).astype(F32)
        zeta_ref[0] = jnp.exp(lg_f[0:1, :] * (c - 1.0 - pos))
        zeta_ref[1] = jnp.exp(lg_b[0:1, :] * pos)
        bcast = lambda lg, h, shape: jnp.broadcast_to(
            lg[0:1, h * RET_QK_DIM:h * RET_QK_DIM + 1], shape)
        row_is_h0 = _qk_lane_is_head0(lax.broadcasted_iota(jnp.int32, st_shape, 0))
        for d, lg in enumerate((lg_f, lg_b)):
            cdec_ref[d] = jnp.exp(
                jnp.where(row_is_h0, bcast(lg, 0, st_shape), bcast(lg, 1, st_shape)) * float(c))
        ci = lax.broadcasted_iota(jnp.int32, (c, c), 0)
        mi = lax.broadcasted_iota(jnp.int32, (c, c), 1)
        delta = (ci - mi).astype(F32)
        for h in range(2):
            xi_ref[h] = jnp.concatenate([jnp.exp(bcast(lg_f, h, (c, dv)) * (pos + 1.0)),
                                         jnp.exp(bcast(lg_b, h, (c, dv)) * (c - pos))], axis=-1)
            din_ref[h] = jnp.where(
                ci >= mi, jnp.exp(bcast(lg_f, h, (c, c)) * jnp.maximum(delta, 0.0)),
                jnp.exp(bcast(lg_b, h, (c, c)) * jnp.maximum(-delta, 0.0)))

    def row0(n):
        return n * c if isinstance(n, int) else pl.multiple_of(n * c, c)

    def chunk_kv(n, direction):
        r0 = row0(n)
        kz = (k_ref[0, pl.ds(r0, c), :].astype(F32) * zeta_ref[direction]).astype(BF16)
        return lax.dot_general(kz, v_ref[0, pl.ds(r0, c), :], (((0,), (0,)), ((), ())),
                               preferred_element_type=F32)

    lane = lax.broadcasted_iota(jnp.int32, (c, pair_w), 1)
    head0 = _qk_lane_is_head0(lane)

    def head_query(qp, h):
        mask_h = head0 if h == 0 else jnp.logical_not(head0)
        return jnp.where(mask_h, qp, jnp.zeros_like(qp))

    def score_chunk(n):
        r0 = row0(n)
        qp = q_ref[0, pl.ds(r0, c), :]
        kp = k_ref[0, pl.ds(r0, c), :]
        for h in range(2):
            s = lax.dot_general(head_query(qp, h), kp, (((1,), (1,)), ((), ())),
                                preferred_element_type=F32)
            s_ref[pl.ds(r0, c), h * c:(h + 1) * c] = (s * din_ref[h]).astype(BF16)

    def state_step(j, carry):
        st_f, st_b = carry
        jb = n_chunks - 1 - j
        for h in range(2):
            hs = slice(h * dv, (h + 1) * dv)
            st_ref[j, :, (2 * h) * dv:(2 * h + 1) * dv] = st_f[:, hs].astype(BF16)
            st_ref[jb, :, (2 * h + 1) * dv:(2 * h + 2) * dv] = st_b[:, hs].astype(BF16)
        st_f = st_f * cdec_ref[0] + chunk_kv(j, 0)
        st_b = st_b * cdec_ref[1] + chunk_kv(jb, 1)
        score_chunk(j)
        return st_f, st_b

    zero_state = jnp.zeros((pair_w, 2 * RET_V_DIM), F32)
    lax.fori_loop(0, n_chunks, state_step, (zero_state, zero_state), unroll=STATE_UNROLL)

    def mix_chunk(n):
        r0 = row0(n)
        qp = q_ref[0, pl.ds(r0, c), :]
        for h in range(2):
            vs = slice(h * dv, (h + 1) * dv)
            y = jnp.dot(s_ref[pl.ds(r0, c), h * c:(h + 1) * c], v_ref[0, pl.ds(r0, c), vs],
                        preferred_element_type=F32)
            cross = jnp.dot(head_query(qp, h), st_ref[n, :, 2 * h * dv:(2 * h + 2) * dv],
                            preferred_element_type=F32) * xi_ref[h]
            y_ref[pl.ds(r0, c), vs] = y + cross[:, :dv] + cross[:, dv:]

    def norm_chunk(n):
        r0 = row0(n)
        for h in range(2):
            vs = slice(h * dv, (h + 1) * dv)
            y = y_ref[pl.ds(r0, c), vs]
            mu = jnp.mean(y, axis=-1, keepdims=True)
            yc = y - mu
            var = jnp.mean(yc * yc, axis=-1, keepdims=True)
            yn = yc * lax.rsqrt(var + EPS) * gret_ref[:, vs]
            gate = gate_ref[0, pl.ds(r0, c), vs].astype(F32)
            o_ref[0, pl.ds(r0, c), vs] = (yn * (gate * _sigmoid(gate))).astype(BF16)

    n_batches = n_chunks // MIX_UNROLL

    def mix_norm(i, carry):
        for u in range(MIX_UNROLL):
            mix_chunk(i * MIX_UNROLL + u)
            norm_chunk((i - 1) * MIX_UNROLL + u)
        return carry

    for u in range(MIX_UNROLL):
        mix_chunk(u)
    lax.fori_loop(1, n_batches, mix_norm, 0)
    for u in range(MIX_UNROLL):
        norm_chunk((n_batches - 1) * MIX_UNROLL + u)


def _retention(qr, kr, vr, gr, dec_f, dec_b, g_ret, batch, seq):
    pairs = RET_PAIRS
    n_chunks = seq // RET_CHUNK
    v3 = lambda a: a.reshape(pairs * batch, seq, a.shape[-1])
    slab = lambda p, b: (p * batch + b, 0, 0)
    qk_spec = pl.BlockSpec((1, seq, 2 * RET_QK_DIM), slab)
    v_spec = pl.BlockSpec((1, seq, 2 * RET_V_DIM), slab)
    dec_spec = pl.BlockSpec((1, SUBLANES, LANES), lambda p, b: (p, 0, 0))
    out = pl.pallas_call(
        functools.partial(_retention_kernel, seq=seq),
        grid=(pairs, batch),
        in_specs=[dec_spec, dec_spec, qk_spec, qk_spec, v_spec, v_spec,
                  pl.BlockSpec((1, 2 * RET_V_DIM), lambda p, b: (0, p))],
        out_specs=v_spec,
        out_shape=jax.ShapeDtypeStruct((pairs * batch, seq, 2 * RET_V_DIM), BF16),
        scratch_shapes=[pltpu.VMEM((n_chunks, 2 * RET_QK_DIM, 4 * RET_V_DIM), BF16),
                        pltpu.VMEM((seq, 2 * RET_CHUNK), BF16),
                        pltpu.VMEM((seq, 2 * RET_V_DIM), F32),
                        pltpu.VMEM((2, RET_CHUNK, 2 * RET_QK_DIM), F32),
                        pltpu.VMEM((2, 2 * RET_QK_DIM, 2 * RET_V_DIM), F32),
                        pltpu.VMEM((2, RET_CHUNK, 2 * RET_V_DIM), F32),
                        pltpu.VMEM((2, RET_CHUNK, RET_CHUNK), F32)],
        compiler_params=_params(2),
        name="retention",
    )(dec_f, dec_b, v3(qr), v3(kr), v3(vr), v3(gr), g_ret)
    return out.reshape(pairs, batch * seq, 2 * RET_V_DIM)


def _mem_kv_kernel(mem_ref, g_ref, w_ref, k_ref, v_ref):
    n = _rms(mem_ref[0], g_ref[...]).astype(BF16)
    kv = jnp.dot(n, w_ref[...], preferred_element_type=F32)
    k_ref[0] = kv[:, :MEM_WIDTH].astype(BF16)
    v_ref[0] = kv[:, MEM_WIDTH:].astype(BF16)


def _mem_kv(mem, g_mem, w_mem_kv):
    batch, mem_len, _ = mem.shape
    out = jax.ShapeDtypeStruct((batch, mem_len, MEM_WIDTH), BF16)
    spec = pl.BlockSpec((1, mem_len, MEM_WIDTH), lambda b: (b, 0, 0))
    return pl.pallas_call(
        _mem_kv_kernel,
        grid=(batch,),
        in_specs=[pl.BlockSpec((1, mem_len, D_MODEL), lambda b: (b, 0, 0)),
                  _resident((1, D_MODEL)), _resident(w_mem_kv.shape)],
        out_specs=[spec, spec],
        out_shape=[out, out],
        compiler_params=_params(1),
        name="mem_kv",
    )(mem, g_mem, w_mem_kv)


def _merge_kernel(x_ref, g_ref, o0_ref, o1_ref, o2_ref, l0_ref, l1_ref, l2_ref, yr_ref, qm_ref,
                  km_ref, vm_ref, wg_ref, bg_ref, wpa_ref, wpr_ref, wpm_ref, wo_ref, h_ref,
                  *stage_refs):
    x = x_ref[...]
    tm = x.shape[0]
    n = _rms(x, g_ref[...]).astype(BF16)
    gates = [_sigmoid(jnp.dot(n, wg_ref[:, i * D_MODEL:(i + 1) * D_MODEL],
                              preferred_element_type=F32)
                      + bg_ref[:, i * D_MODEL:(i + 1) * D_MODEL]) for i in range(3)]

    def token_rows(ref, dil, stage_ref):
        if dil == 1:
            return ref[...].astype(F32)
        n_lane_blocks = GROUP_WIDTH // LANES
        for r in range(dil):
            for jb in range(n_lane_blocks):
                c0 = r * GROUP_WIDTH + jb * LANES
                stage_ref[jb, pl.ds(r, tm // dil, stride=dil), :] = (
                    ref[:, c0:c0 + LANES].astype(F32))
        return jnp.concatenate([stage_ref[jb] for jb in range(n_lane_blocks)], axis=-1)

    y_r = jnp.concatenate([yr_ref[p] for p in range(RET_PAIRS)], axis=-1)
    pr = jnp.dot(y_r, wpr_ref[...], preferred_element_type=F32)

    ym = []
    for hd in range(MEM_HEADS):
        cs = slice(hd * MEM_HEAD_DIM, (hd + 1) * MEM_HEAD_DIM)
        s = lax.dot_general(qm_ref[:, cs], km_ref[0, :, cs], (((1,), (1,)), ((), ())),
                            preferred_element_type=F32) * (MEM_HEAD_DIM ** -0.5 * LOG2_E)
        p = jnp.exp2(s - jnp.max(s, axis=-1, keepdims=True)).astype(BF16)
        vm = vm_ref[0, :, cs]
        ol = jnp.dot(p, jnp.concatenate([vm, jnp.ones_like(vm)], axis=-1),
                     preferred_element_type=F32)
        ym.append((ol[:, :MEM_HEAD_DIM] / ol[:, MEM_HEAD_DIM:]).astype(BF16))
    pm = jnp.dot(jnp.concatenate(ym, axis=-1), wpm_ref[...], preferred_element_type=F32)

    stages = iter(stage_refs)
    outs, lses = [], []
    for o_ref, l_ref, dil in zip((o0_ref, o1_ref, o2_ref), (l0_ref, l1_ref, l2_ref), _DILATIONS):
        outs.append(token_rows(o_ref, dil, None if dil == 1 else next(stages)))
        lses.append(token_rows(l_ref, dil, None if dil == 1 else next(stages)))
    top = jnp.maximum(jnp.maximum(lses[0], lses[1]), lses[2])
    es = [jnp.exp2(l - top) for l in lses]
    inv = 1.0 / (es[0] + es[1] + es[2])
    pa = None
    for g in range(len(outs)):
        y_g = (outs[g] * (es[g] * inv)).astype(BF16)
        part = jnp.dot(y_g, wpa_ref[g * GROUP_WIDTH:(g + 1) * GROUP_WIDTH, :],
                       preferred_element_type=F32)
        pa = part if pa is None else pa + part

    merged = (gates[0] * pa + gates[1] * pr + gates[2] * pm).astype(BF16)
    h_ref[...] = x + jnp.dot(merged, wo_ref[...], preferred_element_type=F32)


def _merge(x2, g_mix, os_, lses, y_r, q_m, k_m, v_m, w_gate, b_gate, w_pa, w_pr, w_pm, w_out,
           seq):
    t = x2.shape[0]
    tm = ROW_TILE
    tiles_per_seq = seq // tm
    row = lambda i: (i, 0)
    mem_len = k_m.shape[1]
    mem_spec = pl.BlockSpec((1, mem_len, MEM_WIDTH), lambda i: (i // tiles_per_seq, 0, 0))
    group_specs = [pl.BlockSpec((tm // d, d * GROUP_WIDTH), row) for d in _DILATIONS]
    n_stages = 2 * sum(d > 1 for d in _DILATIONS)
    return pl.pallas_call(
        _merge_kernel,
        grid=(t // tm,),
        in_specs=[
            pl.BlockSpec((tm, D_MODEL), row), _resident((1, D_MODEL)),
            *group_specs, *group_specs,
            pl.BlockSpec((RET_PAIRS, tm, 2 * RET_V_DIM), lambda i: (0, i, 0)),
            pl.BlockSpec((tm, MEM_WIDTH), row),
            mem_spec, mem_spec,
            _resident(w_gate.shape), _resident(b_gate.shape), _resident(w_pa.shape),
            _resident(w_pr.shape), _resident(w_pm.shape), _resident(w_out.shape),
        ],
        out_specs=pl.BlockSpec((tm, D_MODEL), row),
        out_shape=jax.ShapeDtypeStruct((t, D_MODEL), F32),
        scratch_shapes=[pltpu.VMEM((GROUP_WIDTH // LANES, tm, LANES), F32)] * n_stages,
        compiler_params=_params(1),
        name="merge",
    )(x2, g_mix, *os_, *lses, y_r, q_m, k_m, v_m, w_gate, b_gate, w_pa, w_pr, w_pm, w_out)


FF_CHUNK = MXU_WIDTH
FF_GROUP_CHUNKS = (4, 4, 3)
HALO = SUBLANES


N_FF_CHUNKS = D_FF // FF_CHUNK


def _ffn_kernel(h_ref, prev_ref, next_ref, gffn_ref, wup_ref, cw_ref, cb_ref, wdown_ref,
                gfin_ref, out_ref, n_ref, *scratch, tiles_per_seq):
    u_refs, act_refs, y_ref = scratch[:N_FF_CHUNKS], scratch[N_FF_CHUNKS:-1], scratch[-1]
    i = pl.program_id(0)
    tm = h_ref.shape[0]
    g = gffn_ref[...]
    has_prev = (i % tiles_per_seq) != 0
    has_next = (i % tiles_per_seq) != tiles_per_seq - 1
    n_prev = jnp.where(has_prev, _rms(prev_ref[...], g), 0.0)
    n_next = jnp.where(has_next, _rms(next_ref[...], g), 0.0)
    n_ref[...] = jnp.concatenate([n_prev, _rms(h_ref[...], g), n_next], axis=0).astype(BF16)

    half_rows = tm // 2
    lane_blocks = FF_CHUNK // LANES

    def conv(u_ref, ab, c0, parity):
        parts = []
        for jb in range(lane_blocks):
            sl = slice(c0 + jb * LANES, c0 + (jb + 1) * LANES)
            taps = [u_ref[ab, jb, pl.ds(HALO - 1 + parity + k, half_rows, stride=2), :]
                    * cw_ref[k:k + 1, sl] for k in range(3)]
            parts.append(taps[0] + taps[1] + taps[2] + cb_ref[:, sl])
        return jnp.concatenate(parts, axis=-1)

    def up(chunk):
        c0 = chunk * FF_CHUNK
        for ab, col in enumerate((c0, D_FF + c0)):
            u = jnp.dot(n_ref[...], wup_ref[:, col:col + FF_CHUNK], preferred_element_type=F32)
            for jb in range(lane_blocks):
                u_refs[chunk][ab, jb] = u[:, jb * LANES:(jb + 1) * LANES]

    def gate(chunk, act_ref, col):
        c0 = chunk * FF_CHUNK
        for parity in range(2):
            a = conv(u_refs[chunk], 0, c0, parity)
            b = conv(u_refs[chunk], 1, D_FF + c0, parity)
            act_ref[parity * half_rows:(parity + 1) * half_rows, col:col + FF_CHUNK] = (
                (a * _sigmoid(a) * b).astype(BF16))

    def down(gi):
        grp = groups[gi]
        return jnp.dot(act_refs[gi][...], wdown_ref[grp[0] * FF_CHUNK:(grp[-1] + 1) * FF_CHUNK, :],
                       preferred_element_type=F32)

    assert sum(FF_GROUP_CHUNKS) == N_FF_CHUNKS
    first = [sum(FF_GROUP_CHUNKS[:g]) for g in range(len(FF_GROUP_CHUNKS))]
    groups = [range(f, f + n) for f, n in zip(first, FF_GROUP_CHUNKS)]
    y = None
    for step in range(len(groups) + 2):
        if step < len(groups):
            for chunk in groups[step]:
                up(chunk)
        if 2 <= step:
            part = down(step - 2)
            y = part if y is None else y + part
        if 1 <= step <= len(groups):
            for j, chunk in enumerate(groups[step - 1]):
                gate(chunk, act_refs[step - 1], j * FF_CHUNK)
    for parity in range(2):
        for jb in range(D_MODEL // LANES):
            y_ref[jb, pl.ds(parity, half_rows, stride=2), :] = (
                y[parity * half_rows:(parity + 1) * half_rows, jb * LANES:(jb + 1) * LANES])
    y_tok = jnp.concatenate([y_ref[jb] for jb in range(D_MODEL // LANES)], axis=-1)
    out_ref[...] = _rms(h_ref[...] + y_tok, gfin_ref[...])


def _ffn(h, g_ffn, w_up, conv_w, conv_b, w_down, g_final, seq):
    t = h.shape[0]
    tm = ROW_TILE
    tiles_per_seq = seq // tm
    halo_blocks = tm // HALO
    last_block = t // HALO - 1
    row = lambda i: (i, 0)
    return pl.pallas_call(
        functools.partial(_ffn_kernel, tiles_per_seq=tiles_per_seq),
        grid=(t // tm,),
        in_specs=[
            pl.BlockSpec((tm, D_MODEL), row),
            pl.BlockSpec((HALO, D_MODEL), lambda i: (jnp.maximum(i * halo_blocks - 1, 0), 0)),
            pl.BlockSpec((HALO, D_MODEL),
                         lambda i: (jnp.minimum((i + 1) * halo_blocks, last_block), 0)),
            _resident((1, D_MODEL)), _resident(w_up.shape), _resident(conv_w.shape),
            _resident(conv_b.shape), _resident(w_down.shape), _resident((1, D_MODEL)),
        ],
        out_specs=pl.BlockSpec((tm, D_MODEL), row),
        out_shape=jax.ShapeDtypeStruct((t, D_MODEL), F32),
        scratch_shapes=[pltpu.VMEM((tm + 2 * HALO, D_MODEL), BF16),
                        *[pltpu.VMEM((2, FF_CHUNK // LANES, tm + 2 * HALO, LANES), F32)]
                        * N_FF_CHUNKS,
                        *[pltpu.VMEM((tm, n * FF_CHUNK), BF16) for n in FF_GROUP_CHUNKS],
                        pltpu.VMEM((D_MODEL // LANES, tm, LANES), F32)],
        compiler_params=_params(1),
        name="ffn",
    )(h, h, h, g_ffn, w_up, conv_w, conv_b, w_down, g_final)


def _rotary_tables(seq):
    inv = ROPE_THETA ** (-jnp.arange(0, HEAD_DIM, 2, dtype=F32) / HEAD_DIM)
    ang = jnp.arange(seq, dtype=F32)[:, None] * inv[None, :]
    cos, sin = jnp.cos(ang), jnp.sin(ang)
    reps = LANES // HEAD_DIM
    cos_t = jnp.tile(jnp.concatenate([cos, cos], axis=-1), (1, reps))
    sin_t = jnp.tile(jnp.concatenate([-sin, sin], axis=-1), (1, reps))
    return cos_t, sin_t


def _pair_lanes(v):
    pairs = v.reshape(RET_PAIRS, 2, 1)
    lanes = jnp.broadcast_to(pairs, (RET_PAIRS, 2, RET_QK_DIM)).reshape(RET_PAIRS, 1, LANES)
    return jnp.broadcast_to(lanes, (RET_PAIRS, SUBLANES, LANES)).astype(F32)


def _layer(h2, mem, batch, seq, g_mix, w_in, w_mem_kv, g_mem, decay_fwd, decay_bwd, g_ret,
           w_proj_attn, w_proj_ret, w_proj_mem, w_gate, b_gate, w_out):
    bf = lambda w: w.astype(BF16)
    row = lambda v: v.reshape(1, -1).astype(F32)
    cos_t, sin_t = _rotary_tables(seq)
    proj = _in_proj(h2, row(g_mix), cos_t, sin_t, bf(w_in), seq)
    qa, ka, va = proj[0:3], proj[3:6], proj[6:9]
    q_r, k_r, v_r, g_r, q_m = proj[9:]

    os_, lses = [], []
    for g, (_, dilation) in enumerate(ATTN_GROUPS):
        o, lse = _band_attn(qa[g], ka[g], va[g], batch, seq, dilation)
        os_.append(o)
        lses.append(lse)

    y_r = _retention(q_r, k_r, v_r, g_r, _pair_lanes(decay_fwd), _pair_lanes(decay_bwd),
                     row(g_ret), batch, seq)
    k_m, v_m = _mem_kv(mem, row(g_mem), bf(w_mem_kv))
    return _merge(h2, row(g_mix), os_, lses, y_r, q_m, k_m, v_m, bf(w_gate), row(b_gate),
                  bf(w_proj_attn), bf(w_proj_ret), bf(w_proj_mem), bf(w_out), seq)


def kernel(x, mem, g_mix, w_in, w_mem_kv, g_mem, ret_decay_fwd, ret_decay_bwd, g_ret,
           w_proj_attn, w_proj_ret, w_proj_mem, w_gate, b_gate, w_out,
           g_ffn, w_up, conv_w, conv_b, w_down, g_final):
    batch, seq, d = x.shape
    depth = w_in.shape[0]
    assert d == D_MODEL and depth == 1 and seq % ROW_TILE == 0 and seq % WIDE_ROW_TILE == 0
    h2 = x.reshape(batch * seq, d)
    l = 0
    h2 = _layer(h2, mem, batch, seq, g_mix[l], w_in[l], w_mem_kv[l], g_mem[l],
                ret_decay_fwd[l], ret_decay_bwd[l], g_ret[l], w_proj_attn[l], w_proj_ret[l],
                w_proj_mem[l], w_gate[l], b_gate[l], w_out[l])
    out = _ffn(h2, g_ffn[l].reshape(1, -1), w_up[l].astype(BF16), conv_w[l],
               conv_b[l].reshape(1, -1), w_down[l].astype(BF16), g_final.reshape(1, -1), seq)
    return out.reshape(batch, seq, d)
```

```python
import functools
import math

import jax
import jax.numpy as jnp
from jax import lax
from jax.experimental import pallas as pl
from jax.experimental.pallas import tpu as pltpu

D_MODEL = 1024
HEAD_DIM = 64
ATTN_GROUPS = ((128, 1), (512, 4), (2048, 16))
GROUP_WIDTH = 4 * HEAD_DIM
ATTN_WIDTH = 3 * GROUP_WIDTH
BAND_HALF = 64
RET_HEADS = 6
RET_QK_DIM = 64
RET_V_DIM = 128
RET_QK_WIDTH = RET_HEADS * RET_QK_DIM
RET_V_WIDTH = RET_HEADS * RET_V_DIM
MEM_HEADS = 4
MEM_HEAD_DIM = 128
MEM_WIDTH = MEM_HEADS * MEM_HEAD_DIM
D_FF = 2816
ROPE_THETA = 10000.0
EPS = 1e-6
NEG_INF = -1e30
LOG2_E = math.log2(math.e)

LANES = 128
SUBLANES = 8
MXU_WIDTH = 256
VMEM_LIMIT = 56 * 1024 * 1024

ROW_TILE = 512
WIDE_ROW_TILE = 1024
RET_CHUNK = 256
Q_BLOCK = 128
K_SPAN = Q_BLOCK + 2 * BAND_HALF
MAX_ROW_STRIDE = 4
STATE_UNROLL = 4
MIX_UNROLL = 4
ATTN_PAIRS_IN_FLIGHT = 8

BF16 = jnp.bfloat16
F32 = jnp.float32


def _params(n_grid_axes):
    return pltpu.CompilerParams(
        dimension_semantics=("arbitrary",) * n_grid_axes, vmem_limit_bytes=VMEM_LIMIT)


def _resident(shape):
    nd = len(shape)
    return pl.BlockSpec(shape, lambda *_: (0,) * nd, pipeline_mode=pl.Buffered(1))


def _rms(x, g):
    return x * lax.rsqrt(jnp.mean(x * x, axis=-1, keepdims=True) + EPS) * g


def _qk_lane_is_head0(lane):
    return lane < HEAD_DIM


def _sigmoid(x):
    return 0.5 * jnp.tanh(0.5 * x) + 0.5


_DILATIONS = tuple(d for _, d in ATTN_GROUPS)
RET_PAIRS = RET_HEADS // 2
_IN_SEGMENTS = (
    *[(GROUP_WIDTH, True, HEAD_DIM ** -0.5 * LOG2_E, d, 1) for d in _DILATIONS],
    *[(GROUP_WIDTH, True, 1.0, d, 1) for d in _DILATIONS],
    *[(GROUP_WIDTH, False, 1.0, d, 1) for d in _DILATIONS],
    (RET_QK_WIDTH, True, 1.0, 1, RET_PAIRS),
    (RET_QK_WIDTH, True, RET_QK_DIM ** -0.5, 1, RET_PAIRS),
    (RET_V_WIDTH, False, 1.0, 1, RET_PAIRS),
    (RET_V_WIDTH, False, 1.0, 1, RET_PAIRS),
    (MEM_WIDTH, False, 1.0, 1, 1),
)


def _in_proj_kernel(x_ref, g_ref, cos_ref, sin_ref, w_ref, *refs):
    out_refs, stage_ref, stage2_ref = refs[:-2], refs[-2], refs[-1]
    tm = x_ref.shape[0]
    n = _rms(x_ref[...], g_ref[...]).astype(BF16)
    cos = cos_ref[...]
    sin = sin_ref[...]
    lane = lax.broadcasted_iota(jnp.int32, cos.shape, 1)
    low_half = (lane & (HEAD_DIM // 2)) == 0

    def rotate(a):
        partner = jnp.where(low_half, pltpu.roll(a, LANES - HEAD_DIM // 2, 1),
                            pltpu.roll(a, HEAD_DIM // 2, 1))
        return a * cos + partner * sin

    col = 0
    for out_ref, (width, rotary, scale, dil, parts) in zip(out_refs, _IN_SEGMENTS):
        for c in range(0, width, MXU_WIDTH):
            cw = min(MXU_WIDTH, width - c)
            acc = jnp.dot(n, w_ref[:, col + c:col + c + cw], preferred_element_type=F32)
            for j in range(0, cw, LANES):
                a = acc[:, j:j + LANES]
                if rotary:
                    a = rotate(a)
                if scale != 1.0:
                    a = a * scale
                if parts > 1:
                    part, off = divmod(c + j, width // parts)
                    out_ref[part, :, off:off + LANES] = a.astype(BF16)
                elif dil == 1:
                    out_ref[:, c + j:c + j + LANES] = a.astype(BF16)
                else:
                    stage_ref[j // LANES] = a
            if dil > 1:
                assert width == cw == stage_ref.shape[0] * LANES
                for jb in range(width // LANES):
                    if dil > MAX_ROW_STRIDE:
                        assert dil == MAX_ROW_STRIDE * MAX_ROW_STRIDE
                        n1 = tm // MAX_ROW_STRIDE
                        for r1 in range(MAX_ROW_STRIDE):
                            stage2_ref[jb, r1 * n1:(r1 + 1) * n1, :] = (
                                stage_ref[jb, pl.ds(r1, n1, stride=MAX_ROW_STRIDE), :])
                        residue_rows = [
                            (r1 + MAX_ROW_STRIDE * r2,
                             stage2_ref[jb, pl.ds(r1 * n1 + r2, tm // dil, stride=MAX_ROW_STRIDE), :])
                            for r1 in range(MAX_ROW_STRIDE) for r2 in range(MAX_ROW_STRIDE)]
                    else:
                        residue_rows = [(r, stage_ref[jb, pl.ds(r, tm // dil, stride=dil), :])
                                        for r in range(dil)]
                    for r, rows in residue_rows:
                        out_ref[:, r * width + jb * LANES:r * width + (jb + 1) * LANES] = (
                            rows.astype(BF16))
        col += width


def _in_proj(x2, g_mix, cos_t, sin_t, w_in, seq):
    t = x2.shape[0]
    tm = WIDE_ROW_TILE
    tiles_per_seq = seq // tm
    row = lambda i: (i, 0)
    pos = lambda i: (i % tiles_per_seq, 0)
    out_shape, out_specs = [], []
    for (w, _, _, d, parts) in _IN_SEGMENTS:
        if parts > 1:
            out_shape.append(jax.ShapeDtypeStruct((parts, t, w // parts), BF16))
            out_specs.append(pl.BlockSpec((parts, tm, w // parts), lambda i: (0, i, 0)))
        else:
            out_shape.append(jax.ShapeDtypeStruct((t // d, d * w), BF16))
            out_specs.append(pl.BlockSpec((tm // d, d * w), row))
    return pl.pallas_call(
        _in_proj_kernel,
        grid=(t // tm,),
        in_specs=[
            pl.BlockSpec((tm, D_MODEL), row),
            _resident((1, D_MODEL)),
            pl.BlockSpec((tm, LANES), pos),
            pl.BlockSpec((tm, LANES), pos),
            _resident(w_in.shape),
        ],
        out_specs=out_specs,
        out_shape=out_shape,
        scratch_shapes=[pltpu.VMEM((GROUP_WIDTH // LANES, tm, LANES), F32)] * 2,
        compiler_params=_params(1),
        name="in_proj",
    )(x2, g_mix, cos_t, sin_t, w_in)


def _band_attn_kernel(q_ref, k_ref, v_ref, o_ref, lse_ref, *, seq_len, n_res):
    n_blocks = seq_len // Q_BLOCK
    qi = lax.broadcasted_iota(jnp.int32, (Q_BLOCK, K_SPAN), 0)
    kj = lax.broadcasted_iota(jnp.int32, (Q_BLOCK, K_SPAN), 1)
    rel = qi - kj
    lane = lax.broadcasted_iota(jnp.int32, (Q_BLOCK, LANES), 1)
    head0 = lane < HEAD_DIM
    qk_head0 = _qk_lane_is_head0(lane)

    def band_bias(q0, k0):
        return jnp.where(jnp.abs(rel + (q0 - k0)) <= BAND_HALF, 0.0, NEG_INF)

    def block(q0, k0, bias):
        for pair in range(n_res * 2):
            c0 = pair * LANES
            qp = q_ref[0, pl.ds(q0, Q_BLOCK), c0:c0 + LANES]
            kp = k_ref[0, pl.ds(k0, K_SPAN), c0:c0 + LANES]
            vp = v_ref[0, pl.ds(k0, K_SPAN), c0:c0 + LANES]
            v_ones = jnp.concatenate([vp, jnp.ones_like(vp)], axis=-1)
            outs = []
            for h in range(2):
                mask_h = qk_head0 if h == 0 else jnp.logical_not(qk_head0)
                qh = jnp.where(mask_h, qp, jnp.zeros_like(qp))
                s = lax.dot_general(qh, kp, (((1,), (1,)), ((), ())),
                                    preferred_element_type=F32) + bias
                m = jnp.max(s, axis=-1, keepdims=True)
                p = jnp.exp2(s - m).astype(BF16)
                ol = jnp.dot(p, v_ones, preferred_element_type=F32)
                o, l = ol[:, :LANES], ol[:, LANES:]
                outs.append((o / l, m + jnp.log2(l)))
            o_pair = jnp.where(head0, outs[0][0], outs[1][0])
            lse_pair = jnp.where(head0, outs[0][1], outs[1][1])
            o_ref[0, pl.ds(q0, Q_BLOCK), c0:c0 + LANES] = o_pair.astype(BF16)
            lse_ref[0, pl.ds(q0, Q_BLOCK), c0:c0 + LANES] = lse_pair

    last_q0 = seq_len - Q_BLOCK
    last_k0 = seq_len - K_SPAN
    block(0, 0, band_bias(0, 0))
    if n_blocks > 2:
        mid_bias = band_bias(BAND_HALF, 0)

        def interior(i, carry):
            q0 = pl.multiple_of(i * Q_BLOCK, Q_BLOCK)
            k0 = pl.multiple_of(i * Q_BLOCK - BAND_HALF, BAND_HALF)
            block(q0, k0, mid_bias)
            return carry

        unroll = max(1, ATTN_PAIRS_IN_FLIGHT // (2 * n_res))
        lax.fori_loop(1, n_blocks - 1, interior, 0, unroll=unroll)
    block(last_q0, last_k0, band_bias(last_q0, last_k0))


def _band_attn(q, k, v, batch, seq, dilation):
    seq_len = seq // dilation
    n_res = min(dilation, 4)
    width = dilation * GROUP_WIDTH
    view = lambda a: a.reshape(batch, seq_len, width)
    blk = (1, seq_len, n_res * GROUP_WIDTH)
    idx = lambda b, r: (b, 0, r)
    o, lse = pl.pallas_call(
        functools.partial(_band_attn_kernel, seq_len=seq_len, n_res=n_res),
        grid=(batch, dilation // n_res),
        in_specs=[pl.BlockSpec(blk, idx)] * 3,
        out_specs=[pl.BlockSpec(blk, idx)] * 2,
        out_shape=[jax.ShapeDtypeStruct((batch, seq_len, width), BF16),
                   jax.ShapeDtypeStruct((batch, seq_len, width), F32)],
        compiler_params=_params(2),
        name=f"band_attn_d{dilation}",
    )(view(q), view(k), view(v))
    return o.reshape(batch * seq_len, width), lse.reshape(batch * seq_len, width)


def _log_sigmoid(x):
    return jnp.minimum(x, 0.0) - jnp.log1p(jnp.exp(-jnp.abs(x)))


def _retention_kernel(decf_ref, decb_ref, q_ref, k_ref, v_ref, gate_ref, gret_ref, o_ref,
                      st_ref, s_ref, y_ref, zeta_ref, cdec_ref, xi_ref, din_ref, *, seq):
    c = RET_CHUNK
    n_chunks = seq // c
    pair_w = 2 * RET_QK_DIM
    dv = RET_V_DIM
    st_shape = (pair_w, 2 * RET_V_DIM)

    @pl.when(pl.program_id(1) == 0)
    def _():
        lg_f = _log_sigmoid(decf_ref[0])
        lg_b = _log_sigmoid(decb_ref[0])
        pos = lax.broadcasted_iota(jnp.int32, (c, pair_w), 0).astype(F32)
        zeta_ref[0] = jnp.exp(lg_f[0:1, :] * (c - 1.0 - pos))
        zeta_ref[1] = jnp.exp(lg_b[0:1, :] * pos)
        bcast = lambda lg, h, shape: jnp.broadcast_to(
            lg[0:1, h * RET_QK_DIM:h * RET_QK_DIM + 1], shape)
        row_is_h0 = _qk_lane_is_head0(lax.broadcasted_iota(jnp.int32, st_shape, 0))
        for d, lg in enumerate((lg_f, lg_b)):
            cdec_ref[d] = jnp.exp(
                jnp.where(row_is_h0, bcast(lg, 0, st_shape), bcast(lg, 1, st_shape)) * float(c))
        ci = lax.broadcasted_iota(jnp.int32, (c, c), 0)
        mi = lax.broadcasted_iota(jnp.int32, (c, c), 1)
        delta = (ci - mi).astype(F32)
        for h in range(2):
            xi_ref[h] = jnp.concatenate([jnp.exp(bcast(lg_f, h, (c, dv)) * (pos + 1.0)),
                                         jnp.exp(bcast(lg_b, h, (c, dv)) * (c - pos))], axis=-1)
            din_ref[h] = jnp.where(
                ci >= mi, jnp.exp(bcast(lg_f, h, (c, c)) * jnp.maximum(delta, 0.0)),
                jnp.exp(bcast(lg_b, h, (c, c)) * jnp.maximum(-delta, 0.0)))

    def row0(n):
        return n * c if isinstance(n, int) else pl.multiple_of(n * c, c)

    def chunk_kv(n, direction):
        r0 = row0(n)
        kz = (k_ref[0, pl.ds(r0, c), :].astype(F32) * zeta_ref[direction]).astype(BF16)
        return lax.dot_general(kz, v_ref[0, pl.ds(r0, c), :], (((0,), (0,)), ((), ())),
                               preferred_element_type=F32)

    lane = lax.broadcasted_iota(jnp.int32, (c, pair_w), 1)
    head0 = _qk_lane_is_head0(lane)

    def head_query(qp, h):
        mask_h = head0 if h == 0 else jnp.logical_not(head0)
        return jnp.where(mask_h, qp, jnp.zeros_like(qp))

    def score_chunk(n):
        r0 = row0(n)
        qp = q_ref[0, pl.ds(r0, c), :]
        kp = k_ref[0, pl.ds(r0, c), :]
        for h in range(2):
            s = lax.dot_general(head_query(qp, h), kp, (((1,), (1,)), ((), ())),
                                preferred_element_type=F32)
            s_ref[pl.ds(r0, c), h * c:(h + 1) * c] = (s * din_ref[h]).astype(BF16)

    def state_step(j, carry):
        st_f, st_b = carry
        jb = n_chunks - 1 - j
        for h in range(2):
            hs = slice(h * dv, (h + 1) * dv)
            st_ref[j, :, (2 * h) * dv:(2 * h + 1) * dv] = st_f[:, hs].astype(BF16)
            st_ref[jb, :, (2 * h + 1) * dv:(2 * h + 2) * dv] = st_b[:, hs].astype(BF16)
        st_f = st_f * cdec_ref[0] + chunk_kv(j, 0)
        st_b = st_b * cdec_ref[1] + chunk_kv(jb, 1)
        score_chunk(j)
        return st_f, st_b

    zero_state = jnp.zeros((pair_w, 2 * RET_V_DIM), F32)
    lax.fori_loop(0, n_chunks, state_step, (zero_state, zero_state), unroll=STATE_UNROLL)

    def mix_chunk(n):
        r0 = row0(n)
        qp = q_ref[0, pl.ds(r0, c), :]
        for h in range(2):
            vs = slice(h * dv, (h + 1) * dv)
            y = jnp.dot(s_ref[pl.ds(r0, c), h * c:(h + 1) * c], v_ref[0, pl.ds(r0, c), vs],
                        preferred_element_type=F32)
            cross = jnp.dot(head_query(qp, h), st_ref[n, :, 2 * h * dv:(2 * h + 2) * dv],
                            preferred_element_type=F32) * xi_ref[h]
            y_ref[pl.ds(r0, c), vs] = y + cross[:, :dv] + cross[:, dv:]

    def norm_chunk(n):
        r0 = row0(n)
        for h in range(2):
            vs = slice(h * dv, (h + 1) * dv)
            y = y_ref[pl.ds(r0, c), vs]
            mu = jnp.mean(y, axis=-1, keepdims=True)
            yc = y - mu
            var = jnp.mean(yc * yc, axis=-1, keepdims=True)
            yn = yc * lax.rsqrt(var + EPS) * gret_ref[:, vs]
            gate = gate_ref[0, pl.ds(r0, c), vs].astype(F32)
            o_ref[0, pl.ds(r0, c), vs] = (yn * (gate * _sigmoid(gate))).astype(BF16)

    n_batches = n_chunks // MIX_UNROLL

    def mix_norm(i, carry):
        for u in range(MIX_UNROLL):
            mix_chunk(i * MIX_UNROLL + u)
            norm_chunk((i - 1) * MIX_UNROLL + u)
        return carry

    for u in range(MIX_UNROLL):
        mix_chunk(u)
    lax.fori_loop(1, n_batches, mix_norm, 0)
    for u in range(MIX_UNROLL):
        norm_chunk((n_batches - 1) * MIX_UNROLL + u)


def _retention(qr, kr, vr, gr, dec_f, dec_b, g_ret, batch, seq):
    pairs = RET_PAIRS
    n_chunks = seq // RET_CHUNK
    v3 = lambda a: a.reshape(pairs * batch, seq, a.shape[-1])
    slab = lambda p, b: (p * batch + b, 0, 0)
    qk_spec = pl.BlockSpec((1, seq, 2 * RET_QK_DIM), slab)
    v_spec = pl.BlockSpec((1, seq, 2 * RET_V_DIM), slab)
    dec_spec = pl.BlockSpec((1, SUBLANES, LANES), lambda p, b: (p, 0, 0))
    out = pl.pallas_call(
        functools.partial(_retention_kernel, seq=seq),
        grid=(pairs, batch),
        in_specs=[dec_spec, dec_spec, qk_spec, qk_spec, v_spec, v_spec,
                  pl.BlockSpec((1, 2 * RET_V_DIM), lambda p, b: (0, p))],
        out_specs=v_spec,
        out_shape=jax.ShapeDtypeStruct((pairs * batch, seq, 2 * RET_V_DIM), BF16),
        scratch_shapes=[pltpu.VMEM((n_chunks, 2 * RET_QK_DIM, 4 * RET_V_DIM), BF16),
                        pltpu.VMEM((seq, 2 * RET_CHUNK), BF16),
                        pltpu.VMEM((seq, 2 * RET_V_DIM), F32),
                        pltpu.VMEM((2, RET_CHUNK, 2 * RET_QK_DIM), F32),
                        pltpu.VMEM((2, 2 * RET_QK_DIM, 2 * RET_V_DIM), F32),
                        pltpu.VMEM((2, RET_CHUNK, 2 * RET_V_DIM), F32),
                        pltpu.VMEM((2, RET_CHUNK, RET_CHUNK), F32)],
        compiler_params=_params(2),
        name="retention",
    )(dec_f, dec_b, v3(qr), v3(kr), v3(vr), v3(gr), g_ret)
    return out.reshape(pairs, batch * seq, 2 * RET_V_DIM)


def _mem_kv_kernel(mem_ref, g_ref, w_ref, k_ref, v_ref):
    n = _rms(mem_ref[0], g_ref[...]).astype(BF16)
    kv = jnp.dot(n, w_ref[...], preferred_element_type=F32)
    k_ref[0] = kv[:, :MEM_WIDTH].astype(BF16)
    v_ref[0] = kv[:, MEM_WIDTH:].astype(BF16)


def _mem_kv(mem, g_mem, w_mem_kv):
    batch, mem_len, _ = mem.shape
    out = jax.ShapeDtypeStruct((batch, mem_len, MEM_WIDTH), BF16)
    spec = pl.BlockSpec((1, mem_len, MEM_WIDTH), lambda b: (b, 0, 0))
    return pl.pallas_call(
        _mem_kv_kernel,
        grid=(batch,),
        in_specs=[pl.BlockSpec((1, mem_len, D_MODEL), lambda b: (b, 0, 0)),
                  _resident((1, D_MODEL)), _resident(w_mem_kv.shape)],
        out_specs=[spec, spec],
        out_shape=[out, out],
        compiler_params=_params(1),
        name="mem_kv",
    )(mem, g_mem, w_mem_kv)


def _merge_kernel(x_ref, g_ref, o0_ref, o1_ref, o2_ref, l0_ref, l1_ref, l2_ref, yr_ref, qm_ref,
                  km_ref, vm_ref, wg_ref, bg_ref, wpa_ref, wpr_ref, wpm_ref, wo_ref, h_ref,
                  *stage_refs):
    x = x_ref[...]
    tm = x.shape[0]
    n = _rms(x, g_ref[...]).astype(BF16)
    gates = [_sigmoid(jnp.dot(n, wg_ref[:, i * D_MODEL:(i + 1) * D_MODEL],
                              preferred_element_type=F32)
                      + bg_ref[:, i * D_MODEL:(i + 1) * D_MODEL]) for i in range(3)]

    def token_rows(ref, dil, stages):
        if dil == 1:
            return ref[...].astype(F32)
        stage_ref = next(stages)
        n_lane_blocks = GROUP_WIDTH // LANES
        residue = lambda r, jb: ref[:, r * GROUP_WIDTH + jb * LANES:
                                    r * GROUP_WIDTH + (jb + 1) * LANES].astype(F32)
        if dil > MAX_ROW_STRIDE:
            assert dil == MAX_ROW_STRIDE * MAX_ROW_STRIDE
            stage2_ref = next(stages)
            n1 = tm // MAX_ROW_STRIDE
            for jb in range(n_lane_blocks):
                for r1 in range(MAX_ROW_STRIDE):
                    for r2 in range(MAX_ROW_STRIDE):
                        stage2_ref[jb, pl.ds(r1 * n1 + r2, tm // dil, stride=MAX_ROW_STRIDE), :] = (
                            residue(r1 + MAX_ROW_STRIDE * r2, jb))
                    stage_ref[jb, pl.ds(r1, n1, stride=MAX_ROW_STRIDE), :] = (
                        stage2_ref[jb, r1 * n1:(r1 + 1) * n1, :])
        else:
            for r in range(dil):
                for jb in range(n_lane_blocks):
                    stage_ref[jb, pl.ds(r, tm // dil, stride=dil), :] = residue(r, jb)
        return jnp.concatenate([stage_ref[jb] for jb in range(n_lane_blocks)], axis=-1)

    y_r = jnp.concatenate([yr_ref[p] for p in range(RET_PAIRS)], axis=-1)
    pr = jnp.dot(y_r, wpr_ref[...], preferred_element_type=F32)

    ym = []
    for hd in range(MEM_HEADS):
        cs = slice(hd * MEM_HEAD_DIM, (hd + 1) * MEM_HEAD_DIM)
        s = lax.dot_general(qm_ref[:, cs], km_ref[0, :, cs], (((1,), (1,)), ((), ())),
                            preferred_element_type=F32) * (MEM_HEAD_DIM ** -0.5 * LOG2_E)
        p = jnp.exp2(s - jnp.max(s, axis=-1, keepdims=True)).astype(BF16)
        vm = vm_ref[0, :, cs]
        ol = jnp.dot(p, jnp.concatenate([vm, jnp.ones_like(vm)], axis=-1),
                     preferred_element_type=F32)
        ym.append((ol[:, :MEM_HEAD_DIM] / ol[:, MEM_HEAD_DIM:]).astype(BF16))
    pm = jnp.dot(jnp.concatenate(ym, axis=-1), wpm_ref[...], preferred_element_type=F32)

    stages = iter(stage_refs)
    outs, lses = [], []
    for o_ref, l_ref, dil in zip((o0_ref, o1_ref, o2_ref), (l0_ref, l1_ref, l2_ref), _DILATIONS):
        outs.append(token_rows(o_ref, dil, stages))
        lses.append(token_rows(l_ref, dil, stages))
    top = jnp.maximum(jnp.maximum(lses[0], lses[1]), lses[2])
    es = [jnp.exp2(l - top) for l in lses]
    inv = 1.0 / (es[0] + es[1] + es[2])
    pa = None
    for g in range(len(outs)):
        y_g = (outs[g] * (es[g] * inv)).astype(BF16)
        part = jnp.dot(y_g, wpa_ref[g * GROUP_WIDTH:(g + 1) * GROUP_WIDTH, :],
                       preferred_element_type=F32)
        pa = part if pa is None else pa + part

    merged = (gates[0] * pa + gates[1] * pr + gates[2] * pm).astype(BF16)
    h_ref[...] = x + jnp.dot(merged, wo_ref[...], preferred_element_type=F32)


def _merge(x2, g_mix, os_, lses, y_r, q_m, k_m, v_m, w_gate, b_gate, w_pa, w_pr, w_pm, w_out,
           seq):
    t = x2.shape[0]
    tm = ROW_TILE
    tiles_per_seq = seq // tm
    row = lambda i: (i, 0)
    mem_len = k_m.shape[1]
    mem_spec = pl.BlockSpec((1, mem_len, MEM_WIDTH), lambda i: (i // tiles_per_seq, 0, 0))
    group_specs = [pl.BlockSpec((tm // d, d * GROUP_WIDTH), row) for d in _DILATIONS]
    n_stages = 2 * sum((d > 1) + (d > MAX_ROW_STRIDE) for d in _DILATIONS)
    return pl.pallas_call(
        _merge_kernel,
        grid=(t // tm,),
        in_specs=[
            pl.BlockSpec((tm, D_MODEL), row), _resident((1, D_MODEL)),
            *group_specs, *group_specs,
            pl.BlockSpec((RET_PAIRS, tm, 2 * RET_V_DIM), lambda i: (0, i, 0)),
            pl.BlockSpec((tm, MEM_WIDTH), row),
            mem_spec, mem_spec,
            _resident(w_gate.shape), _resident(b_gate.shape), _resident(w_pa.shape),
            _resident(w_pr.shape), _resident(w_pm.shape), _resident(w_out.shape),
        ],
        out_specs=pl.BlockSpec((tm, D_MODEL), row),
        out_shape=jax.ShapeDtypeStruct((t, D_MODEL), F32),
        scratch_shapes=[pltpu.VMEM((GROUP_WIDTH // LANES, tm, LANES), F32)] * n_stages,
        compiler_params=_params(1),
        name="merge",
    )(x2, g_mix, *os_, *lses, y_r, q_m, k_m, v_m, w_gate, b_gate, w_pa, w_pr, w_pm, w_out)


FF_CHUNK = MXU_WIDTH
FF_GROUP_CHUNKS = (4, 4, 3)
HALO = SUBLANES


N_FF_CHUNKS = D_FF // FF_CHUNK


def _ffn_kernel(h_ref, prev_ref, next_ref, gffn_ref, wup_ref, cw_ref, cb_ref, wdown_ref,
                gfin_ref, out_ref, n_ref, *scratch, tiles_per_seq):
    u_refs, act_refs, y_ref = scratch[:N_FF_CHUNKS], scratch[N_FF_CHUNKS:-1], scratch[-1]
    i = pl.program_id(0)
    tm = h_ref.shape[0]
    g = gffn_ref[...]
    has_prev = (i % tiles_per_seq) != 0
    has_next = (i % tiles_per_seq) != tiles_per_seq - 1
    n_prev = jnp.where(has_prev, _rms(prev_ref[...], g), 0.0)
    n_next = jnp.where(has_next, _rms(next_ref[...], g), 0.0)
    n_ref[...] = jnp.concatenate([n_prev, _rms(h_ref[...], g), n_next], axis=0).astype(BF16)

    half_rows = tm // 2
    lane_blocks = FF_CHUNK // LANES

    def conv(u_ref, ab, c0, parity):
        parts = []
        for jb in range(lane_blocks):
            sl = slice(c0 + jb * LANES, c0 + (jb + 1) * LANES)
            taps = [u_ref[ab, jb, pl.ds(HALO - 1 + parity + k, half_rows, stride=2), :]
                    * cw_ref[k:k + 1, sl] for k in range(3)]
            parts.append(taps[0] + taps[1] + taps[2] + cb_ref[:, sl])
        return jnp.concatenate(parts, axis=-1)

    def up(chunk):
        c0 = chunk * FF_CHUNK
        for ab, col in enumerate((c0, D_FF + c0)):
            u = jnp.dot(n_ref[...], wup_ref[:, col:col + FF_CHUNK], preferred_element_type=F32)
            for jb in range(lane_blocks):
                u_refs[chunk][ab, jb] = u[:, jb * LANES:(jb + 1) * LANES]

    def gate(chunk, act_ref, col):
        c0 = chunk * FF_CHUNK
        for parity in range(2):
            a = conv(u_refs[chunk], 0, c0, parity)
            b = conv(u_refs[chunk], 1, D_FF + c0, parity)
            act_ref[parity * half_rows:(parity + 1) * half_rows, col:col + FF_CHUNK] = (
                (a * _sigmoid(a) * b).astype(BF16))

    def down(gi):
        grp = groups[gi]
        return jnp.dot(act_refs[gi][...], wdown_ref[grp[0] * FF_CHUNK:(grp[-1] + 1) * FF_CHUNK, :],
                       preferred_element_type=F32)

    assert sum(FF_GROUP_CHUNKS) == N_FF_CHUNKS
    first = [sum(FF_GROUP_CHUNKS[:g]) for g in range(len(FF_GROUP_CHUNKS))]
    groups = [range(f, f + n) for f, n in zip(first, FF_GROUP_CHUNKS)]
    y = None
    for step in range(len(groups) + 2):
        if step < len(groups):
            for chunk in groups[step]:
                up(chunk)
        if 2 <= step:
            part = down(step - 2)
            y = part if y is None else y + part
        if 1 <= step <= len(groups):
            for j, chunk in enumerate(groups[step - 1]):
                gate(chunk, act_refs[step - 1], j * FF_CHUNK)
    for parity in range(2):
        for jb in range(D_MODEL // LANES):
            y_ref[jb, pl.ds(parity, half_rows, stride=2), :] = (
                y[parity * half_rows:(parity + 1) * half_rows, jb * LANES:(jb + 1) * LANES])
    y_tok = jnp.concatenate([y_ref[jb] for jb in range(D_MODEL // LANES)], axis=-1)
    out_ref[...] = _rms(h_ref[...] + y_tok, gfin_ref[...])


def _ffn(h, g_ffn, w_up, conv_w, conv_b, w_down, g_final, seq):
    t = h.shape[0]
    tm = ROW_TILE
    tiles_per_seq = seq // tm
    halo_blocks = tm // HALO
    last_block = t // HALO - 1
    row = lambda i: (i, 0)
    return pl.pallas_call(
        functools.partial(_ffn_kernel, tiles_per_seq=tiles_per_seq),
        grid=(t // tm,),
        in_specs=[
            pl.BlockSpec((tm, D_MODEL), row),
            pl.BlockSpec((HALO, D_MODEL), lambda i: (jnp.maximum(i * halo_blocks - 1, 0), 0)),
            pl.BlockSpec((HALO, D_MODEL),
                         lambda i: (jnp.minimum((i + 1) * halo_blocks, last_block), 0)),
            _resident((1, D_MODEL)), _resident(w_up.shape), _resident(conv_w.shape),
            _resident(conv_b.shape), _resident(w_down.shape), _resident((1, D_MODEL)),
        ],
        out_specs=pl.BlockSpec((tm, D_MODEL), row),
        out_shape=jax.ShapeDtypeStruct((t, D_MODEL), F32),
        scratch_shapes=[pltpu.VMEM((tm + 2 * HALO, D_MODEL), BF16),
                        *[pltpu.VMEM((2, FF_CHUNK // LANES, tm + 2 * HALO, LANES), F32)]
                        * N_FF_CHUNKS,
                        *[pltpu.VMEM((tm, n * FF_CHUNK), BF16) for n in FF_GROUP_CHUNKS],
                        pltpu.VMEM((D_MODEL // LANES, tm, LANES), F32)],
        compiler_params=_params(1),
        name="ffn",
    )(h, h, h, g_ffn, w_up, conv_w, conv_b, w_down, g_final)


def _rotary_tables(seq):
    inv = ROPE_THETA ** (-jnp.arange(0, HEAD_DIM, 2, dtype=F32) / HEAD_DIM)
    ang = jnp.arange(seq, dtype=F32)[:, None] * inv[None, :]
    cos, sin = jnp.cos(ang), jnp.sin(ang)
    reps = LANES // HEAD_DIM
    cos_t = jnp.tile(jnp.concatenate([cos, cos], axis=-1), (1, reps))
    sin_t = jnp.tile(jnp.concatenate([-sin, sin], axis=-1), (1, reps))
    return cos_t, sin_t


def _pair_lanes(v):
    pairs = v.reshape(RET_PAIRS, 2, 1)
    lanes = jnp.broadcast_to(pairs, (RET_PAIRS, 2, RET_QK_DIM)).reshape(RET_PAIRS, 1, LANES)
    return jnp.broadcast_to(lanes, (RET_PAIRS, SUBLANES, LANES)).astype(F32)


def _layer(h2, mem, batch, seq, g_mix, w_in, w_mem_kv, g_mem, decay_fwd, decay_bwd, g_ret,
           w_proj_attn, w_proj_ret, w_proj_mem, w_gate, b_gate, w_out):
    bf = lambda w: w.astype(BF16)
    row = lambda v: v.reshape(1, -1).astype(F32)
    cos_t, sin_t = _rotary_tables(seq)
    proj = _in_proj(h2, row(g_mix), cos_t, sin_t, bf(w_in), seq)
    qa, ka, va = proj[0:3], proj[3:6], proj[6:9]
    q_r, k_r, v_r, g_r, q_m = proj[9:]

    os_, lses = [], []
    for g, (_, dilation) in enumerate(ATTN_GROUPS):
        o, lse = _band_attn(qa[g], ka[g], va[g], batch, seq, dilation)
        os_.append(o)
        lses.append(lse)

    y_r = _retention(q_r, k_r, v_r, g_r, _pair_lanes(decay_fwd), _pair_lanes(decay_bwd),
                     row(g_ret), batch, seq)
    k_m, v_m = _mem_kv(mem, row(g_mem), bf(w_mem_kv))
    return _merge(h2, row(g_mix), os_, lses, y_r, q_m, k_m, v_m, bf(w_gate), row(b_gate),
                  bf(w_proj_attn), bf(w_proj_ret), bf(w_proj_mem), bf(w_out), seq)


def kernel(x, mem, g_mix, w_in, w_mem_kv, g_mem, ret_decay_fwd, ret_decay_bwd, g_ret,
           w_proj_attn, w_proj_ret, w_proj_mem, w_gate, b_gate, w_out,
           g_ffn, w_up, conv_w, conv_b, w_down, g_final):
    batch, seq, d = x.shape
    depth = w_in.shape[0]
    assert d == D_MODEL and depth == 1 and seq % ROW_TILE == 0 and seq % WIDE_ROW_TILE == 0
    h2 = x.reshape(batch * seq, d)
    l = 0
    h2 = _layer(h2, mem, batch, seq, g_mix[l], w_in[l], w_mem_kv[l], g_mem[l],
                ret_decay_fwd[l], ret_decay_bwd[l], g_ret[l], w_proj_attn[l], w_proj_ret[l],
                w_proj_mem[l], w_gate[l], b_gate[l], w_out[l])
    out = _ffn(h2, g_ffn[l].reshape(1, -1), w_up[l].astype(BF16), conv_w[l],
               conv_b[l].reshape(1, -1), w_down[l].astype(BF16), g_final.reshape(1, -1), seq)
    return out.reshape(batch, seq, d)
```

```python
import functools
import math

import jax
import jax.numpy as jnp
from jax import lax
from jax.experimental import pallas as pl
from jax.experimental.pallas import tpu as pltpu

D_MODEL = 1024
HEAD_DIM = 64
ATTN_GROUPS = ((128, 1), (512, 4), (2048, 16))
GROUP_WIDTH = 4 * HEAD_DIM
ATTN_WIDTH = 3 * GROUP_WIDTH
BAND_HALF = 64
RET_HEADS = 6
RET_QK_DIM = 64
RET_V_DIM = 128
RET_QK_WIDTH = RET_HEADS * RET_QK_DIM
RET_V_WIDTH = RET_HEADS * RET_V_DIM
MEM_HEADS = 4
MEM_HEAD_DIM = 128
MEM_WIDTH = MEM_HEADS * MEM_HEAD_DIM
D_FF = 2816
ROPE_THETA = 10000.0
EPS = 1e-6
NEG_INF = -1e30
LOG2_E = math.log2(math.e)

LANES = 128
SUBLANES = 8
MXU_WIDTH = 256
VMEM_LIMIT = 56 * 1024 * 1024

ROW_TILE = 512
WIDE_ROW_TILE = 1024
RET_CHUNK = 256
Q_BLOCK = 128
K_SPAN = Q_BLOCK + 2 * BAND_HALF
MAX_ROW_STRIDE = 4
STATE_UNROLL = 4
MIX_UNROLL = 4
ATTN_PAIRS_IN_FLIGHT = 8

BF16 = jnp.bfloat16
F32 = jnp.float32


def _params(n_grid_axes):
    return pltpu.CompilerParams(
        dimension_semantics=("arbitrary",) * n_grid_axes, vmem_limit_bytes=VMEM_LIMIT)


def _resident(shape):
    nd = len(shape)
    return pl.BlockSpec(shape, lambda *_: (0,) * nd, pipeline_mode=pl.Buffered(1))


def _rms(x, g):
    return x * lax.rsqrt(jnp.mean(x * x, axis=-1, keepdims=True) + EPS) * g


def _qk_lane_is_head0(lane):
    return lane < HEAD_DIM


def _sigmoid(x):
    return 0.5 * jnp.tanh(0.5 * x) + 0.5


_DILATIONS = tuple(d for _, d in ATTN_GROUPS)
RET_PAIRS = RET_HEADS // 2
_IN_SEGMENTS = (
    *[(GROUP_WIDTH, True, HEAD_DIM ** -0.5 * LOG2_E, d, 1) for d in _DILATIONS],
    *[(GROUP_WIDTH, True, 1.0, d, 1) for d in _DILATIONS],
    *[(GROUP_WIDTH, False, 1.0, d, 1) for d in _DILATIONS],
    (RET_QK_WIDTH, True, 1.0, 1, RET_PAIRS),
    (RET_QK_WIDTH, True, RET_QK_DIM ** -0.5, 1, RET_PAIRS),
    (RET_V_WIDTH, False, 1.0, 1, RET_PAIRS),
    (RET_V_WIDTH, False, 1.0, 1, RET_PAIRS),
    (MEM_WIDTH, False, 1.0, 1, 1),
)


def _in_proj_kernel(x_ref, g_ref, cos_ref, sin_ref, w_ref, *refs):
    out_refs, stage_ref, stage2_ref = refs[:-2], refs[-2], refs[-1]
    tm = x_ref.shape[0]
    n = _rms(x_ref[...], g_ref[...]).astype(BF16)
    cos = cos_ref[...]
    sin = sin_ref[...]
    lane = lax.broadcasted_iota(jnp.int32, cos.shape, 1)
    low_half = (lane & (HEAD_DIM // 2)) == 0

    def rotate(a):
        partner = jnp.where(low_half, pltpu.roll(a, LANES - HEAD_DIM // 2, 1),
                            pltpu.roll(a, HEAD_DIM // 2, 1))
        return a * cos + partner * sin

    col = 0
    for out_ref, (width, rotary, scale, dil, parts) in zip(out_refs, _IN_SEGMENTS):
        for c in range(0, width, MXU_WIDTH):
            cw = min(MXU_WIDTH, width - c)
            acc = jnp.dot(n, w_ref[:, col + c:col + c + cw], preferred_element_type=F32)
            for j in range(0, cw, LANES):
                a = acc[:, j:j + LANES]
                if rotary:
                    a = rotate(a)
                if scale != 1.0:
                    a = a * scale
                if parts > 1:
                    part, off = divmod(c + j, width // parts)
                    out_ref[part, :, off:off + LANES] = a.astype(BF16)
                elif dil == 1:
                    out_ref[:, c + j:c + j + LANES] = a.astype(BF16)
                else:
                    stage_ref[j // LANES] = a
            if dil > 1:
                assert width == cw == stage_ref.shape[0] * LANES
                for jb in range(width // LANES):
                    if dil > MAX_ROW_STRIDE:
                        assert dil == MAX_ROW_STRIDE * MAX_ROW_STRIDE
                        n1 = tm // MAX_ROW_STRIDE
                        for r1 in range(MAX_ROW_STRIDE):
                            stage2_ref[jb, r1 * n1:(r1 + 1) * n1, :] = (
                                stage_ref[jb, pl.ds(r1, n1, stride=MAX_ROW_STRIDE), :])
                        residue_rows = [
                            (r1 + MAX_ROW_STRIDE * r2,
                             stage2_ref[jb, pl.ds(r1 * n1 + r2, tm // dil, stride=MAX_ROW_STRIDE), :])
                            for r1 in range(MAX_ROW_STRIDE) for r2 in range(MAX_ROW_STRIDE)]
                    else:
                        residue_rows = [(r, stage_ref[jb, pl.ds(r, tm // dil, stride=dil), :])
                                        for r in range(dil)]
                    for r, rows in residue_rows:
                        out_ref[:, r * width + jb * LANES:r * width + (jb + 1) * LANES] = (
                            rows.astype(BF16))
        col += width


def _in_proj(x2, g_mix, cos_t, sin_t, w_in, seq):
    t = x2.shape[0]
    tm = WIDE_ROW_TILE
    tiles_per_seq = seq // tm
    row = lambda i: (i, 0)
    pos = lambda i: (i % tiles_per_seq, 0)
    out_shape, out_specs = [], []
    for (w, _, _, d, parts) in _IN_SEGMENTS:
        if parts > 1:
            out_shape.append(jax.ShapeDtypeStruct((parts, t, w // parts), BF16))
            out_specs.append(pl.BlockSpec((parts, tm, w // parts), lambda i: (0, i, 0)))
        else:
            out_shape.append(jax.ShapeDtypeStruct((t // d, d * w), BF16))
            out_specs.append(pl.BlockSpec((tm // d, d * w), row))
    return pl.pallas_call(
        _in_proj_kernel,
        grid=(t // tm,),
        in_specs=[
            pl.BlockSpec((tm, D_MODEL), row),
            _resident((1, D_MODEL)),
            pl.BlockSpec((tm, LANES), pos),
            pl.BlockSpec((tm, LANES), pos),
            _resident(w_in.shape),
        ],
        out_specs=out_specs,
        out_shape=out_shape,
        scratch_shapes=[pltpu.VMEM((GROUP_WIDTH // LANES, tm, LANES), F32)] * 2,
        compiler_params=_params(1),
        name="in_proj",
    )(x2, g_mix, cos_t, sin_t, w_in)


def _band_attn_kernel(q_ref, k_ref, v_ref, o_ref, lse_ref, *, seq_len, n_res):
    n_blocks = seq_len // Q_BLOCK
    qi = lax.broadcasted_iota(jnp.int32, (Q_BLOCK, K_SPAN), 0)
    kj = lax.broadcasted_iota(jnp.int32, (Q_BLOCK, K_SPAN), 1)
    rel = qi - kj
    lane = lax.broadcasted_iota(jnp.int32, (Q_BLOCK, LANES), 1)
    head0 = lane < HEAD_DIM
    qk_head0 = _qk_lane_is_head0(lane)

    def band_bias(q0, k0):
        return jnp.where(jnp.abs(rel + (q0 - k0)) <= BAND_HALF, 0.0, NEG_INF)

    def block(q0, k0, bias):
        for pair in range(n_res * 2):
            c0 = pair * LANES
            qp = q_ref[0, pl.ds(q0, Q_BLOCK), c0:c0 + LANES]
            kp = k_ref[0, pl.ds(k0, K_SPAN), c0:c0 + LANES]
            vp = v_ref[0, pl.ds(k0, K_SPAN), c0:c0 + LANES]
            v_ones = jnp.concatenate([vp, jnp.ones_like(vp)], axis=-1)
            outs = []
            for h in range(2):
                mask_h = qk_head0 if h == 0 else jnp.logical_not(qk_head0)
                qh = jnp.where(mask_h, qp, jnp.zeros_like(qp))
                s = lax.dot_general(qh, kp, (((1,), (1,)), ((), ())),
                                    preferred_element_type=F32) + bias
                m = jnp.max(s, axis=-1, keepdims=True)
                p = jnp.exp2(s - m).astype(BF16)
                ol = jnp.dot(p, v_ones, preferred_element_type=F32)
                o, l = ol[:, :LANES], ol[:, LANES:]
                outs.append((o / l, m + jnp.log2(l)))
            o_pair = jnp.where(head0, outs[0][0], outs[1][0])
            lse_pair = jnp.where(head0, outs[0][1], outs[1][1])
            o_ref[0, pl.ds(q0, Q_BLOCK), c0:c0 + LANES] = o_pair.astype(BF16)
            lse_ref[0, pl.ds(q0, Q_BLOCK), c0:c0 + LANES] = lse_pair

    last_q0 = seq_len - Q_BLOCK
    last_k0 = seq_len - K_SPAN
    block(0, 0, band_bias(0, 0))
    if n_blocks > 2:
        mid_bias = band_bias(BAND_HALF, 0)

        def interior(i, carry):
            q0 = pl.multiple_of(i * Q_BLOCK, Q_BLOCK)
            k0 = pl.multiple_of(i * Q_BLOCK - BAND_HALF, BAND_HALF)
            block(q0, k0, mid_bias)
            return carry

        unroll = max(1, ATTN_PAIRS_IN_FLIGHT // (2 * n_res))
        lax.fori_loop(1, n_blocks - 1, interior, 0, unroll=unroll)
    block(last_q0, last_k0, band_bias(last_q0, last_k0))


def _band_attn(q, k, v, batch, seq, dilation):
    seq_len = seq // dilation
    n_res = min(dilation, 4)
    width = dilation * GROUP_WIDTH
    view = lambda a: a.reshape(batch, seq_len, width)
    blk = (1, seq_len, n_res * GROUP_WIDTH)
    idx = lambda b, r: (b, 0, r)
    o, lse = pl.pallas_call(
        functools.partial(_band_attn_kernel, seq_len=seq_len, n_res=n_res),
        grid=(batch, dilation // n_res),
        in_specs=[pl.BlockSpec(blk, idx)] * 3,
        out_specs=[pl.BlockSpec(blk, idx)] * 2,
        out_shape=[jax.ShapeDtypeStruct((batch, seq_len, width), BF16),
                   jax.ShapeDtypeStruct((batch, seq_len, width), F32)],
        compiler_params=_params(2),
        name=f"band_attn_d{dilation}",
    )(view(q), view(k), view(v))
    return o.reshape(batch * seq_len, width), lse.reshape(batch * seq_len, width)


def _log_sigmoid(x):
    return jnp.minimum(x, 0.0) - jnp.log1p(jnp.exp(-jnp.abs(x)))


def _retention_kernel(decf_ref, decb_ref, q_ref, k_ref, v_ref, gate_ref, gret_ref, o_ref,
                      st_ref, s_ref, y_ref, zeta_ref, cdec_ref, xi_ref, din_ref, *, seq):
    c = RET_CHUNK
    n_chunks = seq // c
    pair_w = 2 * RET_QK_DIM
    dv = RET_V_DIM
    st_shape = (pair_w, 2 * RET_V_DIM)

    @pl.when(pl.program_id(1) == 0)
    def _():
        lg_f = _log_sigmoid(decf_ref[0])
        lg_b = _log_sigmoid(decb_ref[0])
        pos = lax.broadcasted_iota(jnp.int32, (c, pair_w), 0).astype(F32)
        zeta_ref[0] = jnp.exp(lg_f[0:1, :] * (c - 1.0 - pos))
        zeta_ref[1] = jnp.exp(lg_b[0:1, :] * pos)
        bcast = lambda lg, h, shape: jnp.broadcast_to(
            lg[0:1, h * RET_QK_DIM:h * RET_QK_DIM + 1], shape)
        row_is_h0 = _qk_lane_is_head0(lax.broadcasted_iota(jnp.int32, st_shape, 0))
        for d, lg in enumerate((lg_f, lg_b)):
            cdec_ref[d] = jnp.exp(
                jnp.where(row_is_h0, bcast(lg, 0, st_shape), bcast(lg, 1, st_shape)) * float(c))
        ci = lax.broadcasted_iota(jnp.int32, (c, c), 0)
        mi = lax.broadcasted_iota(jnp.int32, (c, c), 1)
        delta = (ci - mi).astype(F32)
        for h in range(2):
            xi_ref[h] = jnp.concatenate([jnp.exp(bcast(lg_f, h, (c, dv)) * (pos + 1.0)),
                                         jnp.exp(bcast(lg_b, h, (c, dv)) * (c - pos))], axis=-1)
            din_ref[h] = jnp.where(
                ci >= mi, jnp.exp(bcast(lg_f, h, (c, c)) * jnp.maximum(delta, 0.0)),
                jnp.exp(bcast(lg_b, h, (c, c)) * jnp.maximum(-delta, 0.0)))

    def row0(n):
        return n * c if isinstance(n, int) else pl.multiple_of(n * c, c)

    def chunk_kv(n, direction):
        r0 = row0(n)
        kz = (k_ref[0, pl.ds(r0, c), :].astype(F32) * zeta_ref[direction]).astype(BF16)
        return lax.dot_general(kz, v_ref[0, pl.ds(r0, c), :], (((0,), (0,)), ((), ())),
                               preferred_element_type=F32)

    lane = lax.broadcasted_iota(jnp.int32, (c, pair_w), 1)
    head0 = _qk_lane_is_head0(lane)

    def head_query(qp, h):
        mask_h = head0 if h == 0 else jnp.logical_not(head0)
        return jnp.where(mask_h, qp, jnp.zeros_like(qp))

    def score_chunk(n):
        r0 = row0(n)
        qp = q_ref[0, pl.ds(r0, c), :]
        kp = k_ref[0, pl.ds(r0, c), :]
        for h in range(2):
            s = lax.dot_general(head_query(qp, h), kp, (((1,), (1,)), ((), ())),
                                preferred_element_type=F32)
            s_ref[pl.ds(r0, c), h * c:(h + 1) * c] = (s * din_ref[h]).astype(BF16)

    def state_step(j, carry):
        st_f, st_b = carry
        jb = n_chunks - 1 - j
        for h in range(2):
            hs = slice(h * dv, (h + 1) * dv)
            st_ref[j, :, (2 * h) * dv:(2 * h + 1) * dv] = st_f[:, hs].astype(BF16)
            st_ref[jb, :, (2 * h + 1) * dv:(2 * h + 2) * dv] = st_b[:, hs].astype(BF16)
        st_f = st_f * cdec_ref[0] + chunk_kv(j, 0)
        st_b = st_b * cdec_ref[1] + chunk_kv(jb, 1)
        score_chunk(j)
        return st_f, st_b

    zero_state = jnp.zeros((pair_w, 2 * RET_V_DIM), F32)
    lax.fori_loop(0, n_chunks, state_step, (zero_state, zero_state), unroll=STATE_UNROLL)

    def mix_chunk(n):
        r0 = row0(n)
        qp = q_ref[0, pl.ds(r0, c), :]
        for h in range(2):
            vs = slice(h * dv, (h + 1) * dv)
            y = jnp.dot(s_ref[pl.ds(r0, c), h * c:(h + 1) * c], v_ref[0, pl.ds(r0, c), vs],
                        preferred_element_type=F32)
            cross = jnp.dot(head_query(qp, h), st_ref[n, :, 2 * h * dv:(2 * h + 2) * dv],
                            preferred_element_type=F32) * xi_ref[h]
            y_ref[pl.ds(r0, c), vs] = y + cross[:, :dv] + cross[:, dv:]

    def norm_chunk(n):
        r0 = row0(n)
        for h in range(2):
            vs = slice(h * dv, (h + 1) * dv)
            y = y_ref[pl.ds(r0, c), vs]
            mu = jnp.mean(y, axis=-1, keepdims=True)
            yc = y - mu
            var = jnp.mean(yc * yc, axis=-1, keepdims=True)
            yn = yc * lax.rsqrt(var + EPS) * gret_ref[:, vs]
            gate = gate_ref[0, pl.ds(r0, c), vs].astype(F32)
            o_ref[0, pl.ds(r0, c), vs] = (yn * (gate * _sigmoid(gate))).astype(BF16)

    n_batches = n_chunks // MIX_UNROLL

    def mix_norm(i, carry):
        for u in range(MIX_UNROLL):
            mix_chunk(i * MIX_UNROLL + u)
            norm_chunk((i - 1) * MIX_UNROLL + u)
        return carry

    for u in range(MIX_UNROLL):
        mix_chunk(u)
    lax.fori_loop(1, n_batches, mix_norm, 0)
    for u in range(MIX_UNROLL):
        norm_chunk((n_batches - 1) * MIX_UNROLL + u)


def _retention(qr, kr, vr, gr, dec_f, dec_b, g_ret, batch, seq):
    pairs = RET_PAIRS
    n_chunks = seq // RET_CHUNK
    v3 = lambda a: a.reshape(pairs * batch, seq, a.shape[-1])
    slab = lambda p, b: (p * batch + b, 0, 0)
    qk_spec = pl.BlockSpec((1, seq, 2 * RET_QK_DIM), slab)
    v_spec = pl.BlockSpec((1, seq, 2 * RET_V_DIM), slab)
    dec_spec = pl.BlockSpec((1, SUBLANES, LANES), lambda p, b: (p, 0, 0))
    out = pl.pallas_call(
        functools.partial(_retention_kernel, seq=seq),
        grid=(pairs, batch),
        in_specs=[dec_spec, dec_spec, qk_spec, qk_spec, v_spec, v_spec,
                  pl.BlockSpec((1, 2 * RET_V_DIM), lambda p, b: (0, p))],
        out_specs=v_spec,
        out_shape=jax.ShapeDtypeStruct((pairs * batch, seq, 2 * RET_V_DIM), BF16),
        scratch_shapes=[pltpu.VMEM((n_chunks, 2 * RET_QK_DIM, 4 * RET_V_DIM), BF16),
                        pltpu.VMEM((seq, 2 * RET_CHUNK), BF16),
                        pltpu.VMEM((seq, 2 * RET_V_DIM), F32),
                        pltpu.VMEM((2, RET_CHUNK, 2 * RET_QK_DIM), F32),
                        pltpu.VMEM((2, 2 * RET_QK_DIM, 2 * RET_V_DIM), F32),
                        pltpu.VMEM((2, RET_CHUNK, 2 * RET_V_DIM), F32),
                        pltpu.VMEM((2, RET_CHUNK, RET_CHUNK), F32)],
        compiler_params=_params(2),
        name="retention",
    )(dec_f, dec_b, v3(qr), v3(kr), v3(vr), v3(gr), g_ret)
    return out.reshape(pairs, batch * seq, 2 * RET_V_DIM)


def _mem_kv_kernel(mem_ref, g_ref, w_ref, k_ref, v_ref):
    n = _rms(mem_ref[0], g_ref[...]).astype(BF16)
    kv = jnp.dot(n, w_ref[...], preferred_element_type=F32)
    k_ref[0] = kv[:, :MEM_WIDTH].astype(BF16)
    v_ref[0] = kv[:, MEM_WIDTH:].astype(BF16)


def _mem_kv(mem, g_mem, w_mem_kv):
    batch, mem_len, _ = mem.shape
    out = jax.ShapeDtypeStruct((batch, mem_len, MEM_WIDTH), BF16)
    spec = pl.BlockSpec((1, mem_len, MEM_WIDTH), lambda b: (b, 0, 0))
    return pl.pallas_call(
        _mem_kv_kernel,
        grid=(batch,),
        in_specs=[pl.BlockSpec((1, mem_len, D_MODEL), lambda b: (b, 0, 0)),
                  _resident((1, D_MODEL)), _resident(w_mem_kv.shape)],
        out_specs=[spec, spec],
        out_shape=[out, out],
        compiler_params=_params(1),
        name="mem_kv",
    )(mem, g_mem, w_mem_kv)


def _merge_kernel(x_ref, g_ref, o0_ref, o1_ref, o2_ref, l0_ref, l1_ref, l2_ref, yr_ref, qm_ref,
                  km_ref, vm_ref, wg_ref, bg_ref, wpa_ref, wpr_ref, wpm_ref, wo_ref, h_ref,
                  *stage_refs):
    x = x_ref[...]
    tm = x.shape[0]
    n = _rms(x, g_ref[...]).astype(BF16)
    def gate(i):
        return _sigmoid(jnp.dot(n, wg_ref[:, i * D_MODEL:(i + 1) * D_MODEL],
                                preferred_element_type=F32)
                        + bg_ref[:, i * D_MODEL:(i + 1) * D_MODEL])

    def token_rows(ref, dil, stages):
        if dil == 1:
            return ref[...].astype(F32)
        stage_ref = next(stages)
        n_lane_blocks = GROUP_WIDTH // LANES
        residue = lambda r, jb: ref[:, r * GROUP_WIDTH + jb * LANES:
                                    r * GROUP_WIDTH + (jb + 1) * LANES].astype(F32)
        if dil > MAX_ROW_STRIDE:
            assert dil == MAX_ROW_STRIDE * MAX_ROW_STRIDE
            stage2_ref = next(stages)
            n1 = tm // MAX_ROW_STRIDE
            for jb in range(n_lane_blocks):
                for r1 in range(MAX_ROW_STRIDE):
                    for r2 in range(MAX_ROW_STRIDE):
                        stage2_ref[jb, pl.ds(r1 * n1 + r2, tm // dil, stride=MAX_ROW_STRIDE), :] = (
                            residue(r1 + MAX_ROW_STRIDE * r2, jb))
                    stage_ref[jb, pl.ds(r1, n1, stride=MAX_ROW_STRIDE), :] = (
                        stage2_ref[jb, r1 * n1:(r1 + 1) * n1, :])
        else:
            for r in range(dil):
                for jb in range(n_lane_blocks):
                    stage_ref[jb, pl.ds(r, tm // dil, stride=dil), :] = residue(r, jb)
        return jnp.concatenate([stage_ref[jb] for jb in range(n_lane_blocks)], axis=-1)

    y_r = jnp.concatenate([yr_ref[p] for p in range(RET_PAIRS)], axis=-1)
    pr = jnp.dot(y_r, wpr_ref[...], preferred_element_type=F32)

    ym = []
    for hd in range(MEM_HEADS):
        cs = slice(hd * MEM_HEAD_DIM, (hd + 1) * MEM_HEAD_DIM)
        s = lax.dot_general(qm_ref[:, cs], km_ref[0, :, cs], (((1,), (1,)), ((), ())),
                            preferred_element_type=F32) * (MEM_HEAD_DIM ** -0.5 * LOG2_E)
        p = jnp.exp2(s - jnp.max(s, axis=-1, keepdims=True)).astype(BF16)
        vm = vm_ref[0, :, cs]
        ol = jnp.dot(p, jnp.concatenate([vm, jnp.ones_like(vm)], axis=-1),
                     preferred_element_type=F32)
        ym.append((ol[:, :MEM_HEAD_DIM] / ol[:, MEM_HEAD_DIM:]).astype(BF16))
    pm = jnp.dot(jnp.concatenate(ym, axis=-1), wpm_ref[...], preferred_element_type=F32)

    stages = iter(stage_refs)
    outs, lses = [], []
    for o_ref, l_ref, dil in zip((o0_ref, o1_ref, o2_ref), (l0_ref, l1_ref, l2_ref), _DILATIONS):
        outs.append(token_rows(o_ref, dil, stages))
        lses.append(token_rows(l_ref, dil, stages))
    top = jnp.maximum(jnp.maximum(lses[0], lses[1]), lses[2])
    es = [jnp.exp2(l - top) for l in lses]
    inv = 1.0 / (es[0] + es[1] + es[2])
    pa = None
    for g in range(len(outs)):
        y_g = (outs[g] * (es[g] * inv)).astype(BF16)
        part = jnp.dot(y_g, wpa_ref[g * GROUP_WIDTH:(g + 1) * GROUP_WIDTH, :],
                       preferred_element_type=F32)
        pa = part if pa is None else pa + part

    merged = (gate(0) * pa + gate(1) * pr + gate(2) * pm).astype(BF16)
    h_ref[...] = x + jnp.dot(merged, wo_ref[...], preferred_element_type=F32)


def _merge(x2, g_mix, os_, lses, y_r, q_m, k_m, v_m, w_gate, b_gate, w_pa, w_pr, w_pm, w_out,
           seq):
    t = x2.shape[0]
    tm = ROW_TILE
    tiles_per_seq = seq // tm
    row = lambda i: (i, 0)
    mem_len = k_m.shape[1]
    mem_spec = pl.BlockSpec((1, mem_len, MEM_WIDTH), lambda i: (i // tiles_per_seq, 0, 0))
    group_specs = [pl.BlockSpec((tm // d, d * GROUP_WIDTH), row) for d in _DILATIONS]
    n_stages = 2 * sum((d > 1) + (d > MAX_ROW_STRIDE) for d in _DILATIONS)
    return pl.pallas_call(
        _merge_kernel,
        grid=(t // tm,),
        in_specs=[
            pl.BlockSpec((tm, D_MODEL), row), _resident((1, D_MODEL)),
            *group_specs, *group_specs,
            pl.BlockSpec((RET_PAIRS, tm, 2 * RET_V_DIM), lambda i: (0, i, 0)),
            pl.BlockSpec((tm, MEM_WIDTH), row),
            mem_spec, mem_spec,
            _resident(w_gate.shape), _resident(b_gate.shape), _resident(w_pa.shape),
            _resident(w_pr.shape), _resident(w_pm.shape), _resident(w_out.shape),
        ],
        out_specs=pl.BlockSpec((tm, D_MODEL), row),
        out_shape=jax.ShapeDtypeStruct((t, D_MODEL), F32),
        scratch_shapes=[pltpu.VMEM((GROUP_WIDTH // LANES, tm, LANES), F32)] * n_stages,
        compiler_params=_params(1),
        name="merge",
    )(x2, g_mix, *os_, *lses, y_r, q_m, k_m, v_m, w_gate, b_gate, w_pa, w_pr, w_pm, w_out)


FF_CHUNK = MXU_WIDTH
FF_GROUP_CHUNKS = (4, 4, 3)
HALO = SUBLANES


N_FF_CHUNKS = D_FF // FF_CHUNK


def _ffn_kernel(h_ref, prev_ref, next_ref, gffn_ref, wup_ref, cw_ref, cb_ref, wdown_ref,
                gfin_ref, out_ref, *scratch, tiles_per_seq):
    u_refs, act_refs, y_ref = scratch[:N_FF_CHUNKS], scratch[N_FF_CHUNKS:-1], scratch[-1]
    i = pl.program_id(0)
    tm = h_ref.shape[0]
    g = gffn_ref[...]
    has_prev = (i % tiles_per_seq) != 0
    has_next = (i % tiles_per_seq) != tiles_per_seq - 1
    n_prev = jnp.where(has_prev, _rms(prev_ref[...], g), 0.0)
    n_next = jnp.where(has_next, _rms(next_ref[...], g), 0.0)
    n_ext = jnp.concatenate([n_prev, _rms(h_ref[...], g), n_next], axis=0).astype(BF16)

    half_rows = tm // 2
    lane_blocks = FF_CHUNK // LANES

    def conv(u_ref, ab, c0, parity):
        parts = []
        for jb in range(lane_blocks):
            sl = slice(c0 + jb * LANES, c0 + (jb + 1) * LANES)
            taps = [u_ref[ab, jb, pl.ds(HALO - 1 + parity + k, half_rows, stride=2), :]
                    * cw_ref[k:k + 1, sl] for k in range(3)]
            parts.append(taps[0] + taps[1] + taps[2] + cb_ref[:, sl])
        return jnp.concatenate(parts, axis=-1)

    def up(chunk):
        c0 = chunk * FF_CHUNK
        for ab, col in enumerate((c0, D_FF + c0)):
            u = jnp.dot(n_ext, wup_ref[:, col:col + FF_CHUNK], preferred_element_type=F32)
            for jb in range(lane_blocks):
                u_refs[chunk][ab, jb] = u[:, jb * LANES:(jb + 1) * LANES]

    def gate(chunk, act_ref, col):
        c0 = chunk * FF_CHUNK
        for parity in range(2):
            a = conv(u_refs[chunk], 0, c0, parity)
            b = conv(u_refs[chunk], 1, D_FF + c0, parity)
            act_ref[parity * half_rows:(parity + 1) * half_rows, col:col + FF_CHUNK] = (
                (a * _sigmoid(a) * b).astype(BF16))

    def down(gi):
        grp = groups[gi]
        return jnp.dot(act_refs[gi][...], wdown_ref[grp[0] * FF_CHUNK:(grp[-1] + 1) * FF_CHUNK, :],
                       preferred_element_type=F32)

    assert sum(FF_GROUP_CHUNKS) == N_FF_CHUNKS
    first = [sum(FF_GROUP_CHUNKS[:g]) for g in range(len(FF_GROUP_CHUNKS))]
    groups = [range(f, f + n) for f, n in zip(first, FF_GROUP_CHUNKS)]
    y = None
    for step in range(len(groups) + 2):
        if step < len(groups):
            for chunk in groups[step]:
                up(chunk)
        if 2 <= step:
            part = down(step - 2)
            y = part if y is None else y + part
        if 1 <= step <= len(groups):
            for j, chunk in enumerate(groups[step - 1]):
                gate(chunk, act_refs[step - 1], j * FF_CHUNK)
    for parity in range(2):
        for jb in range(D_MODEL // LANES):
            y_ref[jb, pl.ds(parity, half_rows, stride=2), :] = (
                y[parity * half_rows:(parity + 1) * half_rows, jb * LANES:(jb + 1) * LANES])
    y_tok = jnp.concatenate([y_ref[jb] for jb in range(D_MODEL // LANES)], axis=-1)
    out_ref[...] = _rms(h_ref[...] + y_tok, gfin_ref[...])


def _ffn(h, g_ffn, w_up, conv_w, conv_b, w_down, g_final, seq):
    t = h.shape[0]
    tm = ROW_TILE
    tiles_per_seq = seq // tm
    halo_blocks = tm // HALO
    last_block = t // HALO - 1
    row = lambda i: (i, 0)
    return pl.pallas_call(
        functools.partial(_ffn_kernel, tiles_per_seq=tiles_per_seq),
        grid=(t // tm,),
        in_specs=[
            pl.BlockSpec((tm, D_MODEL), row),
            pl.BlockSpec((HALO, D_MODEL), lambda i: (jnp.maximum(i * halo_blocks - 1, 0), 0)),
            pl.BlockSpec((HALO, D_MODEL),
                         lambda i: (jnp.minimum((i + 1) * halo_blocks, last_block), 0)),
            _resident((1, D_MODEL)), _resident(w_up.shape), _resident(conv_w.shape),
            _resident(conv_b.shape), _resident(w_down.shape), _resident((1, D_MODEL)),
        ],
        out_specs=pl.BlockSpec((tm, D_MODEL), row),
        out_shape=jax.ShapeDtypeStruct((t, D_MODEL), F32),
        scratch_shapes=[*[pltpu.VMEM((2, FF_CHUNK // LANES, tm + 2 * HALO, LANES), F32)]
                        * N_FF_CHUNKS,
                        *[pltpu.VMEM((tm, n * FF_CHUNK), BF16) for n in FF_GROUP_CHUNKS],
                        pltpu.VMEM((D_MODEL // LANES, tm, LANES), F32)],
        compiler_params=_params(1),
        name="ffn",
    )(h, h, h, g_ffn, w_up, conv_w, conv_b, w_down, g_final)


def _rotary_tables(seq):
    inv = ROPE_THETA ** (-jnp.arange(0, HEAD_DIM, 2, dtype=F32) / HEAD_DIM)
    ang = jnp.arange(seq, dtype=F32)[:, None] * inv[None, :]
    cos, sin = jnp.cos(ang), jnp.sin(ang)
    reps = LANES // HEAD_DIM
    cos_t = jnp.tile(jnp.concatenate([cos, cos], axis=-1), (1, reps))
    sin_t = jnp.tile(jnp.concatenate([-sin, sin], axis=-1), (1, reps))
    return cos_t, sin_t


def _pair_lanes(v):
    pairs = v.reshape(RET_PAIRS, 2, 1)
    lanes = jnp.broadcast_to(pairs, (RET_PAIRS, 2, RET_QK_DIM)).reshape(RET_PAIRS, 1, LANES)
    return jnp.broadcast_to(lanes, (RET_PAIRS, SUBLANES, LANES)).astype(F32)


def _layer(h2, mem, batch, seq, g_mix, w_in, w_mem_kv, g_mem, decay_fwd, decay_bwd, g_ret,
           w_proj_attn, w_proj_ret, w_proj_mem, w_gate, b_gate, w_out):
    bf = lambda w: w.astype(BF16)
    row = lambda v: v.reshape(1, -1).astype(F32)
    cos_t, sin_t = _rotary_tables(seq)
    proj = _in_proj(h2, row(g_mix), cos_t, sin_t, bf(w_in), seq)
    qa, ka, va = proj[0:3], proj[3:6], proj[6:9]
    q_r, k_r, v_r, g_r, q_m = proj[9:]

    os_, lses = [], []
    for g, (_, dilation) in enumerate(ATTN_GROUPS):
        o, lse = _band_attn(qa[g], ka[g], va[g], batch, seq, dilation)
        os_.append(o)
        lses.append(lse)

    y_r = _retention(q_r, k_r, v_r, g_r, _pair_lanes(decay_fwd), _pair_lanes(decay_bwd),
                     row(g_ret), batch, seq)
    k_m, v_m = _mem_kv(mem, row(g_mem), bf(w_mem_kv))
    return _merge(h2, row(g_mix), os_, lses, y_r, q_m, k_m, v_m, bf(w_gate), row(b_gate),
                  bf(w_proj_attn), bf(w_proj_ret), bf(w_proj_mem), bf(w_out), seq)


def kernel(x, mem, g_mix, w_in, w_mem_kv, g_mem, ret_decay_fwd, ret_decay_bwd, g_ret,
           w_proj_attn, w_proj_ret, w_proj_mem, w_gate, b_gate, w_out,
           g_ffn, w_up, conv_w, conv_b, w_down, g_final):
    batch, seq, d = x.shape
    depth = w_in.shape[0]
    assert d == D_MODEL and depth == 1 and seq % ROW_TILE == 0 and seq % WIDE_ROW_TILE == 0
    h2 = x.reshape(batch * seq, d)
    l = 0
    h2 = _layer(h2, mem, batch, seq, g_mix[l], w_in[l], w_mem_kv[l], g_mem[l],
                ret_decay_fwd[l], ret_decay_bwd[l], g_ret[l], w_proj_attn[l], w_proj_ret[l],
                w_proj_mem[l], w_gate[l], b_gate[l], w_out[l])
    out = _ffn(h2, g_ffn[l].reshape(1, -1), w_up[l].astype(BF16), conv_w[l],
               conv_b[l].reshape(1, -1), w_down[l].astype(BF16), g_final.reshape(1, -1), seq)
    return out.reshape(batch, seq, d)
```

```python
import functools
import math

import jax
import jax.numpy as jnp
from jax import lax
from jax.experimental import pallas as pl
from jax.experimental.pallas import tpu as pltpu

D_MODEL = 1024
HEAD_DIM = 64
ATTN_GROUPS = ((128, 1), (512, 4), (2048, 16))
GROUP_WIDTH = 4 * HEAD_DIM
ATTN_WIDTH = 3 * GROUP_WIDTH
BAND_HALF = 64
RET_HEADS = 6
RET_QK_DIM = 64
RET_V_DIM = 128
RET_QK_WIDTH = RET_HEADS * RET_QK_DIM
RET_V_WIDTH = RET_HEADS * RET_V_DIM
MEM_HEADS = 4
MEM_HEAD_DIM = 128
MEM_WIDTH = MEM_HEADS * MEM_HEAD_DIM
D_FF = 2816
ROPE_THETA = 10000.0
EPS = 1e-6
NEG_INF = -1e30
LOG2_E = math.log2(math.e)

LANES = 128
SUBLANES = 8
MXU_WIDTH = 256
VMEM_LIMIT = 56 * 1024 * 1024

ROW_TILE = 512
WIDE_ROW_TILE = 1024
RET_CHUNK = 256
Q_BLOCK = 128
K_SPAN = Q_BLOCK + 2 * BAND_HALF
MAX_ROW_STRIDE = 4
STATE_UNROLL = 4
MIX_UNROLL = 2
ATTN_PAIRS_IN_FLIGHT = 8

BF16 = jnp.bfloat16
F32 = jnp.float32


def _params(n_grid_axes):
    return pltpu.CompilerParams(
        dimension_semantics=("arbitrary",) * n_grid_axes, vmem_limit_bytes=VMEM_LIMIT)


def _resident(shape):
    nd = len(shape)
    return pl.BlockSpec(shape, lambda *_: (0,) * nd, pipeline_mode=pl.Buffered(1))


def _rms(x, g):
    return x * lax.rsqrt(jnp.mean(x * x, axis=-1, keepdims=True) + EPS) * g


def _qk_lane_is_head0(lane):
    return lane < HEAD_DIM


def _sigmoid(x):
    return 0.5 * jnp.tanh(0.5 * x) + 0.5


_DILATIONS = tuple(d for _, d in ATTN_GROUPS)
RET_PAIRS = RET_HEADS // 2
_IN_SEGMENTS = (
    *[(GROUP_WIDTH, True, HEAD_DIM ** -0.5 * LOG2_E, d, 1) for d in _DILATIONS],
    *[(GROUP_WIDTH, True, 1.0, d, 1) for d in _DILATIONS],
    *[(GROUP_WIDTH, False, 1.0, d, 1) for d in _DILATIONS],
    (RET_QK_WIDTH, True, 1.0, 1, RET_PAIRS),
    (RET_QK_WIDTH, True, RET_QK_DIM ** -0.5, 1, RET_PAIRS),
    (RET_V_WIDTH, False, 1.0, 1, RET_PAIRS),
    (RET_V_WIDTH, False, 1.0, 1, RET_PAIRS),
    (MEM_WIDTH, False, 1.0, 1, 1),
)


def _in_proj_kernel(x_ref, g_ref, cos_ref, sin_ref, w_ref, *refs):
    out_refs, stage_ref, stage2_ref = refs[:-2], refs[-2], refs[-1]
    tm = x_ref.shape[0]
    n = _rms(x_ref[...], g_ref[...]).astype(BF16)
    cos = cos_ref[...]
    sin = sin_ref[...]
    lane = lax.broadcasted_iota(jnp.int32, cos.shape, 1)
    low_half = (lane & (HEAD_DIM // 2)) == 0

    def rotate(a):
        partner = jnp.where(low_half, pltpu.roll(a, LANES - HEAD_DIM // 2, 1),
                            pltpu.roll(a, HEAD_DIM // 2, 1))
        return a * cos + partner * sin

    col = 0
    for out_ref, (width, rotary, scale, dil, parts) in zip(out_refs, _IN_SEGMENTS):
        for c in range(0, width, MXU_WIDTH):
            cw = min(MXU_WIDTH, width - c)
            acc = jnp.dot(n, w_ref[:, col + c:col + c + cw], preferred_element_type=F32)
            for j in range(0, cw, LANES):
                a = acc[:, j:j + LANES]
                if rotary:
                    a = rotate(a)
                if scale != 1.0:
                    a = a * scale
                if parts > 1:
                    part, off = divmod(c + j, width // parts)
                    out_ref[part, :, off:off + LANES] = a.astype(BF16)
                elif dil == 1:
                    out_ref[:, c + j:c + j + LANES] = a.astype(BF16)
                else:
                    stage_ref[j // LANES] = a
            if dil > 1:
                assert width == cw == stage_ref.shape[0] * LANES
                for jb in range(width // LANES):
                    if dil > MAX_ROW_STRIDE:
                        assert dil == MAX_ROW_STRIDE * MAX_ROW_STRIDE
                        n1 = tm // MAX_ROW_STRIDE
                        for r1 in range(MAX_ROW_STRIDE):
                            stage2_ref[jb, r1 * n1:(r1 + 1) * n1, :] = (
                                stage_ref[jb, pl.ds(r1, n1, stride=MAX_ROW_STRIDE), :])
                        residue_rows = [
                            (r1 + MAX_ROW_STRIDE * r2,
                             stage2_ref[jb, pl.ds(r1 * n1 + r2, tm // dil, stride=MAX_ROW_STRIDE), :])
                            for r1 in range(MAX_ROW_STRIDE) for r2 in range(MAX_ROW_STRIDE)]
                    else:
                        residue_rows = [(r, stage_ref[jb, pl.ds(r, tm // dil, stride=dil), :])
                                        for r in range(dil)]
                    for r, rows in residue_rows:
                        out_ref[:, r * width + jb * LANES:r * width + (jb + 1) * LANES] = (
                            rows.astype(BF16))
        col += width


def _in_proj(x2, g_mix, cos_t, sin_t, w_in, seq):
    t = x2.shape[0]
    tm = WIDE_ROW_TILE
    tiles_per_seq = seq // tm
    row = lambda i: (i, 0)
    pos = lambda i: (i % tiles_per_seq, 0)
    out_shape, out_specs = [], []
    for (w, _, _, d, parts) in _IN_SEGMENTS:
        if parts > 1:
            out_shape.append(jax.ShapeDtypeStruct((parts, t, w // parts), BF16))
            out_specs.append(pl.BlockSpec((parts, tm, w // parts), lambda i: (0, i, 0)))
        else:
            out_shape.append(jax.ShapeDtypeStruct((t // d, d * w), BF16))
            out_specs.append(pl.BlockSpec((tm // d, d * w), row))
    return pl.pallas_call(
        _in_proj_kernel,
        grid=(t // tm,),
        in_specs=[
            pl.BlockSpec((tm, D_MODEL), row),
            _resident((1, D_MODEL)),
            pl.BlockSpec((tm, LANES), pos),
            pl.BlockSpec((tm, LANES), pos),
            _resident(w_in.shape),
        ],
        out_specs=out_specs,
        out_shape=out_shape,
        scratch_shapes=[pltpu.VMEM((GROUP_WIDTH // LANES, tm, LANES), F32)] * 2,
        compiler_params=_params(1),
        name="in_proj",
    )(x2, g_mix, cos_t, sin_t, w_in)


def _band_attn_kernel(q_ref, k_ref, v_ref, o_ref, lse_ref, *, seq_len, n_res):
    n_blocks = seq_len // Q_BLOCK
    qi = lax.broadcasted_iota(jnp.int32, (Q_BLOCK, K_SPAN), 0)
    kj = lax.broadcasted_iota(jnp.int32, (Q_BLOCK, K_SPAN), 1)
    rel = qi - kj
    lane = lax.broadcasted_iota(jnp.int32, (Q_BLOCK, LANES), 1)
    head0 = lane < HEAD_DIM
    qk_head0 = _qk_lane_is_head0(lane)

    def band_bias(q0, k0):
        return jnp.where(jnp.abs(rel + (q0 - k0)) <= BAND_HALF, 0.0, NEG_INF)

    def block(q0, k0, bias):
        for pair in range(n_res * 2):
            c0 = pair * LANES
            qp = q_ref[0, pl.ds(q0, Q_BLOCK), c0:c0 + LANES]
            kp = k_ref[0, pl.ds(k0, K_SPAN), c0:c0 + LANES]
            vp = v_ref[0, pl.ds(k0, K_SPAN), c0:c0 + LANES]
            v_ones = jnp.concatenate([vp, jnp.ones_like(vp)], axis=-1)
            outs = []
            for h in range(2):
                mask_h = qk_head0 if h == 0 else jnp.logical_not(qk_head0)
                qh = jnp.where(mask_h, qp, jnp.zeros_like(qp))
                s = lax.dot_general(qh, kp, (((1,), (1,)), ((), ())),
                                    preferred_element_type=F32) + bias
                m = jnp.max(s, axis=-1, keepdims=True)
                p = jnp.exp2(s - m).astype(BF16)
                ol = jnp.dot(p, v_ones, preferred_element_type=F32)
                o, l = ol[:, :LANES], ol[:, LANES:]
                outs.append((o / l, m + jnp.log2(l)))
            o_pair = jnp.where(head0, outs[0][0], outs[1][0])
            lse_pair = jnp.where(head0, outs[0][1], outs[1][1])
            o_ref[0, pl.ds(q0, Q_BLOCK), c0:c0 + LANES] = o_pair.astype(BF16)
            lse_ref[0, pl.ds(q0, Q_BLOCK), c0:c0 + LANES] = lse_pair

    last_q0 = seq_len - Q_BLOCK
    last_k0 = seq_len - K_SPAN
    block(0, 0, band_bias(0, 0))
    if n_blocks > 2:
        mid_bias = band_bias(BAND_HALF, 0)

        def interior(i, carry):
            q0 = pl.multiple_of(i * Q_BLOCK, Q_BLOCK)
            k0 = pl.multiple_of(i * Q_BLOCK - BAND_HALF, BAND_HALF)
            block(q0, k0, mid_bias)
            return carry

        unroll = max(1, ATTN_PAIRS_IN_FLIGHT // (2 * n_res))
        lax.fori_loop(1, n_blocks - 1, interior, 0, unroll=unroll)
    block(last_q0, last_k0, band_bias(last_q0, last_k0))


def _band_attn(q, k, v, batch, seq, dilation):
    seq_len = seq // dilation
    n_res = min(dilation, 4)
    width = dilation * GROUP_WIDTH
    view = lambda a: a.reshape(batch, seq_len, width)
    blk = (1, seq_len, n_res * GROUP_WIDTH)
    idx = lambda b, r: (b, 0, r)
    o, lse = pl.pallas_call(
        functools.partial(_band_attn_kernel, seq_len=seq_len, n_res=n_res),
        grid=(batch, dilation // n_res),
        in_specs=[pl.BlockSpec(blk, idx)] * 3,
        out_specs=[pl.BlockSpec(blk, idx)] * 2,
        out_shape=[jax.ShapeDtypeStruct((batch, seq_len, width), BF16),
                   jax.ShapeDtypeStruct((batch, seq_len, width), F32)],
        compiler_params=_params(2),
        name=f"band_attn_d{dilation}",
    )(view(q), view(k), view(v))
    return o.reshape(batch * seq_len, width), lse.reshape(batch * seq_len, width)


def _log_sigmoid(x):
    return jnp.minimum(x, 0.0) - jnp.log1p(jnp.exp(-jnp.abs(x)))


def _retention_kernel(decf_ref, decb_ref, q_ref, k_ref, v_ref, gate_ref, gret_ref, o_ref,
                      st_ref, s_ref, y_ref, zeta_ref, cdec_ref, xi_ref, din_ref, *, seq):
    c = RET_CHUNK
    n_chunks = seq // c
    pair_w = 2 * RET_QK_DIM
    dv = RET_V_DIM
    st_shape = (pair_w, 2 * RET_V_DIM)

    @pl.when(pl.program_id(1) == 0)
    def _():
        lg_f = _log_sigmoid(decf_ref[0])
        lg_b = _log_sigmoid(decb_ref[0])
        pos = lax.broadcasted_iota(jnp.int32, (c, pair_w), 0).astype(F32)
        zeta_ref[0] = jnp.exp(lg_f[0:1, :] * (c - 1.0 - pos))
        zeta_ref[1] = jnp.exp(lg_b[0:1, :] * pos)
        bcast = lambda lg, h, shape: jnp.broadcast_to(
            lg[0:1, h * RET_QK_DIM:h * RET_QK_DIM + 1], shape)
        row_is_h0 = _qk_lane_is_head0(lax.broadcasted_iota(jnp.int32, st_shape, 0))
        for d, lg in enumerate((lg_f, lg_b)):
            cdec_ref[d] = jnp.exp(
                jnp.where(row_is_h0, bcast(lg, 0, st_shape), bcast(lg, 1, st_shape)) * float(c))
        ci = lax.broadcasted_iota(jnp.int32, (c, c), 0)
        mi = lax.broadcasted_iota(jnp.int32, (c, c), 1)
        delta = (ci - mi).astype(F32)
        for h in range(2):
            xi_ref[h] = jnp.concatenate([jnp.exp(bcast(lg_f, h, (c, dv)) * (pos + 1.0)),
                                         jnp.exp(bcast(lg_b, h, (c, dv)) * (c - pos))], axis=-1)
            din_ref[h] = jnp.where(
                ci >= mi, jnp.exp(bcast(lg_f, h, (c, c)) * jnp.maximum(delta, 0.0)),
                jnp.exp(bcast(lg_b, h, (c, c)) * jnp.maximum(-delta, 0.0)))

    def row0(n):
        return n * c if isinstance(n, int) else pl.multiple_of(n * c, c)

    def chunk_kv(n, direction):
        r0 = row0(n)
        kz = (k_ref[0, pl.ds(r0, c), :].astype(F32) * zeta_ref[direction]).astype(BF16)
        return lax.dot_general(kz, v_ref[0, pl.ds(r0, c), :], (((0,), (0,)), ((), ())),
                               preferred_element_type=F32)

    lane = lax.broadcasted_iota(jnp.int32, (c, pair_w), 1)
    head0 = _qk_lane_is_head0(lane)

    def head_query(qp, h):
        mask_h = head0 if h == 0 else jnp.logical_not(head0)
        return jnp.where(mask_h, qp, jnp.zeros_like(qp))

    def score_chunk(n):
        r0 = row0(n)
        qp = q_ref[0, pl.ds(r0, c), :]
        kp = k_ref[0, pl.ds(r0, c), :]
        for h in range(2):
            s = lax.dot_general(head_query(qp, h), kp, (((1,), (1,)), ((), ())),
                                preferred_element_type=F32)
            s_ref[pl.ds(r0, c), h * c:(h + 1) * c] = (s * din_ref[h]).astype(BF16)

    def state_step(j, carry):
        st_f, st_b = carry
        jb = n_chunks - 1 - j
        score_chunk(j)
        for h in range(2):
            hs = slice(h * dv, (h + 1) * dv)
            st_ref[j, :, (2 * h) * dv:(2 * h + 1) * dv] = st_f[:, hs].astype(BF16)
            st_ref[jb, :, (2 * h + 1) * dv:(2 * h + 2) * dv] = st_b[:, hs].astype(BF16)
        st_f = st_f * cdec_ref[0] + chunk_kv(j, 0)
        st_b = st_b * cdec_ref[1] + chunk_kv(jb, 1)
        return st_f, st_b

    zero_state = jnp.zeros((pair_w, 2 * RET_V_DIM), F32)
    lax.fori_loop(0, n_chunks, state_step, (zero_state, zero_state), unroll=STATE_UNROLL)

    def mix_chunk(n):
        r0 = row0(n)
        qp = q_ref[0, pl.ds(r0, c), :]
        for h in range(2):
            vs = slice(h * dv, (h + 1) * dv)
            y = jnp.dot(s_ref[pl.ds(r0, c), h * c:(h + 1) * c], v_ref[0, pl.ds(r0, c), vs],
                        preferred_element_type=F32)
            cross = jnp.dot(head_query(qp, h), st_ref[n, :, 2 * h * dv:(2 * h + 2) * dv],
                            preferred_element_type=F32) * xi_ref[h]
            y_ref[pl.ds(r0, c), vs] = y + cross[:, :dv] + cross[:, dv:]

    def norm_chunk(n):
        r0 = row0(n)
        for h in range(2):
            vs = slice(h * dv, (h + 1) * dv)
            y = y_ref[pl.ds(r0, c), vs]
            mu = jnp.mean(y, axis=-1, keepdims=True)
            yc = y - mu
            var = jnp.mean(yc * yc, axis=-1, keepdims=True)
            yn = yc * lax.rsqrt(var + EPS) * gret_ref[:, vs]
            gate = gate_ref[0, pl.ds(r0, c), vs].astype(F32)
            o_ref[0, pl.ds(r0, c), vs] = (yn * (gate * _sigmoid(gate))).astype(BF16)

    n_batches = n_chunks // MIX_UNROLL

    def mix_norm(i, carry):
        for u in range(MIX_UNROLL):
            norm_chunk((i - 1) * MIX_UNROLL + u)
        for u in range(MIX_UNROLL):
            mix_chunk(i * MIX_UNROLL + u)
        return carry

    for u in range(MIX_UNROLL):
        mix_chunk(u)
    lax.fori_loop(1, n_batches, mix_norm, 0)
    for u in range(MIX_UNROLL):
        norm_chunk((n_batches - 1) * MIX_UNROLL + u)


def _retention(qr, kr, vr, gr, dec_f, dec_b, g_ret, batch, seq):
    pairs = RET_PAIRS
    n_chunks = seq // RET_CHUNK
    v3 = lambda a: a.reshape(pairs * batch, seq, a.shape[-1])
    slab = lambda p, b: (p * batch + b, 0, 0)
    qk_spec = pl.BlockSpec((1, seq, 2 * RET_QK_DIM), slab)
    v_spec = pl.BlockSpec((1, seq, 2 * RET_V_DIM), slab)
    dec_spec = pl.BlockSpec((1, SUBLANES, LANES), lambda p, b: (p, 0, 0))
    out = pl.pallas_call(
        functools.partial(_retention_kernel, seq=seq),
        grid=(pairs, batch),
        in_specs=[dec_spec, dec_spec, qk_spec, qk_spec, v_spec, v_spec,
                  pl.BlockSpec((1, 2 * RET_V_DIM), lambda p, b: (0, p))],
        out_specs=v_spec,
        out_shape=jax.ShapeDtypeStruct((pairs * batch, seq, 2 * RET_V_DIM), BF16),
        scratch_shapes=[pltpu.VMEM((n_chunks, 2 * RET_QK_DIM, 4 * RET_V_DIM), BF16),
                        pltpu.VMEM((seq, 2 * RET_CHUNK), BF16),
                        pltpu.VMEM((seq, 2 * RET_V_DIM), F32),
                        pltpu.VMEM((2, RET_CHUNK, 2 * RET_QK_DIM), F32),
                        pltpu.VMEM((2, 2 * RET_QK_DIM, 2 * RET_V_DIM), F32),
                        pltpu.VMEM((2, RET_CHUNK, 2 * RET_V_DIM), F32),
                        pltpu.VMEM((2, RET_CHUNK, RET_CHUNK), F32)],
        compiler_params=_params(2),
        name="retention",
    )(dec_f, dec_b, v3(qr), v3(kr), v3(vr), v3(gr), g_ret)
    return out.reshape(pairs, batch * seq, 2 * RET_V_DIM)


def _mem_kv_kernel(mem_ref, g_ref, w_ref, k_ref, v_ref):
    n = _rms(mem_ref[0], g_ref[...]).astype(BF16)
    kv = jnp.dot(n, w_ref[...], preferred_element_type=F32)
    k_ref[0] = kv[:, :MEM_WIDTH].astype(BF16)
    v_ref[0] = kv[:, MEM_WIDTH:].astype(BF16)


def _mem_kv(mem, g_mem, w_mem_kv):
    batch, mem_len, _ = mem.shape
    out = jax.ShapeDtypeStruct((batch, mem_len, MEM_WIDTH), BF16)
    spec = pl.BlockSpec((1, mem_len, MEM_WIDTH), lambda b: (b, 0, 0))
    return pl.pallas_call(
        _mem_kv_kernel,
        grid=(batch,),
        in_specs=[pl.BlockSpec((1, mem_len, D_MODEL), lambda b: (b, 0, 0)),
                  _resident((1, D_MODEL)), _resident(w_mem_kv.shape)],
        out_specs=[spec, spec],
        out_shape=[out, out],
        compiler_params=_params(1),
        name="mem_kv",
    )(mem, g_mem, w_mem_kv)


def _merge_kernel(x_ref, g_ref, o0_ref, o1_ref, o2_ref, l0_ref, l1_ref, l2_ref, yr_ref, qm_ref,
                  km_ref, vm_ref, wg_ref, bg_ref, wpa_ref, wpr_ref, wpm_ref, wo_ref, h_ref,
                  *stage_refs):
    x = x_ref[...]
    tm = x.shape[0]
    n = _rms(x, g_ref[...]).astype(BF16)
    def gate(i):
        return _sigmoid(jnp.dot(n, wg_ref[:, i * D_MODEL:(i + 1) * D_MODEL],
                                preferred_element_type=F32)
                        + bg_ref[:, i * D_MODEL:(i + 1) * D_MODEL])

    def token_rows(ref, dil, stages):
        if dil == 1:
            return ref[...].astype(F32)
        stage_ref = next(stages)
        n_lane_blocks = GROUP_WIDTH // LANES
        residue = lambda r, jb: ref[:, r * GROUP_WIDTH + jb * LANES:
                                    r * GROUP_WIDTH + (jb + 1) * LANES].astype(F32)
        if dil > MAX_ROW_STRIDE:
            assert dil == MAX_ROW_STRIDE * MAX_ROW_STRIDE
            stage2_ref = next(stages)
            n1 = tm // MAX_ROW_STRIDE
            for jb in range(n_lane_blocks):
                for r1 in range(MAX_ROW_STRIDE):
                    for r2 in range(MAX_ROW_STRIDE):
                        stage2_ref[jb, pl.ds(r1 * n1 + r2, tm // dil, stride=MAX_ROW_STRIDE), :] = (
                            residue(r1 + MAX_ROW_STRIDE * r2, jb))
                    stage_ref[jb, pl.ds(r1, n1, stride=MAX_ROW_STRIDE), :] = (
                        stage2_ref[jb, r1 * n1:(r1 + 1) * n1, :])
        else:
            for r in range(dil):
                for jb in range(n_lane_blocks):
                    stage_ref[jb, pl.ds(r, tm // dil, stride=dil), :] = residue(r, jb)
        return jnp.concatenate([stage_ref[jb] for jb in range(n_lane_blocks)], axis=-1)

    y_r = jnp.concatenate([yr_ref[p] for p in range(RET_PAIRS)], axis=-1)
    pr = jnp.dot(y_r, wpr_ref[...], preferred_element_type=F32)

    ym = []
    for hd in range(MEM_HEADS):
        cs = slice(hd * MEM_HEAD_DIM, (hd + 1) * MEM_HEAD_DIM)
        s = lax.dot_general(qm_ref[:, cs], km_ref[0, :, cs], (((1,), (1,)), ((), ())),
                            preferred_element_type=F32) * (MEM_HEAD_DIM ** -0.5 * LOG2_E)
        p = jnp.exp2(s - jnp.max(s, axis=-1, keepdims=True)).astype(BF16)
        vm = vm_ref[0, :, cs]
        ol = jnp.dot(p, jnp.concatenate([vm, jnp.ones_like(vm)], axis=-1),
                     preferred_element_type=F32)
        ym.append((ol[:, :MEM_HEAD_DIM] / ol[:, MEM_HEAD_DIM:]).astype(BF16))
    pm = jnp.dot(jnp.concatenate(ym, axis=-1), wpm_ref[...], preferred_element_type=F32)

    stages = iter(stage_refs)
    outs, lses = [], []
    for o_ref, l_ref, dil in zip((o0_ref, o1_ref, o2_ref), (l0_ref, l1_ref, l2_ref), _DILATIONS):
        outs.append(token_rows(o_ref, dil, stages))
        lses.append(token_rows(l_ref, dil, stages))
    top = jnp.maximum(jnp.maximum(lses[0], lses[1]), lses[2])
    es = [jnp.exp2(l - top) for l in lses]
    inv = 1.0 / (es[0] + es[1] + es[2])
    pa = None
    for g in range(len(outs)):
        y_g = (outs[g] * (es[g] * inv)).astype(BF16)
        part = jnp.dot(y_g, wpa_ref[g * GROUP_WIDTH:(g + 1) * GROUP_WIDTH, :],
                       preferred_element_type=F32)
        pa = part if pa is None else pa + part

    merged = (gate(0) * pa + gate(1) * pr + gate(2) * pm).astype(BF16)
    h_ref[...] = x + jnp.dot(merged, wo_ref[...], preferred_element_type=F32)


def _merge(x2, g_mix, os_, lses, y_r, q_m, k_m, v_m, w_gate, b_gate, w_pa, w_pr, w_pm, w_out,
           seq):
    t = x2.shape[0]
    tm = ROW_TILE
    tiles_per_seq = seq // tm
    row = lambda i: (i, 0)
    mem_len = k_m.shape[1]
    mem_spec = pl.BlockSpec((1, mem_len, MEM_WIDTH), lambda i: (i // tiles_per_seq, 0, 0))
    group_specs = [pl.BlockSpec((tm // d, d * GROUP_WIDTH), row) for d in _DILATIONS]
    n_stages = 2 * sum((d > 1) + (d > MAX_ROW_STRIDE) for d in _DILATIONS)
    return pl.pallas_call(
        _merge_kernel,
        grid=(t // tm,),
        in_specs=[
            pl.BlockSpec((tm, D_MODEL), row), _resident((1, D_MODEL)),
            *group_specs, *group_specs,
            pl.BlockSpec((RET_PAIRS, tm, 2 * RET_V_DIM), lambda i: (0, i, 0)),
            pl.BlockSpec((tm, MEM_WIDTH), row),
            mem_spec, mem_spec,
            _resident(w_gate.shape), _resident(b_gate.shape), _resident(w_pa.shape),
            _resident(w_pr.shape), _resident(w_pm.shape), _resident(w_out.shape),
        ],
        out_specs=pl.BlockSpec((tm, D_MODEL), row),
        out_shape=jax.ShapeDtypeStruct((t, D_MODEL), F32),
        scratch_shapes=[pltpu.VMEM((GROUP_WIDTH // LANES, tm, LANES), F32)] * n_stages,
        compiler_params=_params(1),
        name="merge",
    )(x2, g_mix, *os_, *lses, y_r, q_m, k_m, v_m, w_gate, b_gate, w_pa, w_pr, w_pm, w_out)


FF_CHUNK = MXU_WIDTH
FF_GROUP_CHUNKS = (4, 4, 3)
HALO = SUBLANES


N_FF_CHUNKS = D_FF // FF_CHUNK


def _ffn_kernel(h_ref, prev_ref, next_ref, gffn_ref, wup_ref, cw_ref, cb_ref, wdown_ref,
                gfin_ref, out_ref, *scratch, tiles_per_seq):
    u_refs, act_refs, y_ref = scratch[:N_FF_CHUNKS], scratch[N_FF_CHUNKS:-1], scratch[-1]
    i = pl.program_id(0)
    tm = h_ref.shape[0]
    g = gffn_ref[...]
    has_prev = (i % tiles_per_seq) != 0
    has_next = (i % tiles_per_seq) != tiles_per_seq - 1
    n_prev = jnp.where(has_prev, _rms(prev_ref[...], g), 0.0)
    n_next = jnp.where(has_next, _rms(next_ref[...], g), 0.0)
    n_ext = jnp.concatenate([n_prev, _rms(h_ref[...], g), n_next], axis=0).astype(BF16)

    half_rows = tm // 2
    lane_blocks = FF_CHUNK // LANES

    def conv(u_ref, ab, c0, parity):
        parts = []
        for jb in range(lane_blocks):
            sl = slice(c0 + jb * LANES, c0 + (jb + 1) * LANES)
            taps = [u_ref[ab, jb, pl.ds(HALO - 1 + parity + k, half_rows, stride=2), :]
                    * cw_ref[k:k + 1, sl] for k in range(3)]
            parts.append(taps[0] + taps[1] + taps[2] + cb_ref[:, sl])
        return jnp.concatenate(parts, axis=-1)

    def up(chunk):
        c0 = chunk * FF_CHUNK
        for ab, col in enumerate((c0, D_FF + c0)):
            u = jnp.dot(n_ext, wup_ref[:, col:col + FF_CHUNK], preferred_element_type=F32)
            for jb in range(lane_blocks):
                u_refs[chunk][ab, jb] = u[:, jb * LANES:(jb + 1) * LANES]

    def gate(chunk, act_ref, col):
        c0 = chunk * FF_CHUNK
        for parity in range(2):
            a = conv(u_refs[chunk], 0, c0, parity)
            b = conv(u_refs[chunk], 1, D_FF + c0, parity)
            act_ref[parity * half_rows:(parity + 1) * half_rows, col:col + FF_CHUNK] = (
                (a * _sigmoid(a) * b).astype(BF16))

    def down(gi):
        grp = groups[gi]
        return jnp.dot(act_refs[gi][...], wdown_ref[grp[0] * FF_CHUNK:(grp[-1] + 1) * FF_CHUNK, :],
                       preferred_element_type=F32)

    assert sum(FF_GROUP_CHUNKS) == N_FF_CHUNKS
    first = [sum(FF_GROUP_CHUNKS[:g]) for g in range(len(FF_GROUP_CHUNKS))]
    groups = [range(f, f + n) for f, n in zip(first, FF_GROUP_CHUNKS)]
    y = None
    for step in range(len(groups) + 2):
        if 1 <= step <= len(groups):
            for j, chunk in enumerate(groups[step - 1]):
                gate(chunk, act_refs[step - 1], j * FF_CHUNK)
        if step < len(groups):
            for chunk in groups[step]:
                up(chunk)
        if 2 <= step:
            part = down(step - 2)
            y = part if y is None else y + part
    for parity in range(2):
        for jb in range(D_MODEL // LANES):
            y_ref[jb, pl.ds(parity, half_rows, stride=2), :] = (
                y[parity * half_rows:(parity + 1) * half_rows, jb * LANES:(jb + 1) * LANES])
    y_tok = jnp.concatenate([y_ref[jb] for jb in range(D_MODEL // LANES)], axis=-1)
    out_ref[...] = _rms(h_ref[...] + y_tok, gfin_ref[...])


def _ffn(h, g_ffn, w_up, conv_w, conv_b, w_down, g_final, seq):
    t = h.shape[0]
    tm = ROW_TILE
    tiles_per_seq = seq // tm
    halo_blocks = tm // HALO
    last_block = t // HALO - 1
    row = lambda i: (i, 0)
    return pl.pallas_call(
        functools.partial(_ffn_kernel, tiles_per_seq=tiles_per_seq),
        grid=(t // tm,),
        in_specs=[
            pl.BlockSpec((tm, D_MODEL), row),
            pl.BlockSpec((HALO, D_MODEL), lambda i: (jnp.maximum(i * halo_blocks - 1, 0), 0)),
            pl.BlockSpec((HALO, D_MODEL),
                         lambda i: (jnp.minimum((i + 1) * halo_blocks, last_block), 0)),
            _resident((1, D_MODEL)), _resident(w_up.shape), _resident(conv_w.shape),
            _resident(conv_b.shape), _resident(w_down.shape), _resident((1, D_MODEL)),
        ],
        out_specs=pl.BlockSpec((tm, D_MODEL), row),
        out_shape=jax.ShapeDtypeStruct((t, D_MODEL), F32),
        scratch_shapes=[*[pltpu.VMEM((2, FF_CHUNK // LANES, tm + 2 * HALO, LANES), F32)]
                        * N_FF_CHUNKS,
                        *[pltpu.VMEM((tm, n * FF_CHUNK), BF16) for n in FF_GROUP_CHUNKS],
                        pltpu.VMEM((D_MODEL // LANES, tm, LANES), F32)],
        compiler_params=_params(1),
        name="ffn",
    )(h, h, h, g_ffn, w_up, conv_w, conv_b, w_down, g_final)


def _rotary_tables(seq):
    inv = ROPE_THETA ** (-jnp.arange(0, HEAD_DIM, 2, dtype=F32) / HEAD_DIM)
    ang = jnp.arange(seq, dtype=F32)[:, None] * inv[None, :]
    cos, sin = jnp.cos(ang), jnp.sin(ang)
    reps = LANES // HEAD_DIM
    cos_t = jnp.tile(jnp.concatenate([cos, cos], axis=-1), (1, reps))
    sin_t = jnp.tile(jnp.concatenate([-sin, sin], axis=-1), (1, reps))
    return cos_t, sin_t


def _pair_lanes(v):
    pairs = v.reshape(RET_PAIRS, 2, 1)
    lanes = jnp.broadcast_to(pairs, (RET_PAIRS, 2, RET_QK_DIM)).reshape(RET_PAIRS, 1, LANES)
    return jnp.broadcast_to(lanes, (RET_PAIRS, SUBLANES, LANES)).astype(F32)


def _layer(h2, mem, batch, seq, g_mix, w_in, w_mem_kv, g_mem, decay_fwd, decay_bwd, g_ret,
           w_proj_attn, w_proj_ret, w_proj_mem, w_gate, b_gate, w_out):
    bf = lambda w: w.astype(BF16)
    row = lambda v: v.reshape(1, -1).astype(F32)
    cos_t, sin_t = _rotary_tables(seq)
    proj = _in_proj(h2, row(g_mix), cos_t, sin_t, bf(w_in), seq)
    qa, ka, va = proj[0:3], proj[3:6], proj[6:9]
    q_r, k_r, v_r, g_r, q_m = proj[9:]

    os_, lses = [], []
    for g, (_, dilation) in enumerate(ATTN_GROUPS):
        o, lse = _band_attn(qa[g], ka[g], va[g], batch, seq, dilation)
        os_.append(o)
        lses.append(lse)

    y_r = _retention(q_r, k_r, v_r, g_r, _pair_lanes(decay_fwd), _pair_lanes(decay_bwd),
                     row(g_ret), batch, seq)
    k_m, v_m = _mem_kv(mem, row(g_mem), bf(w_mem_kv))
    return _merge(h2, row(g_mix), os_, lses, y_r, q_m, k_m, v_m, bf(w_gate), row(b_gate),
                  bf(w_proj_attn), bf(w_proj_ret), bf(w_proj_mem), bf(w_out), seq)


def kernel(x, mem, g_mix, w_in, w_mem_kv, g_mem, ret_decay_fwd, ret_decay_bwd, g_ret,
           w_proj_attn, w_proj_ret, w_proj_mem, w_gate, b_gate, w_out,
           g_ffn, w_up, conv_w, conv_b, w_down, g_final):
    batch, seq, d = x.shape
    depth = w_in.shape[0]
    assert d == D_MODEL and depth == 1 and seq % ROW_TILE == 0 and seq % WIDE_ROW_TILE == 0
    h2 = x.reshape(batch * seq, d)
    l = 0
    h2 = _layer(h2, mem, batch, seq, g_mix[l], w_in[l], w_mem_kv[l], g_mem[l],
                ret_decay_fwd[l], ret_decay_bwd[l], g_ret[l], w_proj_attn[l], w_proj_ret[l],
                w_proj_mem[l], w_gate[l], b_gate[l], w_out[l])
    out = _ffn(h2, g_ffn[l].reshape(1, -1), w_up[l].astype(BF16), conv_w[l],
               conv_b[l].reshape(1, -1), w_down[l].astype(BF16), g_final.reshape(1, -1), seq)
    return out.reshape(batch, seq, d)
```

```python
import functools
import math

import jax
import jax.numpy as jnp
from jax import lax
from jax.experimental import pallas as pl
from jax.experimental.pallas import tpu as pltpu

D_MODEL = 1024
HEAD_DIM = 64
ATTN_GROUPS = ((128, 1), (512, 4), (2048, 16))
GROUP_WIDTH = 4 * HEAD_DIM
ATTN_WIDTH = 3 * GROUP_WIDTH
BAND_HALF = 64
RET_HEADS = 6
RET_QK_DIM = 64
RET_V_DIM = 128
RET_QK_WIDTH = RET_HEADS * RET_QK_DIM
RET_V_WIDTH = RET_HEADS * RET_V_DIM
MEM_HEADS = 4
MEM_HEAD_DIM = 128
MEM_WIDTH = MEM_HEADS * MEM_HEAD_DIM
D_FF = 2816
ROPE_THETA = 10000.0
EPS = 1e-6
NEG_INF = -1e30
LOG2_E = math.log2(math.e)

LANES = 128
SUBLANES = 8
MXU_WIDTH = 256
VMEM_LIMIT = 56 * 1024 * 1024

ROW_TILE = 512
WIDE_ROW_TILE = 1024
RET_CHUNK = 256
Q_BLOCK = 128
K_SPAN = Q_BLOCK + 2 * BAND_HALF
MAX_ROW_STRIDE = 4
STATE_UNROLL = 4
MIX_UNROLL = 4
ATTN_PAIRS_IN_FLIGHT = 8

BF16 = jnp.bfloat16
F32 = jnp.float32


def _params(n_grid_axes):
    return pltpu.CompilerParams(
        dimension_semantics=("arbitrary",) * n_grid_axes, vmem_limit_bytes=VMEM_LIMIT)


def _resident(shape):
    nd = len(shape)
    return pl.BlockSpec(shape, lambda *_: (0,) * nd, pipeline_mode=pl.Buffered(1))


def _rms(x, g):
    return x * lax.rsqrt(jnp.mean(x * x, axis=-1, keepdims=True) + EPS) * g


def _qk_lane_is_head0(lane):
    return lane < HEAD_DIM


def _sigmoid(x):
    return 0.5 * jnp.tanh(0.5 * x) + 0.5


_DILATIONS = tuple(d for _, d in ATTN_GROUPS)
RET_PAIRS = RET_HEADS // 2
_IN_SEGMENTS = (
    *[(GROUP_WIDTH, True, HEAD_DIM ** -0.5 * LOG2_E, d, 1) for d in _DILATIONS],
    *[(GROUP_WIDTH, True, 1.0, d, 1) for d in _DILATIONS],
    *[(GROUP_WIDTH, False, 1.0, d, 1) for d in _DILATIONS],
    (RET_QK_WIDTH, True, 1.0, 1, RET_PAIRS),
    (RET_QK_WIDTH, True, RET_QK_DIM ** -0.5, 1, RET_PAIRS),
    (RET_V_WIDTH, False, 1.0, 1, RET_PAIRS),
    (RET_V_WIDTH, False, 1.0, 1, RET_PAIRS),
    (MEM_WIDTH, False, 1.0, 1, 1),
)


def _in_proj_kernel(x_ref, g_ref, cos_ref, sin_ref, w_ref, *refs):
    out_refs, stage_ref, stage2_ref = refs[:-2], refs[-2], refs[-1]
    tm = x_ref.shape[0]
    n = _rms(x_ref[...], g_ref[...]).astype(BF16)
    cos = cos_ref[...]
    sin = sin_ref[...]
    lane = lax.broadcasted_iota(jnp.int32, cos.shape, 1)
    low_half = (lane & (HEAD_DIM // 2)) == 0

    def rotate(a):
        partner = jnp.where(low_half, pltpu.roll(a, LANES - HEAD_DIM // 2, 1),
                            pltpu.roll(a, HEAD_DIM // 2, 1))
        return a * cos + partner * sin

    col = 0
    for out_ref, (width, rotary, scale, dil, parts) in zip(out_refs, _IN_SEGMENTS):
        for c in range(0, width, MXU_WIDTH):
            cw = min(MXU_WIDTH, width - c)
            acc = jnp.dot(n, w_ref[:, col + c:col + c + cw], preferred_element_type=F32)
            for j in range(0, cw, LANES):
                a = acc[:, j:j + LANES]
                if rotary:
                    a = rotate(a)
                if scale != 1.0:
                    a = a * scale
                if parts > 1:
                    part, off = divmod(c + j, width // parts)
                    out_ref[part, :, off:off + LANES] = a.astype(BF16)
                elif dil == 1:
                    out_ref[:, c + j:c + j + LANES] = a.astype(BF16)
                else:
                    stage_ref[j // LANES] = a
            if dil > 1:
                assert width == cw == stage_ref.shape[0] * LANES
                for jb in range(width // LANES):
                    if dil > MAX_ROW_STRIDE:
                        assert dil == MAX_ROW_STRIDE * MAX_ROW_STRIDE
                        n1 = tm // MAX_ROW_STRIDE
                        for r1 in range(MAX_ROW_STRIDE):
                            stage2_ref[jb, r1 * n1:(r1 + 1) * n1, :] = (
                                stage_ref[jb, pl.ds(r1, n1, stride=MAX_ROW_STRIDE), :])
                        residue_rows = [
                            (r1 + MAX_ROW_STRIDE * r2,
                             stage2_ref[jb, pl.ds(r1 * n1 + r2, tm // dil, stride=MAX_ROW_STRIDE), :])
                            for r1 in range(MAX_ROW_STRIDE) for r2 in range(MAX_ROW_STRIDE)]
                    else:
                        residue_rows = [(r, stage_ref[jb, pl.ds(r, tm // dil, stride=dil), :])
                                        for r in range(dil)]
                    for r, rows in residue_rows:
                        out_ref[:, r * width + jb * LANES:r * width + (jb + 1) * LANES] = (
                            rows.astype(BF16))
        col += width


def _in_proj(x2, g_mix, cos_t, sin_t, w_in, seq):
    t = x2.shape[0]
    tm = WIDE_ROW_TILE
    tiles_per_seq = seq // tm
    row = lambda i: (i, 0)
    pos = lambda i: (i % tiles_per_seq, 0)
    out_shape, out_specs = [], []
    for (w, _, _, d, parts) in _IN_SEGMENTS:
        if parts > 1:
            out_shape.append(jax.ShapeDtypeStruct((parts, t, w // parts), BF16))
            out_specs.append(pl.BlockSpec((parts, tm, w // parts), lambda i: (0, i, 0)))
        else:
            out_shape.append(jax.ShapeDtypeStruct((t // d, d * w), BF16))
            out_specs.append(pl.BlockSpec((tm // d, d * w), row))
    return pl.pallas_call(
        _in_proj_kernel,
        grid=(t // tm,),
        in_specs=[
            pl.BlockSpec((tm, D_MODEL), row),
            _resident((1, D_MODEL)),
            pl.BlockSpec((tm, LANES), pos),
            pl.BlockSpec((tm, LANES), pos),
            _resident(w_in.shape),
        ],
        out_specs=out_specs,
        out_shape=out_shape,
        scratch_shapes=[pltpu.VMEM((GROUP_WIDTH // LANES, tm, LANES), F32)] * 2,
        compiler_params=_params(1),
        name="in_proj",
    )(x2, g_mix, cos_t, sin_t, w_in)


def _band_attn_kernel(q_ref, k_ref, v_ref, o_ref, lse_ref, *, seq_len, n_res):
    n_blocks = seq_len // Q_BLOCK
    qi = lax.broadcasted_iota(jnp.int32, (Q_BLOCK, K_SPAN), 0)
    kj = lax.broadcasted_iota(jnp.int32, (Q_BLOCK, K_SPAN), 1)
    rel = qi - kj
    lane = lax.broadcasted_iota(jnp.int32, (Q_BLOCK, LANES), 1)
    head0 = lane < HEAD_DIM
    qk_head0 = _qk_lane_is_head0(lane)

    def band_bias(q0, k0):
        return jnp.where(jnp.abs(rel + (q0 - k0)) <= BAND_HALF, 0.0, NEG_INF)

    def block(q0, k0, bias):
        for pair in range(n_res * 2):
            c0 = pair * LANES
            qp = q_ref[0, pl.ds(q0, Q_BLOCK), c0:c0 + LANES]
            kp = k_ref[0, pl.ds(k0, K_SPAN), c0:c0 + LANES]
            vp = v_ref[0, pl.ds(k0, K_SPAN), c0:c0 + LANES]
            v_ones = jnp.concatenate([vp, jnp.ones_like(vp)], axis=-1)
            outs = []
            for h in range(2):
                mask_h = qk_head0 if h == 0 else jnp.logical_not(qk_head0)
                qh = jnp.where(mask_h, qp, jnp.zeros_like(qp))
                s = lax.dot_general(qh, kp, (((1,), (1,)), ((), ())),
                                    preferred_element_type=F32) + bias
                m = jnp.max(s, axis=-1, keepdims=True)
                p = jnp.exp2(s - m).astype(BF16)
                ol = jnp.dot(p, v_ones, preferred_element_type=F32)
                o, l = ol[:, :LANES], ol[:, LANES:]
                outs.append((o / l, m + jnp.log2(l)))
            o_pair = jnp.where(head0, outs[0][0], outs[1][0])
            lse_pair = jnp.where(head0, outs[0][1], outs[1][1])
            o_ref[0, pl.ds(q0, Q_BLOCK), c0:c0 + LANES] = o_pair.astype(BF16)
            lse_ref[0, pl.ds(q0, Q_BLOCK), c0:c0 + LANES] = lse_pair

    last_q0 = seq_len - Q_BLOCK
    last_k0 = seq_len - K_SPAN
    block(0, 0, band_bias(0, 0))
    if n_blocks > 2:
        mid_bias = band_bias(BAND_HALF, 0)

        def interior(i, carry):
            q0 = pl.multiple_of(i * Q_BLOCK, Q_BLOCK)
            k0 = pl.multiple_of(i * Q_BLOCK - BAND_HALF, BAND_HALF)
            block(q0, k0, mid_bias)
            return carry

        unroll = max(1, ATTN_PAIRS_IN_FLIGHT // (2 * n_res))
        lax.fori_loop(1, n_blocks - 1, interior, 0, unroll=unroll)
    block(last_q0, last_k0, band_bias(last_q0, last_k0))


def _band_attn(q, k, v, batch, seq, dilation):
    seq_len = seq // dilation
    n_res = min(dilation, 4)
    width = dilation * GROUP_WIDTH
    view = lambda a: a.reshape(batch, seq_len, width)
    blk = (1, seq_len, n_res * GROUP_WIDTH)
    idx = lambda b, r: (b, 0, r)
    o, lse = pl.pallas_call(
        functools.partial(_band_attn_kernel, seq_len=seq_len, n_res=n_res),
        grid=(batch, dilation // n_res),
        in_specs=[pl.BlockSpec(blk, idx)] * 3,
        out_specs=[pl.BlockSpec(blk, idx)] * 2,
        out_shape=[jax.ShapeDtypeStruct((batch, seq_len, width), BF16),
                   jax.ShapeDtypeStruct((batch, seq_len, width), F32)],
        compiler_params=_params(2),
        name=f"band_attn_d{dilation}",
    )(view(q), view(k), view(v))
    return o.reshape(batch * seq_len, width), lse.reshape(batch * seq_len, width)


def _log_sigmoid(x):
    return jnp.minimum(x, 0.0) - jnp.log1p(jnp.exp(-jnp.abs(x)))


def _retention_kernel(decf_ref, decb_ref, q_ref, k_ref, v_ref, gate_ref, gret_ref, o_ref,
                      st_ref, s_ref, y_ref, zeta_ref, cdec_ref, xi_ref, din_ref, *, seq):
    c = RET_CHUNK
    n_chunks = seq // c
    pair_w = 2 * RET_QK_DIM
    dv = RET_V_DIM
    st_shape = (pair_w, 2 * RET_V_DIM)

    @pl.when(pl.program_id(1) == 0)
    def _():
        lg_f = _log_sigmoid(decf_ref[0])
        lg_b = _log_sigmoid(decb_ref[0])
        pos = lax.broadcasted_iota(jnp.int32, (c, pair_w), 0).astype(F32)
        zeta_ref[0] = jnp.exp(lg_f[0:1, :] * (c - 1.0 - pos))
        zeta_ref[1] = jnp.exp(lg_b[0:1, :] * pos)
        bcast = lambda lg, h, shape: jnp.broadcast_to(
            lg[0:1, h * RET_QK_DIM:h * RET_QK_DIM + 1], shape)
        row_is_h0 = _qk_lane_is_head0(lax.broadcasted_iota(jnp.int32, st_shape, 0))
        for d, lg in enumerate((lg_f, lg_b)):
            cdec_ref[d] = jnp.exp(
                jnp.where(row_is_h0, bcast(lg, 0, st_shape), bcast(lg, 1, st_shape)) * float(c))
        ci = lax.broadcasted_iota(jnp.int32, (c, c), 0)
        mi = lax.broadcasted_iota(jnp.int32, (c, c), 1)
        delta = (ci - mi).astype(F32)
        for h in range(2):
            xi_ref[h] = jnp.concatenate([jnp.exp(bcast(lg_f, h, (c, dv)) * (pos + 1.0)),
                                         jnp.exp(bcast(lg_b, h, (c, dv)) * (c - pos))], axis=-1)
            din_ref[h] = jnp.where(
                ci >= mi, jnp.exp(bcast(lg_f, h, (c, c)) * jnp.maximum(delta, 0.0)),
                jnp.exp(bcast(lg_b, h, (c, c)) * jnp.maximum(-delta, 0.0)))

    def row0(n):
        return n * c if isinstance(n, int) else pl.multiple_of(n * c, c)

    def chunk_kv(n, direction):
        r0 = row0(n)
        kz = (k_ref[0, pl.ds(r0, c), :].astype(F32) * zeta_ref[direction]).astype(BF16)
        return lax.dot_general(kz, v_ref[0, pl.ds(r0, c), :], (((0,), (0,)), ((), ())),
                               preferred_element_type=F32)

    lane = lax.broadcasted_iota(jnp.int32, (c, pair_w), 1)
    head0 = _qk_lane_is_head0(lane)

    def head_query(qp, h):
        mask_h = head0 if h == 0 else jnp.logical_not(head0)
        return jnp.where(mask_h, qp, jnp.zeros_like(qp))

    def score_chunk(n):
        r0 = row0(n)
        qp = q_ref[0, pl.ds(r0, c), :]
        kp = k_ref[0, pl.ds(r0, c), :]
        for h in range(2):
            s = lax.dot_general(head_query(qp, h), kp, (((1,), (1,)), ((), ())),
                                preferred_element_type=F32)
            s_ref[pl.ds(r0, c), h * c:(h + 1) * c] = (s * din_ref[h]).astype(BF16)

    def state_step(j, carry):
        st_f, st_b = carry
        jb = n_chunks - 1 - j
        score_chunk(j)
        for h in range(2):
            hs = slice(h * dv, (h + 1) * dv)
            st_ref[j, :, (2 * h) * dv:(2 * h + 1) * dv] = st_f[:, hs].astype(BF16)
            st_ref[jb, :, (2 * h + 1) * dv:(2 * h + 2) * dv] = st_b[:, hs].astype(BF16)
        st_f = st_f * cdec_ref[0] + chunk_kv(j, 0)
        st_b = st_b * cdec_ref[1] + chunk_kv(jb, 1)
        return st_f, st_b

    zero_state = jnp.zeros((pair_w, 2 * RET_V_DIM), F32)
    lax.fori_loop(0, n_chunks, state_step, (zero_state, zero_state), unroll=STATE_UNROLL)

    def mix_chunk(n):
        r0 = row0(n)
        qp = q_ref[0, pl.ds(r0, c), :]
        for h in range(2):
            vs = slice(h * dv, (h + 1) * dv)
            y = jnp.dot(s_ref[pl.ds(r0, c), h * c:(h + 1) * c], v_ref[0, pl.ds(r0, c), vs],
                        preferred_element_type=F32)
            cross = jnp.dot(head_query(qp, h), st_ref[n, :, 2 * h * dv:(2 * h + 2) * dv],
                            preferred_element_type=F32) * xi_ref[h]
            y_ref[pl.ds(r0, c), vs] = y + cross[:, :dv] + cross[:, dv:]

    def norm_chunk(n):
        r0 = row0(n)
        for h in range(2):
            vs = slice(h * dv, (h + 1) * dv)
            y = y_ref[pl.ds(r0, c), vs]
            mu = jnp.mean(y, axis=-1, keepdims=True)
            yc = y - mu
            var = jnp.mean(yc * yc, axis=-1, keepdims=True)
            yn = yc * lax.rsqrt(var + EPS) * gret_ref[:, vs]
            gate = gate_ref[0, pl.ds(r0, c), vs].astype(F32)
            o_ref[0, pl.ds(r0, c), vs] = (yn * (gate * _sigmoid(gate))).astype(BF16)

    n_batches = n_chunks // MIX_UNROLL

    def mix_norm(i, carry):
        for u in range(MIX_UNROLL):
            norm_chunk((i - 1) * MIX_UNROLL + u)
        for u in range(MIX_UNROLL):
            mix_chunk(i * MIX_UNROLL + u)
        return carry

    for u in range(MIX_UNROLL):
        mix_chunk(u)
    lax.fori_loop(1, n_batches, mix_norm, 0)
    for u in range(MIX_UNROLL):
        norm_chunk((n_batches - 1) * MIX_UNROLL + u)


def _retention(qr, kr, vr, gr, dec_f, dec_b, g_ret, batch, seq):
    pairs = RET_PAIRS
    n_chunks = seq // RET_CHUNK
    v3 = lambda a: a.reshape(pairs * batch, seq, a.shape[-1])
    slab = lambda p, b: (p * batch + b, 0, 0)
    qk_spec = pl.BlockSpec((1, seq, 2 * RET_QK_DIM), slab)
    v_spec = pl.BlockSpec((1, seq, 2 * RET_V_DIM), slab)
    dec_spec = pl.BlockSpec((1, SUBLANES, LANES), lambda p, b: (p, 0, 0))
    out = pl.pallas_call(
        functools.partial(_retention_kernel, seq=seq),
        grid=(pairs, batch),
        in_specs=[dec_spec, dec_spec, qk_spec, qk_spec, v_spec, v_spec,
                  pl.BlockSpec((1, 2 * RET_V_DIM), lambda p, b: (0, p))],
        out_specs=v_spec,
        out_shape=jax.ShapeDtypeStruct((pairs * batch, seq, 2 * RET_V_DIM), BF16),
        scratch_shapes=[pltpu.VMEM((n_chunks, 2 * RET_QK_DIM, 4 * RET_V_DIM), BF16),
                        pltpu.VMEM((seq, 2 * RET_CHUNK), BF16),
                        pltpu.VMEM((seq, 2 * RET_V_DIM), F32),
                        pltpu.VMEM((2, RET_CHUNK, 2 * RET_QK_DIM), F32),
                        pltpu.VMEM((2, 2 * RET_QK_DIM, 2 * RET_V_DIM), F32),
                        pltpu.VMEM((2, RET_CHUNK, 2 * RET_V_DIM), F32),
                        pltpu.VMEM((2, RET_CHUNK, RET_CHUNK), F32)],
        compiler_params=_params(2),
        name="retention",
    )(dec_f, dec_b, v3(qr), v3(kr), v3(vr), v3(gr), g_ret)
    return out.reshape(pairs, batch * seq, 2 * RET_V_DIM)


def _mem_kv_kernel(mem_ref, g_ref, w_ref, k_ref, v_ref):
    n = _rms(mem_ref[0], g_ref[...]).astype(BF16)
    kv = jnp.dot(n, w_ref[...], preferred_element_type=F32)
    k_ref[0] = kv[:, :MEM_WIDTH].astype(BF16)
    v_ref[0] = kv[:, MEM_WIDTH:].astype(BF16)


def _mem_kv(mem, g_mem, w_mem_kv):
    batch, mem_len, _ = mem.shape
    out = jax.ShapeDtypeStruct((batch, mem_len, MEM_WIDTH), BF16)
    spec = pl.BlockSpec((1, mem_len, MEM_WIDTH), lambda b: (b, 0, 0))
    return pl.pallas_call(
        _mem_kv_kernel,
        grid=(batch,),
        in_specs=[pl.BlockSpec((1, mem_len, D_MODEL), lambda b: (b, 0, 0)),
                  _resident((1, D_MODEL)), _resident(w_mem_kv.shape)],
        out_specs=[spec, spec],
        out_shape=[out, out],
        compiler_params=_params(1),
        name="mem_kv",
    )(mem, g_mem, w_mem_kv)


def _merge_kernel(x_ref, g_ref, o0_ref, o1_ref, o2_ref, l0_ref, l1_ref, l2_ref, yr_ref, qm_ref,
                  km_ref, vm_ref, wg_ref, bg_ref, wpa_ref, wpr_ref, wpm_ref, wo_ref, h_ref,
                  *stage_refs):
    x = x_ref[...]
    tm = x.shape[0]
    n = _rms(x, g_ref[...]).astype(BF16)
    def gate(i):
        return _sigmoid(jnp.dot(n, wg_ref[:, i * D_MODEL:(i + 1) * D_MODEL],
                                preferred_element_type=F32)
                        + bg_ref[:, i * D_MODEL:(i + 1) * D_MODEL])

    def token_rows(ref, dil, stages):
        if dil == 1:
            return ref[...].astype(F32)
        stage_ref = next(stages)
        n_lane_blocks = GROUP_WIDTH // LANES
        residue = lambda r, jb: ref[:, r * GROUP_WIDTH + jb * LANES:
                                    r * GROUP_WIDTH + (jb + 1) * LANES].astype(F32)
        if dil > MAX_ROW_STRIDE:
            assert dil == MAX_ROW_STRIDE * MAX_ROW_STRIDE
            stage2_ref = next(stages)
            n1 = tm // MAX_ROW_STRIDE
            for jb in range(n_lane_blocks):
                for r1 in range(MAX_ROW_STRIDE):
                    for r2 in range(MAX_ROW_STRIDE):
                        stage2_ref[jb, pl.ds(r1 * n1 + r2, tm // dil, stride=MAX_ROW_STRIDE), :] = (
                            residue(r1 + MAX_ROW_STRIDE * r2, jb))
                    stage_ref[jb, pl.ds(r1, n1, stride=MAX_ROW_STRIDE), :] = (
                        stage2_ref[jb, r1 * n1:(r1 + 1) * n1, :])
        else:
            for r in range(dil):
                for jb in range(n_lane_blocks):
                    stage_ref[jb, pl.ds(r, tm // dil, stride=dil), :] = residue(r, jb)
        return jnp.concatenate([stage_ref[jb] for jb in range(n_lane_blocks)], axis=-1)

    y_r = jnp.concatenate([yr_ref[p] for p in range(RET_PAIRS)], axis=-1)
    pr = jnp.dot(y_r, wpr_ref[...], preferred_element_type=F32)

    ym = []
    for hd in range(MEM_HEADS):
        cs = slice(hd * MEM_HEAD_DIM, (hd + 1) * MEM_HEAD_DIM)
        s = lax.dot_general(qm_ref[:, cs], km_ref[0, :, cs], (((1,), (1,)), ((), ())),
                            preferred_element_type=F32) * (MEM_HEAD_DIM ** -0.5 * LOG2_E)
        p = jnp.exp2(s - jnp.max(s, axis=-1, keepdims=True)).astype(BF16)
        vm = vm_ref[0, :, cs]
        ol = jnp.dot(p, jnp.concatenate([vm, jnp.ones_like(vm)], axis=-1),
                     preferred_element_type=F32)
        ym.append((ol[:, :MEM_HEAD_DIM] / ol[:, MEM_HEAD_DIM:]).astype(BF16))
    pm = jnp.dot(jnp.concatenate(ym, axis=-1), wpm_ref[...], preferred_element_type=F32)

    stages = iter(stage_refs)
    outs, lses = [], []
    for o_ref, l_ref, dil in zip((o0_ref, o1_ref, o2_ref), (l0_ref, l1_ref, l2_ref), _DILATIONS):
        outs.append(token_rows(o_ref, dil, stages))
        lses.append(token_rows(l_ref, dil, stages))
    top = jnp.maximum(jnp.maximum(lses[0], lses[1]), lses[2])
    es = [jnp.exp2(l - top) for l in lses]
    inv = 1.0 / (es[0] + es[1] + es[2])
    pa = None
    for g in range(len(outs)):
        y_g = (outs[g] * (es[g] * inv)).astype(BF16)
        part = jnp.dot(y_g, wpa_ref[g * GROUP_WIDTH:(g + 1) * GROUP_WIDTH, :],
                       preferred_element_type=F32)
        pa = part if pa is None else pa + part

    merged = (gate(0) * pa + gate(1) * pr + gate(2) * pm).astype(BF16)
    h_ref[...] = x + jnp.dot(merged, wo_ref[...], preferred_element_type=F32)


def _merge(x2, g_mix, os_, lses, y_r, q_m, k_m, v_m, w_gate, b_gate, w_pa, w_pr, w_pm, w_out,
           seq):
    t = x2.shape[0]
    tm = ROW_TILE
    tiles_per_seq = seq // tm
    row = lambda i: (i, 0)
    mem_len = k_m.shape[1]
    mem_spec = pl.BlockSpec((1, mem_len, MEM_WIDTH), lambda i: (i // tiles_per_seq, 0, 0))
    group_specs = [pl.BlockSpec((tm // d, d * GROUP_WIDTH), row) for d in _DILATIONS]
    n_stages = 2 * sum((d > 1) + (d > MAX_ROW_STRIDE) for d in _DILATIONS)
    return pl.pallas_call(
        _merge_kernel,
        grid=(t // tm,),
        in_specs=[
            pl.BlockSpec((tm, D_MODEL), row), _resident((1, D_MODEL)),
            *group_specs, *group_specs,
            pl.BlockSpec((RET_PAIRS, tm, 2 * RET_V_DIM), lambda i: (0, i, 0)),
            pl.BlockSpec((tm, MEM_WIDTH), row),
            mem_spec, mem_spec,
            _resident(w_gate.shape), _resident(b_gate.shape), _resident(w_pa.shape),
            _resident(w_pr.shape), _resident(w_pm.shape), _resident(w_out.shape),
        ],
        out_specs=pl.BlockSpec((tm, D_MODEL), row),
        out_shape=jax.ShapeDtypeStruct((t, D_MODEL), F32),
        scratch_shapes=[pltpu.VMEM((GROUP_WIDTH // LANES, tm, LANES), F32)] * n_stages,
        compiler_params=_params(1),
        name="merge",
    )(x2, g_mix, *os_, *lses, y_r, q_m, k_m, v_m, w_gate, b_gate, w_pa, w_pr, w_pm, w_out)


FF_CHUNK = MXU_WIDTH
FF_GROUP_CHUNKS = (4, 4, 3)
HALO = SUBLANES


N_FF_CHUNKS = D_FF // FF_CHUNK


def _ffn_kernel(h_ref, prev_ref, next_ref, gffn_ref, wup_ref, cw_ref, cb_ref, wdown_ref,
                gfin_ref, out_ref, *scratch, tiles_per_seq):
    u_refs, act_refs, y_ref = scratch[:N_FF_CHUNKS], scratch[N_FF_CHUNKS:-1], scratch[-1]
    i = pl.program_id(0)
    tm = h_ref.shape[0]
    g = gffn_ref[...]
    has_prev = (i % tiles_per_seq) != 0
    has_next = (i % tiles_per_seq) != tiles_per_seq - 1
    n_prev = jnp.where(has_prev, _rms(prev_ref[...], g), 0.0)
    n_next = jnp.where(has_next, _rms(next_ref[...], g), 0.0)
    n_ext = jnp.concatenate([n_prev, _rms(h_ref[...], g), n_next], axis=0).astype(BF16)

    half_rows = tm // 2
    lane_blocks = FF_CHUNK // LANES

    def conv(u_ref, ab, c0, parity):
        parts = []
        for jb in range(lane_blocks):
            sl = slice(c0 + jb * LANES, c0 + (jb + 1) * LANES)
            taps = [u_ref[ab, jb, pl.ds(HALO - 1 + parity + k, half_rows, stride=2), :]
                    * cw_ref[k:k + 1, sl] for k in range(3)]
            parts.append(taps[0] + taps[1] + taps[2] + cb_ref[:, sl])
        return jnp.concatenate(parts, axis=-1)

    def up(chunk):
        c0 = chunk * FF_CHUNK
        for ab, col in enumerate((c0, D_FF + c0)):
            u = jnp.dot(n_ext, wup_ref[:, col:col + FF_CHUNK], preferred_element_type=F32)
            for jb in range(lane_blocks):
                u_refs[chunk][ab, jb] = u[:, jb * LANES:(jb + 1) * LANES]

    def gate(chunk, act_ref, col):
        c0 = chunk * FF_CHUNK
        for parity in range(2):
            a = conv(u_refs[chunk], 0, c0, parity)
            b = conv(u_refs[chunk], 1, D_FF + c0, parity)
            act_ref[parity * half_rows:(parity + 1) * half_rows, col:col + FF_CHUNK] = (
                (a * _sigmoid(a) * b).astype(BF16))

    def down(gi):
        grp = groups[gi]
        return jnp.dot(act_refs[gi][...], wdown_ref[grp[0] * FF_CHUNK:(grp[-1] + 1) * FF_CHUNK, :],
                       preferred_element_type=F32)

    assert sum(FF_GROUP_CHUNKS) == N_FF_CHUNKS
    first = [sum(FF_GROUP_CHUNKS[:g]) for g in range(len(FF_GROUP_CHUNKS))]
    groups = [range(f, f + n) for f, n in zip(first, FF_GROUP_CHUNKS)]
    y = None
    for step in range(len(groups) + 2):
        if step < len(groups):
            for chunk in groups[step]:
                up(chunk)
        if 2 <= step:
            part = down(step - 2)
            y = part if y is None else y + part
        if 1 <= step <= len(groups):
            for j, chunk in enumerate(groups[step - 1]):
                gate(chunk, act_refs[step - 1], j * FF_CHUNK)
    for parity in range(2):
        for jb in range(D_MODEL // LANES):
            y_ref[jb, pl.ds(parity, half_rows, stride=2), :] = (
                y[parity * half_rows:(parity + 1) * half_rows, jb * LANES:(jb + 1) * LANES])
    y_tok = jnp.concatenate([y_ref[jb] for jb in range(D_MODEL // LANES)], axis=-1)
    out_ref[...] = _rms(h_ref[...] + y_tok, gfin_ref[...])


def _ffn(h, g_ffn, w_up, conv_w, conv_b, w_down, g_final, seq):
    t = h.shape[0]
    tm = ROW_TILE
    tiles_per_seq = seq // tm
    halo_blocks = tm // HALO
    last_block = t // HALO - 1
    row = lambda i: (i, 0)
    return pl.pallas_call(
        functools.partial(_ffn_kernel, tiles_per_seq=tiles_per_seq),
        grid=(t // tm,),
        in_specs=[
            pl.BlockSpec((tm, D_MODEL), row),
            pl.BlockSpec((HALO, D_MODEL), lambda i: (jnp.maximum(i * halo_blocks - 1, 0), 0)),
            pl.BlockSpec((HALO, D_MODEL),
                         lambda i: (jnp.minimum((i + 1) * halo_blocks, last_block), 0)),
            _resident((1, D_MODEL)), _resident(w_up.shape), _resident(conv_w.shape),
            _resident(conv_b.shape), _resident(w_down.shape), _resident((1, D_MODEL)),
        ],
        out_specs=pl.BlockSpec((tm, D_MODEL), row),
        out_shape=jax.ShapeDtypeStruct((t, D_MODEL), F32),
        scratch_shapes=[*[pltpu.VMEM((2, FF_CHUNK // LANES, tm + 2 * HALO, LANES), F32)]
                        * N_FF_CHUNKS,
                        *[pltpu.VMEM((tm, n * FF_CHUNK), BF16) for n in FF_GROUP_CHUNKS],
                        pltpu.VMEM((D_MODEL // LANES, tm, LANES), F32)],
        compiler_params=_params(1),
        name="ffn",
    )(h, h, h, g_ffn, w_up, conv_w, conv_b, w_down, g_final)


def _rotary_tables(seq):
    inv = ROPE_THETA ** (-jnp.arange(0, HEAD_DIM, 2, dtype=F32) / HEAD_DIM)
    ang = jnp.arange(seq, dtype=F32)[:, None] * inv[None, :]
    cos, sin = jnp.cos(ang), jnp.sin(ang)
    reps = LANES // HEAD_DIM
    cos_t = jnp.tile(jnp.concatenate([cos, cos], axis=-1), (1, reps))
    sin_t = jnp.tile(jnp.concatenate([-sin, sin], axis=-1), (1, reps))
    return cos_t, sin_t


def _pair_lanes(v):
    pairs = v.reshape(RET_PAIRS, 2, 1)
    lanes = jnp.broadcast_to(pairs, (RET_PAIRS, 2, RET_QK_DIM)).reshape(RET_PAIRS, 1, LANES)
    return jnp.broadcast_to(lanes, (RET_PAIRS, SUBLANES, LANES)).astype(F32)


def _layer(h2, mem, batch, seq, g_mix, w_in, w_mem_kv, g_mem, decay_fwd, decay_bwd, g_ret,
           w_proj_attn, w_proj_ret, w_proj_mem, w_gate, b_gate, w_out):
    bf = lambda w: w.astype(BF16)
    row = lambda v: v.reshape(1, -1).astype(F32)
    cos_t, sin_t = _rotary_tables(seq)
    proj = _in_proj(h2, row(g_mix), cos_t, sin_t, bf(w_in), seq)
    qa, ka, va = proj[0:3], proj[3:6], proj[6:9]
    q_r, k_r, v_r, g_r, q_m = proj[9:]

    os_, lses = [], []
    for g, (_, dilation) in enumerate(ATTN_GROUPS):
        o, lse = _band_attn(qa[g], ka[g], va[g], batch, seq, dilation)
        os_.append(o)
        lses.append(lse)

    y_r = _retention(q_r, k_r, v_r, g_r, _pair_lanes(decay_fwd), _pair_lanes(decay_bwd),
                     row(g_ret), batch, seq)
    k_m, v_m = _mem_kv(mem, row(g_mem), bf(w_mem_kv))
    return _merge(h2, row(g_mix), os_, lses, y_r, q_m, k_m, v_m, bf(w_gate), row(b_gate),
                  bf(w_proj_attn), bf(w_proj_ret), bf(w_proj_mem), bf(w_out), seq)


def kernel(x, mem, g_mix, w_in, w_mem_kv, g_mem, ret_decay_fwd, ret_decay_bwd, g_ret,
           w_proj_attn, w_proj_ret, w_proj_mem, w_gate, b_gate, w_out,
           g_ffn, w_up, conv_w, conv_b, w_down, g_final):
    batch, seq, d = x.shape
    depth = w_in.shape[0]
    assert d == D_MODEL and depth == 1 and seq % ROW_TILE == 0 and seq % WIDE_ROW_TILE == 0
    h2 = x.reshape(batch * seq, d)
    l = 0
    h2 = _layer(h2, mem, batch, seq, g_mix[l], w_in[l], w_mem_kv[l], g_mem[l],
                ret_decay_fwd[l], ret_decay_bwd[l], g_ret[l], w_proj_attn[l], w_proj_ret[l],
                w_proj_mem[l], w_gate[l], b_gate[l], w_out[l])
    out = _ffn(h2, g_ffn[l].reshape(1, -1), w_up[l].astype(BF16), conv_w[l],
               conv_b[l].reshape(1, -1), w_down[l].astype(BF16), g_final.reshape(1, -1), seq)
    return out.reshape(batch, seq, d)
```

```python
import functools
import math

import jax
import jax.numpy as jnp
from jax import lax
from jax.experimental import pallas as pl
from jax.experimental.pallas import tpu as pltpu

D_MODEL = 1024
HEAD_DIM = 64
ATTN_GROUPS = ((128, 1), (512, 4), (2048, 16))
GROUP_WIDTH = 4 * HEAD_DIM
ATTN_WIDTH = 3 * GROUP_WIDTH
BAND_HALF = 64
RET_HEADS = 6
RET_QK_DIM = 64
RET_V_DIM = 128
RET_QK_WIDTH = RET_HEADS * RET_QK_DIM
RET_V_WIDTH = RET_HEADS * RET_V_DIM
MEM_HEADS = 4
MEM_HEAD_DIM = 128
MEM_WIDTH = MEM_HEADS * MEM_HEAD_DIM
D_FF = 2816
ROPE_THETA = 10000.0
EPS = 1e-6
NEG_INF = -1e30
LOG2_E = math.log2(math.e)

LANES = 128
SUBLANES = 8
MXU_WIDTH = 256
VMEM_LIMIT = 56 * 1024 * 1024

ROW_TILE = 512
WIDE_ROW_TILE = 1024
RET_CHUNK = 256
Q_BLOCK = 128
K_SPAN = Q_BLOCK + 2 * BAND_HALF
MAX_ROW_STRIDE = 4
STATE_UNROLL = 4
MIX_UNROLL = 4
ATTN_PAIRS_IN_FLIGHT = 16

BF16 = jnp.bfloat16
F32 = jnp.float32


def _params(n_grid_axes):
    return pltpu.CompilerParams(
        dimension_semantics=("arbitrary",) * n_grid_axes, vmem_limit_bytes=VMEM_LIMIT)


def _resident(shape):
    nd = len(shape)
    return pl.BlockSpec(shape, lambda *_: (0,) * nd, pipeline_mode=pl.Buffered(1))


def _rms(x, g):
    return x * lax.rsqrt(jnp.mean(x * x, axis=-1, keepdims=True) + EPS) * g


def _qk_lane_is_head0(lane):
    return lane < HEAD_DIM


def _sigmoid(x):
    return 0.5 * jnp.tanh(0.5 * x) + 0.5


_DILATIONS = tuple(d for _, d in ATTN_GROUPS)
RET_PAIRS = RET_HEADS // 2
_IN_SEGMENTS = (
    *[(GROUP_WIDTH, True, HEAD_DIM ** -0.5 * LOG2_E, d, 1) for d in _DILATIONS],
    *[(GROUP_WIDTH, True, 1.0, d, 1) for d in _DILATIONS],
    *[(GROUP_WIDTH, False, 1.0, d, 1) for d in _DILATIONS],
    (RET_QK_WIDTH, True, 1.0, 1, RET_PAIRS),
    (RET_QK_WIDTH, True, RET_QK_DIM ** -0.5, 1, RET_PAIRS),
    (RET_V_WIDTH, False, 1.0, 1, RET_PAIRS),
    (RET_V_WIDTH, False, 1.0, 1, RET_PAIRS),
    (MEM_WIDTH, False, 1.0, 1, 1),
)


def _in_proj_kernel(x_ref, g_ref, cos_ref, sin_ref, w_ref, *refs):
    out_refs, stage_ref, stage2_ref = refs[:-2], refs[-2], refs[-1]
    tm = x_ref.shape[0]
    n = _rms(x_ref[...], g_ref[...]).astype(BF16)
    cos = cos_ref[...]
    sin = sin_ref[...]
    lane = lax.broadcasted_iota(jnp.int32, cos.shape, 1)
    low_half = (lane & (HEAD_DIM // 2)) == 0

    def rotate(a):
        partner = jnp.where(low_half, pltpu.roll(a, LANES - HEAD_DIM // 2, 1),
                            pltpu.roll(a, HEAD_DIM // 2, 1))
        return a * cos + partner * sin

    col = 0
    for out_ref, (width, rotary, scale, dil, parts) in zip(out_refs, _IN_SEGMENTS):
        for c in range(0, width, MXU_WIDTH):
            cw = min(MXU_WIDTH, width - c)
            acc = jnp.dot(n, w_ref[:, col + c:col + c + cw], preferred_element_type=F32)
            for j in range(0, cw, LANES):
                a = acc[:, j:j + LANES]
                if rotary:
                    a = rotate(a)
                if scale != 1.0:
                    a = a * scale
                if parts > 1:
                    part, off = divmod(c + j, width // parts)
                    out_ref[part, :, off:off + LANES] = a.astype(BF16)
                elif dil == 1:
                    out_ref[:, c + j:c + j + LANES] = a.astype(BF16)
                else:
                    stage_ref[j // LANES] = a
            if dil > 1:
                assert width == cw == stage_ref.shape[0] * LANES
                for jb in range(width // LANES):
                    if dil > MAX_ROW_STRIDE:
                        assert dil == MAX_ROW_STRIDE * MAX_ROW_STRIDE
                        n1 = tm // MAX_ROW_STRIDE
                        for r1 in range(MAX_ROW_STRIDE):
                            stage2_ref[jb, r1 * n1:(r1 + 1) * n1, :] = (
                                stage_ref[jb, pl.ds(r1, n1, stride=MAX_ROW_STRIDE), :])
                        residue_rows = [
                            (r1 + MAX_ROW_STRIDE * r2,
                             stage2_ref[jb, pl.ds(r1 * n1 + r2, tm // dil, stride=MAX_ROW_STRIDE), :])
                            for r1 in range(MAX_ROW_STRIDE) for r2 in range(MAX_ROW_STRIDE)]
                    else:
                        residue_rows = [(r, stage_ref[jb, pl.ds(r, tm // dil, stride=dil), :])
                                        for r in range(dil)]
                    for r, rows in residue_rows:
                        out_ref[:, r * width + jb * LANES:r * width + (jb + 1) * LANES] = (
                            rows.astype(BF16))
        col += width


def _in_proj(x2, g_mix, cos_t, sin_t, w_in, seq):
    t = x2.shape[0]
    tm = WIDE_ROW_TILE
    tiles_per_seq = seq // tm
    row = lambda i: (i, 0)
    pos = lambda i: (i % tiles_per_seq, 0)
    out_shape, out_specs = [], []
    for (w, _, _, d, parts) in _IN_SEGMENTS:
        if parts > 1:
            out_shape.append(jax.ShapeDtypeStruct((parts, t, w // parts), BF16))
            out_specs.append(pl.BlockSpec((parts, tm, w // parts), lambda i: (0, i, 0)))
        else:
            out_shape.append(jax.ShapeDtypeStruct((t // d, d * w), BF16))
            out_specs.append(pl.BlockSpec((tm // d, d * w), row))
    return pl.pallas_call(
        _in_proj_kernel,
        grid=(t // tm,),
        in_specs=[
            pl.BlockSpec((tm, D_MODEL), row),
            _resident((1, D_MODEL)),
            pl.BlockSpec((tm, LANES), pos),
            pl.BlockSpec((tm, LANES), pos),
            _resident(w_in.shape),
        ],
        out_specs=out_specs,
        out_shape=out_shape,
        scratch_shapes=[pltpu.VMEM((GROUP_WIDTH // LANES, tm, LANES), F32)] * 2,
        compiler_params=_params(1),
        name="in_proj",
    )(x2, g_mix, cos_t, sin_t, w_in)


def _band_attn_kernel(q_ref, k_ref, v_ref, o_ref, lse_ref, *, seq_len, n_res):
    n_blocks = seq_len // Q_BLOCK
    qi = lax.broadcasted_iota(jnp.int32, (Q_BLOCK, K_SPAN), 0)
    kj = lax.broadcasted_iota(jnp.int32, (Q_BLOCK, K_SPAN), 1)
    rel = qi - kj
    lane = lax.broadcasted_iota(jnp.int32, (Q_BLOCK, LANES), 1)
    head0 = lane < HEAD_DIM
    qk_head0 = _qk_lane_is_head0(lane)

    def band_bias(q0, k0):
        return jnp.where(jnp.abs(rel + (q0 - k0)) <= BAND_HALF, 0.0, NEG_INF)

    def block(q0, k0, bias):
        for pair in range(n_res * 2):
            c0 = pair * LANES
            qp = q_ref[0, pl.ds(q0, Q_BLOCK), c0:c0 + LANES]
            kp = k_ref[0, pl.ds(k0, K_SPAN), c0:c0 + LANES]
            vp = v_ref[0, pl.ds(k0, K_SPAN), c0:c0 + LANES]
            v_ones = jnp.concatenate([vp, jnp.ones_like(vp)], axis=-1)
            outs = []
            for h in range(2):
                mask_h = qk_head0 if h == 0 else jnp.logical_not(qk_head0)
                qh = jnp.where(mask_h, qp, jnp.zeros_like(qp))
                s = lax.dot_general(qh, kp, (((1,), (1,)), ((), ())),
                                    preferred_element_type=F32) + bias
                m = jnp.max(s, axis=-1, keepdims=True)
                p = jnp.exp2(s - m).astype(BF16)
                ol = jnp.dot(p, v_ones, preferred_element_type=F32)
                o, l = ol[:, :LANES], ol[:, LANES:]
                outs.append((o / l, m + jnp.log2(l)))
            o_pair = jnp.where(head0, outs[0][0], outs[1][0])
            lse_pair = jnp.where(head0, outs[0][1], outs[1][1])
            o_ref[0, pl.ds(q0, Q_BLOCK), c0:c0 + LANES] = o_pair.astype(BF16)
            lse_ref[0, pl.ds(q0, Q_BLOCK), c0:c0 + LANES] = lse_pair

    last_q0 = seq_len - Q_BLOCK
    last_k0 = seq_len - K_SPAN
    block(0, 0, band_bias(0, 0))
    if n_blocks > 2:
        mid_bias = band_bias(BAND_HALF, 0)

        def interior(i, carry):
            q0 = pl.multiple_of(i * Q_BLOCK, Q_BLOCK)
            k0 = pl.multiple_of(i * Q_BLOCK - BAND_HALF, BAND_HALF)
            block(q0, k0, mid_bias)
            return carry

        unroll = max(1, ATTN_PAIRS_IN_FLIGHT // (2 * n_res))
        lax.fori_loop(1, n_blocks - 1, interior, 0, unroll=unroll)
    block(last_q0, last_k0, band_bias(last_q0, last_k0))


def _band_attn(q, k, v, batch, seq, dilation):
    seq_len = seq // dilation
    n_res = min(dilation, 4)
    width = dilation * GROUP_WIDTH
    view = lambda a: a.reshape(batch, seq_len, width)
    blk = (1, seq_len, n_res * GROUP_WIDTH)
    idx = lambda b, r: (b, 0, r)
    o, lse = pl.pallas_call(
        functools.partial(_band_attn_kernel, seq_len=seq_len, n_res=n_res),
        grid=(batch, dilation // n_res),
        in_specs=[pl.BlockSpec(blk, idx)] * 3,
        out_specs=[pl.BlockSpec(blk, idx)] * 2,
        out_shape=[jax.ShapeDtypeStruct((batch, seq_len, width), BF16),
                   jax.ShapeDtypeStruct((batch, seq_len, width), F32)],
        compiler_params=_params(2),
        name=f"band_attn_d{dilation}",
    )(view(q), view(k), view(v))
    return o.reshape(batch * seq_len, width), lse.reshape(batch * seq_len, width)


def _log_sigmoid(x):
    return jnp.minimum(x, 0.0) - jnp.log1p(jnp.exp(-jnp.abs(x)))


def _retention_kernel(decf_ref, decb_ref, q_ref, k_ref, v_ref, gate_ref, gret_ref, o_ref,
                      st_ref, s_ref, y_ref, zeta_ref, cdec_ref, xi_ref, din_ref, *, seq):
    c = RET_CHUNK
    n_chunks = seq // c
    pair_w = 2 * RET_QK_DIM
    dv = RET_V_DIM
    st_shape = (pair_w, 2 * RET_V_DIM)

    @pl.when(pl.program_id(1) == 0)
    def _():
        lg_f = _log_sigmoid(decf_ref[0])
        lg_b = _log_sigmoid(decb_ref[0])
        pos = lax.broadcasted_iota(jnp.int32, (c, pair_w), 0).astype(F32)
        zeta_ref[0] = jnp.exp(lg_f[0:1, :] * (c - 1.0 - pos))
        zeta_ref[1] = jnp.exp(lg_b[0:1, :] * pos)
        bcast = lambda lg, h, shape: jnp.broadcast_to(
            lg[0:1, h * RET_QK_DIM:h * RET_QK_DIM + 1], shape)
        row_is_h0 = _qk_lane_is_head0(lax.broadcasted_iota(jnp.int32, st_shape, 0))
        for d, lg in enumerate((lg_f, lg_b)):
            cdec_ref[d] = jnp.exp(
                jnp.where(row_is_h0, bcast(lg, 0, st_shape), bcast(lg, 1, st_shape)) * float(c))
        ci = lax.broadcasted_iota(jnp.int32, (c, c), 0)
        mi = lax.broadcasted_iota(jnp.int32, (c, c), 1)
        delta = (ci - mi).astype(F32)
        for h in range(2):
            xi_ref[h] = jnp.concatenate([jnp.exp(bcast(lg_f, h, (c, dv)) * (pos + 1.0)),
                                         jnp.exp(bcast(lg_b, h, (c, dv)) * (c - pos))], axis=-1)
            din_ref[h] = jnp.where(
                ci >= mi, jnp.exp(bcast(lg_f, h, (c, c)) * jnp.maximum(delta, 0.0)),
                jnp.exp(bcast(lg_b, h, (c, c)) * jnp.maximum(-delta, 0.0)))

    def row0(n):
        return n * c if isinstance(n, int) else pl.multiple_of(n * c, c)

    def chunk_kv(n, direction):
        r0 = row0(n)
        kz = (k_ref[0, pl.ds(r0, c), :].astype(F32) * zeta_ref[direction]).astype(BF16)
        return lax.dot_general(kz, v_ref[0, pl.ds(r0, c), :], (((0,), (0,)), ((), ())),
                               preferred_element_type=F32)

    lane = lax.broadcasted_iota(jnp.int32, (c, pair_w), 1)
    head0 = _qk_lane_is_head0(lane)

    def head_query(qp, h):
        mask_h = head0 if h == 0 else jnp.logical_not(head0)
        return jnp.where(mask_h, qp, jnp.zeros_like(qp))

    def score_chunk(n):
        r0 = row0(n)
        qp = q_ref[0, pl.ds(r0, c), :]
        kp = k_ref[0, pl.ds(r0, c), :]
        for h in range(2):
            s = lax.dot_general(head_query(qp, h), kp, (((1,), (1,)), ((), ())),
                                preferred_element_type=F32)
            s_ref[pl.ds(r0, c), h * c:(h + 1) * c] = (s * din_ref[h]).astype(BF16)

    def state_step(j, carry):
        st_f, st_b = carry
        jb = n_chunks - 1 - j
        score_chunk(j)
        for h in range(2):
            hs = slice(h * dv, (h + 1) * dv)
            st_ref[j, :, (2 * h) * dv:(2 * h + 1) * dv] = st_f[:, hs].astype(BF16)
            st_ref[jb, :, (2 * h + 1) * dv:(2 * h + 2) * dv] = st_b[:, hs].astype(BF16)
        st_f = st_f * cdec_ref[0] + chunk_kv(j, 0)
        st_b = st_b * cdec_ref[1] + chunk_kv(jb, 1)
        return st_f, st_b

    zero_state = jnp.zeros((pair_w, 2 * RET_V_DIM), F32)
    lax.fori_loop(0, n_chunks, state_step, (zero_state, zero_state), unroll=STATE_UNROLL)

    def mix_chunk(n):
        r0 = row0(n)
        qp = q_ref[0, pl.ds(r0, c), :]
        for h in range(2):
            vs = slice(h * dv, (h + 1) * dv)
            y = jnp.dot(s_ref[pl.ds(r0, c), h * c:(h + 1) * c], v_ref[0, pl.ds(r0, c), vs],
                        preferred_element_type=F32)
            cross = jnp.dot(head_query(qp, h), st_ref[n, :, 2 * h * dv:(2 * h + 2) * dv],
                            preferred_element_type=F32) * xi_ref[h]
            y_ref[pl.ds(r0, c), vs] = y + cross[:, :dv] + cross[:, dv:]

    def norm_chunk(n):
        r0 = row0(n)
        for h in range(2):
            vs = slice(h * dv, (h + 1) * dv)
            y = y_ref[pl.ds(r0, c), vs]
            mu = jnp.mean(y, axis=-1, keepdims=True)
            yc = y - mu
            var = jnp.mean(yc * yc, axis=-1, keepdims=True)
            yn = yc * lax.rsqrt(var + EPS) * gret_ref[:, vs]
            gate = gate_ref[0, pl.ds(r0, c), vs].astype(F32)
            o_ref[0, pl.ds(r0, c), vs] = (yn * (gate * _sigmoid(gate))).astype(BF16)

    n_batches = n_chunks // MIX_UNROLL

    def mix_norm(i, carry):
        for u in range(MIX_UNROLL):
            norm_chunk((i - 1) * MIX_UNROLL + u)
        for u in range(MIX_UNROLL):
            mix_chunk(i * MIX_UNROLL + u)
        return carry

    for u in range(MIX_UNROLL):
        mix_chunk(u)
    lax.fori_loop(1, n_batches, mix_norm, 0)
    for u in range(MIX_UNROLL):
        norm_chunk((n_batches - 1) * MIX_UNROLL + u)


def _retention(qr, kr, vr, gr, dec_f, dec_b, g_ret, batch, seq):
    pairs = RET_PAIRS
    n_chunks = seq // RET_CHUNK
    v3 = lambda a: a.reshape(pairs * batch, seq, a.shape[-1])
    slab = lambda p, b: (p * batch + b, 0, 0)
    qk_spec = pl.BlockSpec((1, seq, 2 * RET_QK_DIM), slab)
    v_spec = pl.BlockSpec((1, seq, 2 * RET_V_DIM), slab)
    dec_spec = pl.BlockSpec((1, SUBLANES, LANES), lambda p, b: (p, 0, 0))
    out = pl.pallas_call(
        functools.partial(_retention_kernel, seq=seq),
        grid=(pairs, batch),
        in_specs=[dec_spec, dec_spec, qk_spec, qk_spec, v_spec, v_spec,
                  pl.BlockSpec((1, 2 * RET_V_DIM), lambda p, b: (0, p))],
        out_specs=v_spec,
        out_shape=jax.ShapeDtypeStruct((pairs * batch, seq, 2 * RET_V_DIM), BF16),
        scratch_shapes=[pltpu.VMEM((n_chunks, 2 * RET_QK_DIM, 4 * RET_V_DIM), BF16),
                        pltpu.VMEM((seq, 2 * RET_CHUNK), BF16),
                        pltpu.VMEM((seq, 2 * RET_V_DIM), F32),
                        pltpu.VMEM((2, RET_CHUNK, 2 * RET_QK_DIM), F32),
                        pltpu.VMEM((2, 2 * RET_QK_DIM, 2 * RET_V_DIM), F32),
                        pltpu.VMEM((2, RET_CHUNK, 2 * RET_V_DIM), F32),
                        pltpu.VMEM((2, RET_CHUNK, RET_CHUNK), F32)],
        compiler_params=_params(2),
        name="retention",
    )(dec_f, dec_b, v3(qr), v3(kr), v3(vr), v3(gr), g_ret)
    return out.reshape(pairs, batch * seq, 2 * RET_V_DIM)


def _mem_kv_kernel(mem_ref, g_ref, w_ref, k_ref, v_ref):
    n = _rms(mem_ref[0], g_ref[...]).astype(BF16)
    kv = jnp.dot(n, w_ref[...], preferred_element_type=F32)
    k_ref[0] = kv[:, :MEM_WIDTH].astype(BF16)
    v_ref[0] = kv[:, MEM_WIDTH:].astype(BF16)


def _mem_kv(mem, g_mem, w_mem_kv):
    batch, mem_len, _ = mem.shape
    out = jax.ShapeDtypeStruct((batch, mem_len, MEM_WIDTH), BF16)
    spec = pl.BlockSpec((1, mem_len, MEM_WIDTH), lambda b: (b, 0, 0))
    return pl.pallas_call(
        _mem_kv_kernel,
        grid=(batch,),
        in_specs=[pl.BlockSpec((1, mem_len, D_MODEL), lambda b: (b, 0, 0)),
                  _resident((1, D_MODEL)), _resident(w_mem_kv.shape)],
        out_specs=[spec, spec],
        out_shape=[out, out],
        compiler_params=_params(1),
        name="mem_kv",
    )(mem, g_mem, w_mem_kv)


def _merge_kernel(x_ref, g_ref, o0_ref, o1_ref, o2_ref, l0_ref, l1_ref, l2_ref, yr_ref, qm_ref,
                  km_ref, vm_ref, wg_ref, bg_ref, wpa_ref, wpr_ref, wpm_ref, wo_ref, h_ref,
                  *stage_refs):
    x = x_ref[...]
    tm = x.shape[0]
    n = _rms(x, g_ref[...]).astype(BF16)
    def gate(i):
        return _sigmoid(jnp.dot(n, wg_ref[:, i * D_MODEL:(i + 1) * D_MODEL],
                                preferred_element_type=F32)
                        + bg_ref[:, i * D_MODEL:(i + 1) * D_MODEL])

    def token_rows(ref, dil, stages):
        if dil == 1:
            return ref[...].astype(F32)
        stage_ref = next(stages)
        n_lane_blocks = GROUP_WIDTH // LANES
        residue = lambda r, jb: ref[:, r * GROUP_WIDTH + jb * LANES:
                                    r * GROUP_WIDTH + (jb + 1) * LANES].astype(F32)
        if dil > MAX_ROW_STRIDE:
            assert dil == MAX_ROW_STRIDE * MAX_ROW_STRIDE
            stage2_ref = next(stages)
            n1 = tm // MAX_ROW_STRIDE
            for jb in range(n_lane_blocks):
                for r1 in range(MAX_ROW_STRIDE):
                    for r2 in range(MAX_ROW_STRIDE):
                        stage2_ref[jb, pl.ds(r1 * n1 + r2, tm // dil, stride=MAX_ROW_STRIDE), :] = (
                            residue(r1 + MAX_ROW_STRIDE * r2, jb))
                    stage_ref[jb, pl.ds(r1, n1, stride=MAX_ROW_STRIDE), :] = (
                        stage2_ref[jb, r1 * n1:(r1 + 1) * n1, :])
        else:
            for r in range(dil):
                for jb in range(n_lane_blocks):
                    stage_ref[jb, pl.ds(r, tm // dil, stride=dil), :] = residue(r, jb)
        return jnp.concatenate([stage_ref[jb] for jb in range(n_lane_blocks)], axis=-1)

    y_r = jnp.concatenate([yr_ref[p] for p in range(RET_PAIRS)], axis=-1)
    pr = jnp.dot(y_r, wpr_ref[...], preferred_element_type=F32)

    ym = []
    for hd in range(MEM_HEADS):
        cs = slice(hd * MEM_HEAD_DIM, (hd + 1) * MEM_HEAD_DIM)
        s = lax.dot_general(qm_ref[:, cs], km_ref[0, :, cs], (((1,), (1,)), ((), ())),
                            preferred_element_type=F32) * (MEM_HEAD_DIM ** -0.5 * LOG2_E)
        p = jnp.exp2(s - jnp.max(s, axis=-1, keepdims=True)).astype(BF16)
        vm = vm_ref[0, :, cs]
        ol = jnp.dot(p, jnp.concatenate([vm, jnp.ones_like(vm)], axis=-1),
                     preferred_element_type=F32)
        ym.append((ol[:, :MEM_HEAD_DIM] / ol[:, MEM_HEAD_DIM:]).astype(BF16))
    pm = jnp.dot(jnp.concatenate(ym, axis=-1), wpm_ref[...], preferred_element_type=F32)

    stages = iter(stage_refs)
    outs, lses = [], []
    for o_ref, l_ref, dil in zip((o0_ref, o1_ref, o2_ref), (l0_ref, l1_ref, l2_ref), _DILATIONS):
        outs.append(token_rows(o_ref, dil, stages))
        lses.append(token_rows(l_ref, dil, stages))
    top = jnp.maximum(jnp.maximum(lses[0], lses[1]), lses[2])
    es = [jnp.exp2(l - top) for l in lses]
    inv = 1.0 / (es[0] + es[1] + es[2])
    pa = None
    for g in range(len(outs)):
        y_g = (outs[g] * (es[g] * inv)).astype(BF16)
        part = jnp.dot(y_g, wpa_ref[g * GROUP_WIDTH:(g + 1) * GROUP_WIDTH, :],
                       preferred_element_type=F32)
        pa = part if pa is None else pa + part

    merged = (gate(0) * pa + gate(1) * pr + gate(2) * pm).astype(BF16)
    h_ref[...] = x + jnp.dot(merged, wo_ref[...], preferred_element_type=F32)


def _merge(x2, g_mix, os_, lses, y_r, q_m, k_m, v_m, w_gate, b_gate, w_pa, w_pr, w_pm, w_out,
           seq):
    t = x2.shape[0]
    tm = ROW_TILE
    tiles_per_seq = seq // tm
    row = lambda i: (i, 0)
    mem_len = k_m.shape[1]
    mem_spec = pl.BlockSpec((1, mem_len, MEM_WIDTH), lambda i: (i // tiles_per_seq, 0, 0))
    group_specs = [pl.BlockSpec((tm // d, d * GROUP_WIDTH), row) for d in _DILATIONS]
    n_stages = 2 * sum((d > 1) + (d > MAX_ROW_STRIDE) for d in _DILATIONS)
    return pl.pallas_call(
        _merge_kernel,
        grid=(t // tm,),
        in_specs=[
            pl.BlockSpec((tm, D_MODEL), row), _resident((1, D_MODEL)),
            *group_specs, *group_specs,
            pl.BlockSpec((RET_PAIRS, tm, 2 * RET_V_DIM), lambda i: (0, i, 0)),
            pl.BlockSpec((tm, MEM_WIDTH), row),
            mem_spec, mem_spec,
            _resident(w_gate.shape), _resident(b_gate.shape), _resident(w_pa.shape),
            _resident(w_pr.shape), _resident(w_pm.shape), _resident(w_out.shape),
        ],
        out_specs=pl.BlockSpec((tm, D_MODEL), row),
        out_shape=jax.ShapeDtypeStruct((t, D_MODEL), F32),
        scratch_shapes=[pltpu.VMEM((GROUP_WIDTH // LANES, tm, LANES), F32)] * n_stages,
        compiler_params=_params(1),
        name="merge",
    )(x2, g_mix, *os_, *lses, y_r, q_m, k_m, v_m, w_gate, b_gate, w_pa, w_pr, w_pm, w_out)


FF_CHUNK = MXU_WIDTH
FF_GROUP_CHUNKS = (4, 4, 3)
HALO = SUBLANES


N_FF_CHUNKS = D_FF // FF_CHUNK


def _ffn_kernel(h_ref, prev_ref, next_ref, gffn_ref, wup_ref, cw_ref, cb_ref, wdown_ref,
                gfin_ref, out_ref, *scratch, tiles_per_seq):
    u_refs, act_refs, y_ref = scratch[:N_FF_CHUNKS], scratch[N_FF_CHUNKS:-1], scratch[-1]
    i = pl.program_id(0)
    tm = h_ref.shape[0]
    g = gffn_ref[...]
    has_prev = (i % tiles_per_seq) != 0
    has_next = (i % tiles_per_seq) != tiles_per_seq - 1
    n_prev = jnp.where(has_prev, _rms(prev_ref[...], g), 0.0)
    n_next = jnp.where(has_next, _rms(next_ref[...], g), 0.0)
    n_ext = jnp.concatenate([n_prev, _rms(h_ref[...], g), n_next], axis=0).astype(BF16)

    half_rows = tm // 2
    lane_blocks = FF_CHUNK // LANES

    def conv(u_ref, ab, c0, parity):
        parts = []
        for jb in range(lane_blocks):
            sl = slice(c0 + jb * LANES, c0 + (jb + 1) * LANES)
            taps = [u_ref[ab, jb, pl.ds(HALO - 1 + parity + k, half_rows, stride=2), :]
                    * cw_ref[k:k + 1, sl] for k in range(3)]
            parts.append(taps[0] + taps[1] + taps[2] + cb_ref[:, sl])
        return jnp.concatenate(parts, axis=-1)

    def up(chunk):
        c0 = chunk * FF_CHUNK
        for ab, col in enumerate((c0, D_FF + c0)):
            u = jnp.dot(n_ext, wup_ref[:, col:col + FF_CHUNK], preferred_element_type=F32)
            for jb in range(lane_blocks):
                u_refs[chunk][ab, jb] = u[:, jb * LANES:(jb + 1) * LANES]

    def gate(chunk, act_ref, col):
        c0 = chunk * FF_CHUNK
        for parity in range(2):
            a = conv(u_refs[chunk], 0, c0, parity)
            b = conv(u_refs[chunk], 1, D_FF + c0, parity)
            act_ref[parity * half_rows:(parity + 1) * half_rows, col:col + FF_CHUNK] = (
                (a * _sigmoid(a) * b).astype(BF16))

    def down(gi):
        grp = groups[gi]
        return jnp.dot(act_refs[gi][...], wdown_ref[grp[0] * FF_CHUNK:(grp[-1] + 1) * FF_CHUNK, :],
                       preferred_element_type=F32)

    assert sum(FF_GROUP_CHUNKS) == N_FF_CHUNKS
    first = [sum(FF_GROUP_CHUNKS[:g]) for g in range(len(FF_GROUP_CHUNKS))]
    groups = [range(f, f + n) for f, n in zip(first, FF_GROUP_CHUNKS)]
    y = None
    for step in range(len(groups) + 2):
        if step < len(groups):
            for chunk in groups[step]:
                up(chunk)
        if 2 <= step:
            part = down(step - 2)
            y = part if y is None else y + part
        if 1 <= step <= len(groups):
            for j, chunk in enumerate(groups[step - 1]):
                gate(chunk, act_refs[step - 1], j * FF_CHUNK)
    for parity in range(2):
        for jb in range(D_MODEL // LANES):
            y_ref[jb, pl.ds(parity, half_rows, stride=2), :] = (
                y[parity * half_rows:(parity + 1) * half_rows, jb * LANES:(jb + 1) * LANES])
    y_tok = jnp.concatenate([y_ref[jb] for jb in range(D_MODEL // LANES)], axis=-1)
    out_ref[...] = _rms(h_ref[...] + y_tok, gfin_ref[...])


def _ffn(h, g_ffn, w_up, conv_w, conv_b, w_down, g_final, seq):
    t = h.shape[0]
    tm = ROW_TILE
    tiles_per_seq = seq // tm
    halo_blocks = tm // HALO
    last_block = t // HALO - 1
    row = lambda i: (i, 0)
    return pl.pallas_call(
        functools.partial(_ffn_kernel, tiles_per_seq=tiles_per_seq),
        grid=(t // tm,),
        in_specs=[
            pl.BlockSpec((tm, D_MODEL), row),
            pl.BlockSpec((HALO, D_MODEL), lambda i: (jnp.maximum(i * halo_blocks - 1, 0), 0)),
            pl.BlockSpec((HALO, D_MODEL),
                         lambda i: (jnp.minimum((i + 1) * halo_blocks, last_block), 0)),
            _resident((1, D_MODEL)), _resident(w_up.shape), _resident(conv_w.shape),
            _resident(conv_b.shape), _resident(w_down.shape), _resident((1, D_MODEL)),
        ],
        out_specs=pl.BlockSpec((tm, D_MODEL), row),
        out_shape=jax.ShapeDtypeStruct((t, D_MODEL), F32),
        scratch_shapes=[*[pltpu.VMEM((2, FF_CHUNK // LANES, tm + 2 * HALO, LANES), F32)]
                        * N_FF_CHUNKS,
                        *[pltpu.VMEM((tm, n * FF_CHUNK), BF16) for n in FF_GROUP_CHUNKS],
                        pltpu.VMEM((D_MODEL // LANES, tm, LANES), F32)],
        compiler_params=_params(1),
        name="ffn",
    )(h, h, h, g_ffn, w_up, conv_w, conv_b, w_down, g_final)


def _rotary_tables(seq):
    inv = ROPE_THETA ** (-jnp.arange(0, HEAD_DIM, 2, dtype=F32) / HEAD_DIM)
    ang = jnp.arange(seq, dtype=F32)[:, None] * inv[None, :]
    cos, sin = jnp.cos(ang), jnp.sin(ang)
    reps = LANES // HEAD_DIM
    cos_t = jnp.tile(jnp.concatenate([cos, cos], axis=-1), (1, reps))
    sin_t = jnp.tile(jnp.concatenate([-sin, sin], axis=-1), (1, reps))
    return cos_t, sin_t


def _pair_lanes(v):
    pairs = v.reshape(RET_PAIRS, 2, 1)
    lanes = jnp.broadcast_to(pairs, (RET_PAIRS, 2, RET_QK_DIM)).reshape(RET_PAIRS, 1, LANES)
    return jnp.broadcast_to(lanes, (RET_PAIRS, SUBLANES, LANES)).astype(F32)


def _layer(h2, mem, batch, seq, g_mix, w_in, w_mem_kv, g_mem, decay_fwd, decay_bwd, g_ret,
           w_proj_attn, w_proj_ret, w_proj_mem, w_gate, b_gate, w_out):
    bf = lambda w: w.astype(BF16)
    row = lambda v: v.reshape(1, -1).astype(F32)
    cos_t, sin_t = _rotary_tables(seq)
    proj = _in_proj(h2, row(g_mix), cos_t, sin_t, bf(w_in), seq)
    qa, ka, va = proj[0:3], proj[3:6], proj[6:9]
    q_r, k_r, v_r, g_r, q_m = proj[9:]

    os_, lses = [], []
    for g, (_, dilation) in enumerate(ATTN_GROUPS):
        o, lse = _band_attn(qa[g], ka[g], va[g], batch, seq, dilation)
        os_.append(o)
        lses.append(lse)

    y_r = _retention(q_r, k_r, v_r, g_r, _pair_lanes(decay_fwd), _pair_lanes(decay_bwd),
                     row(g_ret), batch, seq)
    k_m, v_m = _mem_kv(mem, row(g_mem), bf(w_mem_kv))
    return _merge(h2, row(g_mix), os_, lses, y_r, q_m, k_m, v_m, bf(w_gate), row(b_gate),
                  bf(w_proj_attn), bf(w_proj_ret), bf(w_proj_mem), bf(w_out), seq)


def kernel(x, mem, g_mix, w_in, w_mem_kv, g_mem, ret_decay_fwd, ret_decay_bwd, g_ret,
           w_proj_attn, w_proj_ret, w_proj_mem, w_gate, b_gate, w_out,
           g_ffn, w_up, conv_w, conv_b, w_down, g_final):
    batch, seq, d = x.shape
    depth = w_in.shape[0]
    assert d == D_MODEL and depth == 1 and seq % ROW_TILE == 0 and seq % WIDE_ROW_TILE == 0
    h2 = x.reshape(batch * seq, d)
    l = 0
    h2 = _layer(h2, mem, batch, seq, g_mix[l], w_in[l], w_mem_kv[l], g_mem[l],
                ret_decay_fwd[l], ret_decay_bwd[l], g_ret[l], w_proj_attn[l], w_proj_ret[l],
                w_proj_mem[l], w_gate[l], b_gate[l], w_out[l])
    out = _ffn(h2, g_ffn[l].reshape(1, -1), w_up[l].astype(BF16), conv_w[l],
               conv_b[l].reshape(1, -1), w_down[l].astype(BF16), g_final.reshape(1, -1), seq)
    return out.reshape(batch, seq, d)
```

```python
import functools
import math

import jax
import jax.numpy as jnp
from jax import lax
from jax.experimental import pallas as pl
from jax.experimental.pallas import tpu as pltpu

D_MODEL = 1024
HEAD_DIM = 64
ATTN_GROUPS = ((128, 1), (512, 4), (2048, 16))
GROUP_WIDTH = 4 * HEAD_DIM
ATTN_WIDTH = 3 * GROUP_WIDTH
BAND_HALF = 64
RET_HEADS = 6
RET_QK_DIM = 64
RET_V_DIM = 128
RET_QK_WIDTH = RET_HEADS * RET_QK_DIM
RET_V_WIDTH = RET_HEADS * RET_V_DIM
MEM_HEADS = 4
MEM_HEAD_DIM = 128
MEM_WIDTH = MEM_HEADS * MEM_HEAD_DIM
D_FF = 2816
ROPE_THETA = 10000.0
EPS = 1e-6
NEG_INF = -1e30
LOG2_E = math.log2(math.e)

LANES = 128
SUBLANES = 8
MXU_WIDTH = 256
VMEM_LIMIT = 56 * 1024 * 1024

ROW_TILE = 512
WIDE_ROW_TILE = 1024
RET_CHUNK = 256
Q_BLOCK = 128
K_SPAN = Q_BLOCK + 2 * BAND_HALF
MAX_ROW_STRIDE = 4
STATE_UNROLL = 8
MIX_UNROLL = 4
ATTN_PAIRS_IN_FLIGHT = 16

BF16 = jnp.bfloat16
F32 = jnp.float32


def _params(n_grid_axes):
    return pltpu.CompilerParams(
        dimension_semantics=("arbitrary",) * n_grid_axes, vmem_limit_bytes=VMEM_LIMIT)


def _resident(shape):
    nd = len(shape)
    return pl.BlockSpec(shape, lambda *_: (0,) * nd, pipeline_mode=pl.Buffered(1))


def _rms(x, g):
    return x * lax.rsqrt(jnp.mean(x * x, axis=-1, keepdims=True) + EPS) * g


def _qk_lane_is_head0(lane):
    return lane < HEAD_DIM


def _sigmoid(x):
    return 0.5 * jnp.tanh(0.5 * x) + 0.5


_DILATIONS = tuple(d for _, d in ATTN_GROUPS)
RET_PAIRS = RET_HEADS // 2
_IN_SEGMENTS = (
    *[(GROUP_WIDTH, True, HEAD_DIM ** -0.5 * LOG2_E, d, 1) for d in _DILATIONS],
    *[(GROUP_WIDTH, True, 1.0, d, 1) for d in _DILATIONS],
    *[(GROUP_WIDTH, False, 1.0, d, 1) for d in _DILATIONS],
    (RET_QK_WIDTH, True, 1.0, 1, RET_PAIRS),
    (RET_QK_WIDTH, True, RET_QK_DIM ** -0.5, 1, RET_PAIRS),
    (RET_V_WIDTH, False, 1.0, 1, RET_PAIRS),
    (RET_V_WIDTH, False, 1.0, 1, RET_PAIRS),
    (MEM_WIDTH, False, 1.0, 1, 1),
)


def _in_proj_kernel(x_ref, g_ref, cos_ref, sin_ref, w_ref, *refs):
    out_refs, stage_ref, stage2_ref = refs[:-2], refs[-2], refs[-1]
    tm = x_ref.shape[0]
    n = _rms(x_ref[...], g_ref[...]).astype(BF16)
    cos = cos_ref[...]
    sin = sin_ref[...]
    lane = lax.broadcasted_iota(jnp.int32, cos.shape, 1)
    low_half = (lane & (HEAD_DIM // 2)) == 0

    def rotate(a):
        partner = jnp.where(low_half, pltpu.roll(a, LANES - HEAD_DIM // 2, 1),
                            pltpu.roll(a, HEAD_DIM // 2, 1))
        return a * cos + partner * sin

    col = 0
    for out_ref, (width, rotary, scale, dil, parts) in zip(out_refs, _IN_SEGMENTS):
        for c in range(0, width, MXU_WIDTH):
            cw = min(MXU_WIDTH, width - c)
            acc = jnp.dot(n, w_ref[:, col + c:col + c + cw], preferred_element_type=F32)
            for j in range(0, cw, LANES):
                a = acc[:, j:j + LANES]
                if rotary:
                    a = rotate(a)
                if scale != 1.0:
                    a = a * scale
                if parts > 1:
                    part, off = divmod(c + j, width // parts)
                    out_ref[part, :, off:off + LANES] = a.astype(BF16)
                elif dil == 1:
                    out_ref[:, c + j:c + j + LANES] = a.astype(BF16)
                else:
                    stage_ref[j // LANES] = a
            if dil > 1:
                assert width == cw == stage_ref.shape[0] * LANES
                for jb in range(width // LANES):
                    if dil > MAX_ROW_STRIDE:
                        assert dil == MAX_ROW_STRIDE * MAX_ROW_STRIDE
                        n1 = tm // MAX_ROW_STRIDE
                        for r1 in range(MAX_ROW_STRIDE):
                            stage2_ref[jb, r1 * n1:(r1 + 1) * n1, :] = (
                                stage_ref[jb, pl.ds(r1, n1, stride=MAX_ROW_STRIDE), :])
                        residue_rows = [
                            (r1 + MAX_ROW_STRIDE * r2,
                             stage2_ref[jb, pl.ds(r1 * n1 + r2, tm // dil, stride=MAX_ROW_STRIDE), :])
                            for r1 in range(MAX_ROW_STRIDE) for r2 in range(MAX_ROW_STRIDE)]
                    else:
                        residue_rows = [(r, stage_ref[jb, pl.ds(r, tm // dil, stride=dil), :])
                                        for r in range(dil)]
                    for r, rows in residue_rows:
                        out_ref[:, r * width + jb * LANES:r * width + (jb + 1) * LANES] = (
                            rows.astype(BF16))
        col += width


def _in_proj(x2, g_mix, cos_t, sin_t, w_in, seq):
    t = x2.shape[0]
    tm = WIDE_ROW_TILE
    tiles_per_seq = seq // tm
    row = lambda i: (i, 0)
    pos = lambda i: (i % tiles_per_seq, 0)
    out_shape, out_specs = [], []
    for (w, _, _, d, parts) in _IN_SEGMENTS:
        if parts > 1:
            out_shape.append(jax.ShapeDtypeStruct((parts, t, w // parts), BF16))
            out_specs.append(pl.BlockSpec((parts, tm, w // parts), lambda i: (0, i, 0)))
        else:
            out_shape.append(jax.ShapeDtypeStruct((t // d, d * w), BF16))
            out_specs.append(pl.BlockSpec((tm // d, d * w), row))
    return pl.pallas_call(
        _in_proj_kernel,
        grid=(t // tm,),
        in_specs=[
            pl.BlockSpec((tm, D_MODEL), row),
            _resident((1, D_MODEL)),
            pl.BlockSpec((tm, LANES), pos),
            pl.BlockSpec((tm, LANES), pos),
            _resident(w_in.shape),
        ],
        out_specs=out_specs,
        out_shape=out_shape,
        scratch_shapes=[pltpu.VMEM((GROUP_WIDTH // LANES, tm, LANES), F32)] * 2,
        compiler_params=_params(1),
        name="in_proj",
    )(x2, g_mix, cos_t, sin_t, w_in)


def _band_attn_kernel(q_ref, k_ref, v_ref, o_ref, lse_ref, *, seq_len, n_res):
    n_blocks = seq_len // Q_BLOCK
    qi = lax.broadcasted_iota(jnp.int32, (Q_BLOCK, K_SPAN), 0)
    kj = lax.broadcasted_iota(jnp.int32, (Q_BLOCK, K_SPAN), 1)
    rel = qi - kj
    lane = lax.broadcasted_iota(jnp.int32, (Q_BLOCK, LANES), 1)
    head0 = lane < HEAD_DIM
    qk_head0 = _qk_lane_is_head0(lane)

    def band_bias(q0, k0):
        return jnp.where(jnp.abs(rel + (q0 - k0)) <= BAND_HALF, 0.0, NEG_INF)

    def block(q0, k0, bias):
        for pair in range(n_res * 2):
            c0 = pair * LANES
            qp = q_ref[0, pl.ds(q0, Q_BLOCK), c0:c0 + LANES]
            kp = k_ref[0, pl.ds(k0, K_SPAN), c0:c0 + LANES]
            vp = v_ref[0, pl.ds(k0, K_SPAN), c0:c0 + LANES]
            v_ones = jnp.concatenate([vp, jnp.ones_like(vp)], axis=-1)
            outs = []
            for h in range(2):
                mask_h = qk_head0 if h == 0 else jnp.logical_not(qk_head0)
                qh = jnp.where(mask_h, qp, jnp.zeros_like(qp))
                s = lax.dot_general(qh, kp, (((1,), (1,)), ((), ())),
                                    preferred_element_type=F32) + bias
                m = jnp.max(s, axis=-1, keepdims=True)
                p = jnp.exp2(s - m).astype(BF16)
                ol = jnp.dot(p, v_ones, preferred_element_type=F32)
                o, l = ol[:, :LANES], ol[:, LANES:]
                outs.append((o / l, m + jnp.log2(l)))
            o_pair = jnp.where(head0, outs[0][0], outs[1][0])
            lse_pair = jnp.where(head0, outs[0][1], outs[1][1])
            o_ref[0, pl.ds(q0, Q_BLOCK), c0:c0 + LANES] = o_pair.astype(BF16)
            lse_ref[0, pl.ds(q0, Q_BLOCK), c0:c0 + LANES] = lse_pair

    last_q0 = seq_len - Q_BLOCK
    last_k0 = seq_len - K_SPAN
    block(0, 0, band_bias(0, 0))
    if n_blocks > 2:
        mid_bias = band_bias(BAND_HALF, 0)

        def interior(i, carry):
            q0 = pl.multiple_of(i * Q_BLOCK, Q_BLOCK)
            k0 = pl.multiple_of(i * Q_BLOCK - BAND_HALF, BAND_HALF)
            block(q0, k0, mid_bias)
            return carry

        unroll = max(1, ATTN_PAIRS_IN_FLIGHT // (2 * n_res))
        lax.fori_loop(1, n_blocks - 1, interior, 0, unroll=unroll)
    block(last_q0, last_k0, band_bias(last_q0, last_k0))


def _band_attn(q, k, v, batch, seq, dilation):
    seq_len = seq // dilation
    n_res = min(dilation, 4)
    width = dilation * GROUP_WIDTH
    view = lambda a: a.reshape(batch, seq_len, width)
    blk = (1, seq_len, n_res * GROUP_WIDTH)
    idx = lambda b, r: (b, 0, r)
    o, lse = pl.pallas_call(
        functools.partial(_band_attn_kernel, seq_len=seq_len, n_res=n_res),
        grid=(batch, dilation // n_res),
        in_specs=[pl.BlockSpec(blk, idx)] * 3,
        out_specs=[pl.BlockSpec(blk, idx)] * 2,
        out_shape=[jax.ShapeDtypeStruct((batch, seq_len, width), BF16),
                   jax.ShapeDtypeStruct((batch, seq_len, width), F32)],
        compiler_params=_params(2),
        name=f"band_attn_d{dilation}",
    )(view(q), view(k), view(v))
    return o.reshape(batch * seq_len, width), lse.reshape(batch * seq_len, width)


def _log_sigmoid(x):
    return jnp.minimum(x, 0.0) - jnp.log1p(jnp.exp(-jnp.abs(x)))


def _retention_kernel(decf_ref, decb_ref, q_ref, k_ref, v_ref, gate_ref, gret_ref, o_ref,
                      st_ref, s_ref, y_ref, zeta_ref, cdec_ref, xi_ref, din_ref, *, seq):
    c = RET_CHUNK
    n_chunks = seq // c
    pair_w = 2 * RET_QK_DIM
    dv = RET_V_DIM
    st_shape = (pair_w, 2 * RET_V_DIM)

    @pl.when(pl.program_id(1) == 0)
    def _():
        lg_f = _log_sigmoid(decf_ref[0])
        lg_b = _log_sigmoid(decb_ref[0])
        pos = lax.broadcasted_iota(jnp.int32, (c, pair_w), 0).astype(F32)
        zeta_ref[0] = jnp.exp(lg_f[0:1, :] * (c - 1.0 - pos))
        zeta_ref[1] = jnp.exp(lg_b[0:1, :] * pos)
        bcast = lambda lg, h, shape: jnp.broadcast_to(
            lg[0:1, h * RET_QK_DIM:h * RET_QK_DIM + 1], shape)
        row_is_h0 = _qk_lane_is_head0(lax.broadcasted_iota(jnp.int32, st_shape, 0))
        for d, lg in enumerate((lg_f, lg_b)):
            cdec_ref[d] = jnp.exp(
                jnp.where(row_is_h0, bcast(lg, 0, st_shape), bcast(lg, 1, st_shape)) * float(c))
        ci = lax.broadcasted_iota(jnp.int32, (c, c), 0)
        mi = lax.broadcasted_iota(jnp.int32, (c, c), 1)
        delta = (ci - mi).astype(F32)
        for h in range(2):
            xi_ref[h] = jnp.concatenate([jnp.exp(bcast(lg_f, h, (c, dv)) * (pos + 1.0)),
                                         jnp.exp(bcast(lg_b, h, (c, dv)) * (c - pos))], axis=-1)
            din_ref[h] = jnp.where(
                ci >= mi, jnp.exp(bcast(lg_f, h, (c, c)) * jnp.maximum(delta, 0.0)),
                jnp.exp(bcast(lg_b, h, (c, c)) * jnp.maximum(-delta, 0.0)))

    def row0(n):
        return n * c if isinstance(n, int) else pl.multiple_of(n * c, c)

    def chunk_kv(n, direction):
        r0 = row0(n)
        kz = (k_ref[0, pl.ds(r0, c), :].astype(F32) * zeta_ref[direction]).astype(BF16)
        return lax.dot_general(kz, v_ref[0, pl.ds(r0, c), :], (((0,), (0,)), ((), ())),
                               preferred_element_type=F32)

    lane = lax.broadcasted_iota(jnp.int32, (c, pair_w), 1)
    head0 = _qk_lane_is_head0(lane)

    def head_query(qp, h):
        mask_h = head0 if h == 0 else jnp.logical_not(head0)
        return jnp.where(mask_h, qp, jnp.zeros_like(qp))

    def score_chunk(n):
        r0 = row0(n)
        qp = q_ref[0, pl.ds(r0, c), :]
        kp = k_ref[0, pl.ds(r0, c), :]
        for h in range(2):
            s = lax.dot_general(head_query(qp, h), kp, (((1,), (1,)), ((), ())),
                                preferred_element_type=F32)
            s_ref[pl.ds(r0, c), h * c:(h + 1) * c] = (s * din_ref[h]).astype(BF16)

    def state_step(j, carry):
        st_f, st_b = carry
        jb = n_chunks - 1 - j
        score_chunk(j)
        for h in range(2):
            hs = slice(h * dv, (h + 1) * dv)
            st_ref[j, :, (2 * h) * dv:(2 * h + 1) * dv] = st_f[:, hs].astype(BF16)
            st_ref[jb, :, (2 * h + 1) * dv:(2 * h + 2) * dv] = st_b[:, hs].astype(BF16)
        st_f = st_f * cdec_ref[0] + chunk_kv(j, 0)
        st_b = st_b * cdec_ref[1] + chunk_kv(jb, 1)
        return st_f, st_b

    zero_state = jnp.zeros((pair_w, 2 * RET_V_DIM), F32)
    lax.fori_loop(0, n_chunks, state_step, (zero_state, zero_state), unroll=STATE_UNROLL)

    def mix_chunk(n):
        r0 = row0(n)
        qp = q_ref[0, pl.ds(r0, c), :]
        for h in range(2):
            vs = slice(h * dv, (h + 1) * dv)
            y = jnp.dot(s_ref[pl.ds(r0, c), h * c:(h + 1) * c], v_ref[0, pl.ds(r0, c), vs],
                        preferred_element_type=F32)
            cross = jnp.dot(head_query(qp, h), st_ref[n, :, 2 * h * dv:(2 * h + 2) * dv],
                            preferred_element_type=F32) * xi_ref[h]
            y_ref[pl.ds(r0, c), vs] = y + cross[:, :dv] + cross[:, dv:]

    def norm_chunk(n):
        r0 = row0(n)
        for h in range(2):
            vs = slice(h * dv, (h + 1) * dv)
            y = y_ref[pl.ds(r0, c), vs]
            mu = jnp.mean(y, axis=-1, keepdims=True)
            yc = y - mu
            var = jnp.mean(yc * yc, axis=-1, keepdims=True)
            yn = yc * lax.rsqrt(var + EPS) * gret_ref[:, vs]
            gate = gate_ref[0, pl.ds(r0, c), vs].astype(F32)
            o_ref[0, pl.ds(r0, c), vs] = (yn * (gate * _sigmoid(gate))).astype(BF16)

    n_batches = n_chunks // MIX_UNROLL

    def mix_norm(i, carry):
        for u in range(MIX_UNROLL):
            norm_chunk((i - 1) * MIX_UNROLL + u)
        for u in range(MIX_UNROLL):
            mix_chunk(i * MIX_UNROLL + u)
        return carry

    for u in range(MIX_UNROLL):
        mix_chunk(u)
    lax.fori_loop(1, n_batches, mix_norm, 0)
    for u in range(MIX_UNROLL):
        norm_chunk((n_batches - 1) * MIX_UNROLL + u)


def _retention(qr, kr, vr, gr, dec_f, dec_b, g_ret, batch, seq):
    pairs = RET_PAIRS
    n_chunks = seq // RET_CHUNK
    v3 = lambda a: a.reshape(pairs * batch, seq, a.shape[-1])
    slab = lambda p, b: (p * batch + b, 0, 0)
    qk_spec = pl.BlockSpec((1, seq, 2 * RET_QK_DIM), slab)
    v_spec = pl.BlockSpec((1, seq, 2 * RET_V_DIM), slab)
    dec_spec = pl.BlockSpec((1, SUBLANES, LANES), lambda p, b: (p, 0, 0))
    out = pl.pallas_call(
        functools.partial(_retention_kernel, seq=seq),
        grid=(pairs, batch),
        in_specs=[dec_spec, dec_spec, qk_spec, qk_spec, v_spec, v_spec,
                  pl.BlockSpec((1, 2 * RET_V_DIM), lambda p, b: (0, p))],
        out_specs=v_spec,
        out_shape=jax.ShapeDtypeStruct((pairs * batch, seq, 2 * RET_V_DIM), BF16),
        scratch_shapes=[pltpu.VMEM((n_chunks, 2 * RET_QK_DIM, 4 * RET_V_DIM), BF16),
                        pltpu.VMEM((seq, 2 * RET_CHUNK), BF16),
                        pltpu.VMEM((seq, 2 * RET_V_DIM), F32),
                        pltpu.VMEM((2, RET_CHUNK, 2 * RET_QK_DIM), F32),
                        pltpu.VMEM((2, 2 * RET_QK_DIM, 2 * RET_V_DIM), F32),
                        pltpu.VMEM((2, RET_CHUNK, 2 * RET_V_DIM), F32),
                        pltpu.VMEM((2, RET_CHUNK, RET_CHUNK), F32)],
        compiler_params=_params(2),
        name="retention",
    )(dec_f, dec_b, v3(qr), v3(kr), v3(vr), v3(gr), g_ret)
    return out.reshape(pairs, batch * seq, 2 * RET_V_DIM)


def _mem_kv_kernel(mem_ref, g_ref, w_ref, k_ref, v_ref):
    n = _rms(mem_ref[0], g_ref[...]).astype(BF16)
    kv = jnp.dot(n, w_ref[...], preferred_element_type=F32)
    k_ref[0] = kv[:, :MEM_WIDTH].astype(BF16)
    v_ref[0] = kv[:, MEM_WIDTH:].astype(BF16)


def _mem_kv(mem, g_mem, w_mem_kv):
    batch, mem_len, _ = mem.shape
    out = jax.ShapeDtypeStruct((batch, mem_len, MEM_WIDTH), BF16)
    spec = pl.BlockSpec((1, mem_len, MEM_WIDTH), lambda b: (b, 0, 0))
    return pl.pallas_call(
        _mem_kv_kernel,
        grid=(batch,),
        in_specs=[pl.BlockSpec((1, mem_len, D_MODEL), lambda b: (b, 0, 0)),
                  _resident((1, D_MODEL)), _resident(w_mem_kv.shape)],
        out_specs=[spec, spec],
        out_shape=[out, out],
        compiler_params=_params(1),
        name="mem_kv",
    )(mem, g_mem, w_mem_kv)


def _merge_kernel(x_ref, g_ref, o0_ref, o1_ref, o2_ref, l0_ref, l1_ref, l2_ref, yr_ref, qm_ref,
                  km_ref, vm_ref, wg_ref, bg_ref, wpa_ref, wpr_ref, wpm_ref, wo_ref, h_ref,
                  *stage_refs):
    x = x_ref[...]
    tm = x.shape[0]
    n = _rms(x, g_ref[...]).astype(BF16)
    def gate(i):
        return _sigmoid(jnp.dot(n, wg_ref[:, i * D_MODEL:(i + 1) * D_MODEL],
                                preferred_element_type=F32)
                        + bg_ref[:, i * D_MODEL:(i + 1) * D_MODEL])

    def token_rows(ref, dil, stages):
        if dil == 1:
            return ref[...].astype(F32)
        stage_ref = next(stages)
        n_lane_blocks = GROUP_WIDTH // LANES
        residue = lambda r, jb: ref[:, r * GROUP_WIDTH + jb * LANES:
                                    r * GROUP_WIDTH + (jb + 1) * LANES].astype(F32)
        if dil > MAX_ROW_STRIDE:
            assert dil == MAX_ROW_STRIDE * MAX_ROW_STRIDE
            stage2_ref = next(stages)
            n1 = tm // MAX_ROW_STRIDE
            for jb in range(n_lane_blocks):
                for r1 in range(MAX_ROW_STRIDE):
                    for r2 in range(MAX_ROW_STRIDE):
                        stage2_ref[jb, pl.ds(r1 * n1 + r2, tm // dil, stride=MAX_ROW_STRIDE), :] = (
                            residue(r1 + MAX_ROW_STRIDE * r2, jb))
                    stage_ref[jb, pl.ds(r1, n1, stride=MAX_ROW_STRIDE), :] = (
                        stage2_ref[jb, r1 * n1:(r1 + 1) * n1, :])
        else:
            for r in range(dil):
                for jb in range(n_lane_blocks):
                    stage_ref[jb, pl.ds(r, tm // dil, stride=dil), :] = residue(r, jb)
        return jnp.concatenate([stage_ref[jb] for jb in range(n_lane_blocks)], axis=-1)

    y_r = jnp.concatenate([yr_ref[p] for p in range(RET_PAIRS)], axis=-1)
    pr = jnp.dot(y_r, wpr_ref[...], preferred_element_type=F32)

    ym = []
    for hd in range(MEM_HEADS):
        cs = slice(hd * MEM_HEAD_DIM, (hd + 1) * MEM_HEAD_DIM)
        s = lax.dot_general(qm_ref[:, cs], km_ref[0, :, cs], (((1,), (1,)), ((), ())),
                            preferred_element_type=F32) * (MEM_HEAD_DIM ** -0.5 * LOG2_E)
        p = jnp.exp2(s - jnp.max(s, axis=-1, keepdims=True)).astype(BF16)
        vm = vm_ref[0, :, cs]
        ol = jnp.dot(p, jnp.concatenate([vm, jnp.ones_like(vm)], axis=-1),
                     preferred_element_type=F32)
        ym.append((ol[:, :MEM_HEAD_DIM] / ol[:, MEM_HEAD_DIM:]).astype(BF16))
    pm = jnp.dot(jnp.concatenate(ym, axis=-1), wpm_ref[...], preferred_element_type=F32)

    stages = iter(stage_refs)
    outs, lses = [], []
    for o_ref, l_ref, dil in zip((o0_ref, o1_ref, o2_ref), (l0_ref, l1_ref, l2_ref), _DILATIONS):
        outs.append(token_rows(o_ref, dil, stages))
        lses.append(token_rows(l_ref, dil, stages))
    top = jnp.maximum(jnp.maximum(lses[0], lses[1]), lses[2])
    es = [jnp.exp2(l - top) for l in lses]
    inv = 1.0 / (es[0] + es[1] + es[2])
    pa = None
    for g in range(len(outs)):
        y_g = (outs[g] * (es[g] * inv)).astype(BF16)
        part = jnp.dot(y_g, wpa_ref[g * GROUP_WIDTH:(g + 1) * GROUP_WIDTH, :],
                       preferred_element_type=F32)
        pa = part if pa is None else pa + part

    merged = (gate(0) * pa + gate(1) * pr + gate(2) * pm).astype(BF16)
    h_ref[...] = x + jnp.dot(merged, wo_ref[...], preferred_element_type=F32)


def _merge(x2, g_mix, os_, lses, y_r, q_m, k_m, v_m, w_gate, b_gate, w_pa, w_pr, w_pm, w_out,
           seq):
    t = x2.shape[0]
    tm = ROW_TILE
    tiles_per_seq = seq // tm
    row = lambda i: (i, 0)
    mem_len = k_m.shape[1]
    mem_spec = pl.BlockSpec((1, mem_len, MEM_WIDTH), lambda i: (i // tiles_per_seq, 0, 0))
    group_specs = [pl.BlockSpec((tm // d, d * GROUP_WIDTH), row) for d in _DILATIONS]
    n_stages = 2 * sum((d > 1) + (d > MAX_ROW_STRIDE) for d in _DILATIONS)
    return pl.pallas_call(
        _merge_kernel,
        grid=(t // tm,),
        in_specs=[
            pl.BlockSpec((tm, D_MODEL), row), _resident((1, D_MODEL)),
            *group_specs, *group_specs,
            pl.BlockSpec((RET_PAIRS, tm, 2 * RET_V_DIM), lambda i: (0, i, 0)),
            pl.BlockSpec((tm, MEM_WIDTH), row),
            mem_spec, mem_spec,
            _resident(w_gate.shape), _resident(b_gate.shape), _resident(w_pa.shape),
            _resident(w_pr.shape), _resident(w_pm.shape), _resident(w_out.shape),
        ],
        out_specs=pl.BlockSpec((tm, D_MODEL), row),
        out_shape=jax.ShapeDtypeStruct((t, D_MODEL), F32),
        scratch_shapes=[pltpu.VMEM((GROUP_WIDTH // LANES, tm, LANES), F32)] * n_stages,
        compiler_params=_params(1),
        name="merge",
    )(x2, g_mix, *os_, *lses, y_r, q_m, k_m, v_m, w_gate, b_gate, w_pa, w_pr, w_pm, w_out)


FF_CHUNK = MXU_WIDTH
FF_GROUP_CHUNKS = (4, 4, 3)
HALO = SUBLANES


N_FF_CHUNKS = D_FF // FF_CHUNK


def _ffn_kernel(h_ref, prev_ref, next_ref, gffn_ref, wup_ref, cw_ref, cb_ref, wdown_ref,
                gfin_ref, out_ref, *scratch, tiles_per_seq):
    u_refs, act_refs, y_ref = scratch[:N_FF_CHUNKS], scratch[N_FF_CHUNKS:-1], scratch[-1]
    i = pl.program_id(0)
    tm = h_ref.shape[0]
    g = gffn_ref[...]
    has_prev = (i % tiles_per_seq) != 0
    has_next = (i % tiles_per_seq) != tiles_per_seq - 1
    n_prev = jnp.where(has_prev, _rms(prev_ref[...], g), 0.0)
    n_next = jnp.where(has_next, _rms(next_ref[...], g), 0.0)
    n_ext = jnp.concatenate([n_prev, _rms(h_ref[...], g), n_next], axis=0).astype(BF16)

    half_rows = tm // 2
    lane_blocks = FF_CHUNK // LANES

    def conv(u_ref, ab, c0, parity):
        parts = []
        for jb in range(lane_blocks):
            sl = slice(c0 + jb * LANES, c0 + (jb + 1) * LANES)
            taps = [u_ref[ab, jb, pl.ds(HALO - 1 + parity + k, half_rows, stride=2), :]
                    * cw_ref[k:k + 1, sl] for k in range(3)]
            parts.append(taps[0] + taps[1] + taps[2] + cb_ref[:, sl])
        return jnp.concatenate(parts, axis=-1)

    def up(chunk):
        c0 = chunk * FF_CHUNK
        for ab, col in enumerate((c0, D_FF + c0)):
            u = jnp.dot(n_ext, wup_ref[:, col:col + FF_CHUNK], preferred_element_type=F32)
            for jb in range(lane_blocks):
                u_refs[chunk][ab, jb] = u[:, jb * LANES:(jb + 1) * LANES]

    def gate(chunk, act_ref, col):
        c0 = chunk * FF_CHUNK
        for parity in range(2):
            a = conv(u_refs[chunk], 0, c0, parity)
            b = conv(u_refs[chunk], 1, D_FF + c0, parity)
            act_ref[parity * half_rows:(parity + 1) * half_rows, col:col + FF_CHUNK] = (
                (a * _sigmoid(a) * b).astype(BF16))

    def down(gi):
        grp = groups[gi]
        return jnp.dot(act_refs[gi][...], wdown_ref[grp[0] * FF_CHUNK:(grp[-1] + 1) * FF_CHUNK, :],
                       preferred_element_type=F32)

    assert sum(FF_GROUP_CHUNKS) == N_FF_CHUNKS
    first = [sum(FF_GROUP_CHUNKS[:g]) for g in range(len(FF_GROUP_CHUNKS))]
    groups = [range(f, f + n) for f, n in zip(first, FF_GROUP_CHUNKS)]
    y = None
    for step in range(len(groups) + 2):
        if step < len(groups):
            for chunk in groups[step]:
                up(chunk)
        if 2 <= step:
            part = down(step - 2)
            y = part if y is None else y + part
        if 1 <= step <= len(groups):
            for j, chunk in enumerate(groups[step - 1]):
                gate(chunk, act_refs[step - 1], j * FF_CHUNK)
    for parity in range(2):
        for jb in range(D_MODEL // LANES):
            y_ref[jb, pl.ds(parity, half_rows, stride=2), :] = (
                y[parity * half_rows:(parity + 1) * half_rows, jb * LANES:(jb + 1) * LANES])
    y_tok = jnp.concatenate([y_ref[jb] for jb in range(D_MODEL // LANES)], axis=-1)
    out_ref[...] = _rms(h_ref[...] + y_tok, gfin_ref[...])


def _ffn(h, g_ffn, w_up, conv_w, conv_b, w_down, g_final, seq):
    t = h.shape[0]
    tm = ROW_TILE
    tiles_per_seq = seq // tm
    halo_blocks = tm // HALO
    last_block = t // HALO - 1
    row = lambda i: (i, 0)
    return pl.pallas_call(
        functools.partial(_ffn_kernel, tiles_per_seq=tiles_per_seq),
        grid=(t // tm,),
        in_specs=[
            pl.BlockSpec((tm, D_MODEL), row),
            pl.BlockSpec((HALO, D_MODEL), lambda i: (jnp.maximum(i * halo_blocks - 1, 0), 0)),
            pl.BlockSpec((HALO, D_MODEL),
                         lambda i: (jnp.minimum((i + 1) * halo_blocks, last_block), 0)),
            _resident((1, D_MODEL)), _resident(w_up.shape), _resident(conv_w.shape),
            _resident(conv_b.shape), _resident(w_down.shape), _resident((1, D_MODEL)),
        ],
        out_specs=pl.BlockSpec((tm, D_MODEL), row),
        out_shape=jax.ShapeDtypeStruct((t, D_MODEL), F32),
        scratch_shapes=[*[pltpu.VMEM((2, FF_CHUNK // LANES, tm + 2 * HALO, LANES), F32)]
                        * N_FF_CHUNKS,
                        *[pltpu.VMEM((tm, n * FF_CHUNK), BF16) for n in FF_GROUP_CHUNKS],
                        pltpu.VMEM((D_MODEL // LANES, tm, LANES), F32)],
        compiler_params=_params(1),
        name="ffn",
    )(h, h, h, g_ffn, w_up, conv_w, conv_b, w_down, g_final)


def _rotary_tables(seq):
    inv = ROPE_THETA ** (-jnp.arange(0, HEAD_DIM, 2, dtype=F32) / HEAD_DIM)
    ang = jnp.arange(seq, dtype=F32)[:, None] * inv[None, :]
    cos, sin = jnp.cos(ang), jnp.sin(ang)
    reps = LANES // HEAD_DIM
    cos_t = jnp.tile(jnp.concatenate([cos, cos], axis=-1), (1, reps))
    sin_t = jnp.tile(jnp.concatenate([-sin, sin], axis=-1), (1, reps))
    return cos_t, sin_t


def _pair_lanes(v):
    pairs = v.reshape(RET_PAIRS, 2, 1)
    lanes = jnp.broadcast_to(pairs, (RET_PAIRS, 2, RET_QK_DIM)).reshape(RET_PAIRS, 1, LANES)
    return jnp.broadcast_to(lanes, (RET_PAIRS, SUBLANES, LANES)).astype(F32)


def _layer(h2, mem, batch, seq, g_mix, w_in, w_mem_kv, g_mem, decay_fwd, decay_bwd, g_ret,
           w_proj_attn, w_proj_ret, w_proj_mem, w_gate, b_gate, w_out):
    bf = lambda w: w.astype(BF16)
    row = lambda v: v.reshape(1, -1).astype(F32)
    cos_t, sin_t = _rotary_tables(seq)
    proj = _in_proj(h2, row(g_mix), cos_t, sin_t, bf(w_in), seq)
    qa, ka, va = proj[0:3], proj[3:6], proj[6:9]
    q_r, k_r, v_r, g_r, q_m = proj[9:]

    os_, lses = [], []
    for g, (_, dilation) in enumerate(ATTN_GROUPS):
        o, lse = _band_attn(qa[g], ka[g], va[g], batch, seq, dilation)
        os_.append(o)
        lses.append(lse)

    y_r = _retention(q_r, k_r, v_r, g_r, _pair_lanes(decay_fwd), _pair_lanes(decay_bwd),
                     row(g_ret), batch, seq)
    k_m, v_m = _mem_kv(mem, row(g_mem), bf(w_mem_kv))
    return _merge(h2, row(g_mix), os_, lses, y_r, q_m, k_m, v_m, bf(w_gate), row(b_gate),
                  bf(w_proj_attn), bf(w_proj_ret), bf(w_proj_mem), bf(w_out), seq)


def kernel(x, mem, g_mix, w_in, w_mem_kv, g_mem, ret_decay_fwd, ret_decay_bwd, g_ret,
           w_proj_attn, w_proj_ret, w_proj_mem, w_gate, b_gate, w_out,
           g_ffn, w_up, conv_w, conv_b, w_down, g_final):
    batch, seq, d = x.shape
    depth = w_in.shape[0]
    assert d == D_MODEL and depth == 1 and seq % ROW_TILE == 0 and seq % WIDE_ROW_TILE == 0
    h2 = x.reshape(batch * seq, d)
    l = 0
    h2 = _layer(h2, mem, batch, seq, g_mix[l], w_in[l], w_mem_kv[l], g_mem[l],
                ret_decay_fwd[l], ret_decay_bwd[l], g_ret[l], w_proj_attn[l], w_proj_ret[l],
                w_proj_mem[l], w_gate[l], b_gate[l], w_out[l])
    out = _ffn(h2, g_ffn[l].reshape(1, -1), w_up[l].astype(BF16), conv_w[l],
               conv_b[l].reshape(1, -1), w_down[l].astype(BF16), g_final.reshape(1, -1), seq)
    return out.reshape(batch, seq, d)
```

```python
import functools
import math

import jax
import jax.numpy as jnp
from jax import lax
from jax.experimental import pallas as pl
from jax.experimental.pallas import tpu as pltpu

D_MODEL = 1024
HEAD_DIM = 64
ATTN_GROUPS = ((128, 1), (512, 4), (2048, 16))
GROUP_WIDTH = 4 * HEAD_DIM
ATTN_WIDTH = 3 * GROUP_WIDTH
BAND_HALF = 64
RET_HEADS = 6
RET_QK_DIM = 64
RET_V_DIM = 128
RET_QK_WIDTH = RET_HEADS * RET_QK_DIM
RET_V_WIDTH = RET_HEADS * RET_V_DIM
MEM_HEADS = 4
MEM_HEAD_DIM = 128
MEM_WIDTH = MEM_HEADS * MEM_HEAD_DIM
D_FF = 2816
ROPE_THETA = 10000.0
EPS = 1e-6
NEG_INF = -1e30
LOG2_E = math.log2(math.e)

LANES = 128
SUBLANES = 8
MXU_WIDTH = 256
VMEM_LIMIT = 56 * 1024 * 1024

ROW_TILE = 512
WIDE_ROW_TILE = 1024
RET_CHUNK = 256
Q_BLOCK = 128
K_SPAN = Q_BLOCK + 2 * BAND_HALF
MAX_ROW_STRIDE = 4
STATE_UNROLL = 16
MIX_UNROLL = 4
ATTN_PAIRS_IN_FLIGHT = 16

BF16 = jnp.bfloat16
F32 = jnp.float32


def _params(n_grid_axes):
    return pltpu.CompilerParams(
        dimension_semantics=("arbitrary",) * n_grid_axes, vmem_limit_bytes=VMEM_LIMIT)


def _resident(shape):
    nd = len(shape)
    return pl.BlockSpec(shape, lambda *_: (0,) * nd, pipeline_mode=pl.Buffered(1))


def _rms(x, g):
    return x * lax.rsqrt(jnp.mean(x * x, axis=-1, keepdims=True) + EPS) * g


def _qk_lane_is_head0(lane):
    return lane < HEAD_DIM


def _sigmoid(x):
    return 0.5 * jnp.tanh(0.5 * x) + 0.5


_DILATIONS = tuple(d for _, d in ATTN_GROUPS)
RET_PAIRS = RET_HEADS // 2
_IN_SEGMENTS = (
    *[(GROUP_WIDTH, True, HEAD_DIM ** -0.5 * LOG2_E, d, 1) for d in _DILATIONS],
    *[(GROUP_WIDTH, True, 1.0, d, 1) for d in _DILATIONS],
    *[(GROUP_WIDTH, False, 1.0, d, 1) for d in _DILATIONS],
    (RET_QK_WIDTH, True, 1.0, 1, RET_PAIRS),
    (RET_QK_WIDTH, True, RET_QK_DIM ** -0.5, 1, RET_PAIRS),
    (RET_V_WIDTH, False, 1.0, 1, RET_PAIRS),
    (RET_V_WIDTH, False, 1.0, 1, RET_PAIRS),
    (MEM_WIDTH, False, 1.0, 1, 1),
)


def _in_proj_kernel(x_ref, g_ref, cos_ref, sin_ref, w_ref, *refs):
    out_refs, stage_ref, stage2_ref = refs[:-2], refs[-2], refs[-1]
    tm = x_ref.shape[0]
    n = _rms(x_ref[...], g_ref[...]).astype(BF16)
    cos = cos_ref[...]
    sin = sin_ref[...]
    lane = lax.broadcasted_iota(jnp.int32, cos.shape, 1)
    low_half = (lane & (HEAD_DIM // 2)) == 0

    def rotate(a):
        partner = jnp.where(low_half, pltpu.roll(a, LANES - HEAD_DIM // 2, 1),
                            pltpu.roll(a, HEAD_DIM // 2, 1))
        return a * cos + partner * sin

    col = 0
    for out_ref, (width, rotary, scale, dil, parts) in zip(out_refs, _IN_SEGMENTS):
        for c in range(0, width, MXU_WIDTH):
            cw = min(MXU_WIDTH, width - c)
            acc = jnp.dot(n, w_ref[:, col + c:col + c + cw], preferred_element_type=F32)
            for j in range(0, cw, LANES):
                a = acc[:, j:j + LANES]
                if rotary:
                    a = rotate(a)
                if scale != 1.0:
                    a = a * scale
                if parts > 1:
                    part, off = divmod(c + j, width // parts)
                    out_ref[part, :, off:off + LANES] = a.astype(BF16)
                elif dil == 1:
                    out_ref[:, c + j:c + j + LANES] = a.astype(BF16)
                else:
                    stage_ref[j // LANES] = a
            if dil > 1:
                assert width == cw == stage_ref.shape[0] * LANES
                for jb in range(width // LANES):
                    if dil > MAX_ROW_STRIDE:
                        assert dil == MAX_ROW_STRIDE * MAX_ROW_STRIDE
                        n1 = tm // MAX_ROW_STRIDE
                        for r1 in range(MAX_ROW_STRIDE):
                            stage2_ref[jb, r1 * n1:(r1 + 1) * n1, :] = (
                                stage_ref[jb, pl.ds(r1, n1, stride=MAX_ROW_STRIDE), :])
                        residue_rows = [
                            (r1 + MAX_ROW_STRIDE * r2,
                             stage2_ref[jb, pl.ds(r1 * n1 + r2, tm // dil, stride=MAX_ROW_STRIDE), :])
                            for r1 in range(MAX_ROW_STRIDE) for r2 in range(MAX_ROW_STRIDE)]
                    else:
                        residue_rows = [(r, stage_ref[jb, pl.ds(r, tm // dil, stride=dil), :])
                                        for r in range(dil)]
                    for r, rows in residue_rows:
                        out_ref[:, r * width + jb * LANES:r * width + (jb + 1) * LANES] = (
                            rows.astype(BF16))
        col += width


def _in_proj(x2, g_mix, cos_t, sin_t, w_in, seq):
    t = x2.shape[0]
    tm = WIDE_ROW_TILE
    tiles_per_seq = seq // tm
    row = lambda i: (i, 0)
    pos = lambda i: (i % tiles_per_seq, 0)
    out_shape, out_specs = [], []
    for (w, _, _, d, parts) in _IN_SEGMENTS:
        if parts > 1:
            out_shape.append(jax.ShapeDtypeStruct((parts, t, w // parts), BF16))
            out_specs.append(pl.BlockSpec((parts, tm, w // parts), lambda i: (0, i, 0)))
        else:
            out_shape.append(jax.ShapeDtypeStruct((t // d, d * w), BF16))
            out_specs.append(pl.BlockSpec((tm // d, d * w), row))
    return pl.pallas_call(
        _in_proj_kernel,
        grid=(t // tm,),
        in_specs=[
            pl.BlockSpec((tm, D_MODEL), row),
            _resident((1, D_MODEL)),
            pl.BlockSpec((tm, LANES), pos),
            pl.BlockSpec((tm, LANES), pos),
            _resident(w_in.shape),
        ],
        out_specs=out_specs,
        out_shape=out_shape,
        scratch_shapes=[pltpu.VMEM((GROUP_WIDTH // LANES, tm, LANES), F32)] * 2,
        compiler_params=_params(1),
        name="in_proj",
    )(x2, g_mix, cos_t, sin_t, w_in)


def _band_attn_kernel(q_ref, k_ref, v_ref, o_ref, lse_ref, *, seq_len, n_res):
    n_blocks = seq_len // Q_BLOCK
    qi = lax.broadcasted_iota(jnp.int32, (Q_BLOCK, K_SPAN), 0)
    kj = lax.broadcasted_iota(jnp.int32, (Q_BLOCK, K_SPAN), 1)
    rel = qi - kj
    lane = lax.broadcasted_iota(jnp.int32, (Q_BLOCK, LANES), 1)
    head0 = lane < HEAD_DIM
    qk_head0 = _qk_lane_is_head0(lane)

    def band_bias(q0, k0):
        return jnp.where(jnp.abs(rel + (q0 - k0)) <= BAND_HALF, 0.0, NEG_INF)

    def block(q0, k0, bias):
        for pair in range(n_res * 2):
            c0 = pair * LANES
            qp = q_ref[0, pl.ds(q0, Q_BLOCK), c0:c0 + LANES]
            kp = k_ref[0, pl.ds(k0, K_SPAN), c0:c0 + LANES]
            vp = v_ref[0, pl.ds(k0, K_SPAN), c0:c0 + LANES]
            v_ones = jnp.concatenate([vp, jnp.ones_like(vp)], axis=-1)
            outs = []
            for h in range(2):
                mask_h = qk_head0 if h == 0 else jnp.logical_not(qk_head0)
                qh = jnp.where(mask_h, qp, jnp.zeros_like(qp))
                s = lax.dot_general(qh, kp, (((1,), (1,)), ((), ())),
                                    preferred_element_type=F32) + bias
                m = jnp.max(s, axis=-1, keepdims=True)
                p = jnp.exp2(s - m).astype(BF16)
                ol = jnp.dot(p, v_ones, preferred_element_type=F32)
                o, l = ol[:, :LANES], ol[:, LANES:]
                outs.append((o / l, m + jnp.log2(l)))
            o_pair = jnp.where(head0, outs[0][0], outs[1][0])
            lse_pair = jnp.where(head0, outs[0][1], outs[1][1])
            o_ref[0, pl.ds(q0, Q_BLOCK), c0:c0 + LANES] = o_pair.astype(BF16)
            lse_ref[0, pl.ds(q0, Q_BLOCK), c0:c0 + LANES] = lse_pair

    last_q0 = seq_len - Q_BLOCK
    last_k0 = seq_len - K_SPAN
    block(0, 0, band_bias(0, 0))
    if n_blocks > 2:
        mid_bias = band_bias(BAND_HALF, 0)

        def interior(i, carry):
            q0 = pl.multiple_of(i * Q_BLOCK, Q_BLOCK)
            k0 = pl.multiple_of(i * Q_BLOCK - BAND_HALF, BAND_HALF)
            block(q0, k0, mid_bias)
            return carry

        unroll = max(1, ATTN_PAIRS_IN_FLIGHT // (2 * n_res))
        lax.fori_loop(1, n_blocks - 1, interior, 0, unroll=unroll)
    block(last_q0, last_k0, band_bias(last_q0, last_k0))


def _band_attn(q, k, v, batch, seq, dilation):
    seq_len = seq // dilation
    n_res = min(dilation, 4)
    width = dilation * GROUP_WIDTH
    view = lambda a: a.reshape(batch, seq_len, width)
    blk = (1, seq_len, n_res * GROUP_WIDTH)
    idx = lambda b, r: (b, 0, r)
    o, lse = pl.pallas_call(
        functools.partial(_band_attn_kernel, seq_len=seq_len, n_res=n_res),
        grid=(batch, dilation // n_res),
        in_specs=[pl.BlockSpec(blk, idx)] * 3,
        out_specs=[pl.BlockSpec(blk, idx)] * 2,
        out_shape=[jax.ShapeDtypeStruct((batch, seq_len, width), BF16),
                   jax.ShapeDtypeStruct((batch, seq_len, width), F32)],
        compiler_params=_params(2),
        name=f"band_attn_d{dilation}",
    )(view(q), view(k), view(v))
    return o.reshape(batch * seq_len, width), lse.reshape(batch * seq_len, width)


def _log_sigmoid(x):
    return jnp.minimum(x, 0.0) - jnp.log1p(jnp.exp(-jnp.abs(x)))


def _retention_kernel(decf_ref, decb_ref, q_ref, k_ref, v_ref, gate_ref, gret_ref, o_ref,
                      st_ref, s_ref, y_ref, zeta_ref, cdec_ref, xi_ref, din_ref, *, seq):
    c = RET_CHUNK
    n_chunks = seq // c
    pair_w = 2 * RET_QK_DIM
    dv = RET_V_DIM
    st_shape = (pair_w, 2 * RET_V_DIM)

    @pl.when(pl.program_id(1) == 0)
    def _():
        lg_f = _log_sigmoid(decf_ref[0])
        lg_b = _log_sigmoid(decb_ref[0])
        pos = lax.broadcasted_iota(jnp.int32, (c, pair_w), 0).astype(F32)
        zeta_ref[0] = jnp.exp(lg_f[0:1, :] * (c - 1.0 - pos))
        zeta_ref[1] = jnp.exp(lg_b[0:1, :] * pos)
        bcast = lambda lg, h, shape: jnp.broadcast_to(
            lg[0:1, h * RET_QK_DIM:h * RET_QK_DIM + 1], shape)
        row_is_h0 = _qk_lane_is_head0(lax.broadcasted_iota(jnp.int32, st_shape, 0))
        for d, lg in enumerate((lg_f, lg_b)):
            cdec_ref[d] = jnp.exp(
                jnp.where(row_is_h0, bcast(lg, 0, st_shape), bcast(lg, 1, st_shape)) * float(c))
        ci = lax.broadcasted_iota(jnp.int32, (c, c), 0)
        mi = lax.broadcasted_iota(jnp.int32, (c, c), 1)
        delta = (ci - mi).astype(F32)
        for h in range(2):
            xi_ref[h] = jnp.concatenate([jnp.exp(bcast(lg_f, h, (c, dv)) * (pos + 1.0)),
                                         jnp.exp(bcast(lg_b, h, (c, dv)) * (c - pos))], axis=-1)
            din_ref[h] = jnp.where(
                ci >= mi, jnp.exp(bcast(lg_f, h, (c, c)) * jnp.maximum(delta, 0.0)),
                jnp.exp(bcast(lg_b, h, (c, c)) * jnp.maximum(-delta, 0.0)))

    def row0(n):
        return n * c if isinstance(n, int) else pl.multiple_of(n * c, c)

    def chunk_kv(n, direction):
        r0 = row0(n)
        kz = (k_ref[0, pl.ds(r0, c), :].astype(F32) * zeta_ref[direction]).astype(BF16)
        return lax.dot_general(kz, v_ref[0, pl.ds(r0, c), :], (((0,), (0,)), ((), ())),
                               preferred_element_type=F32)

    lane = lax.broadcasted_iota(jnp.int32, (c, pair_w), 1)
    head0 = _qk_lane_is_head0(lane)

    def head_query(qp, h):
        mask_h = head0 if h == 0 else jnp.logical_not(head0)
        return jnp.where(mask_h, qp, jnp.zeros_like(qp))

    def score_chunk(n):
        r0 = row0(n)
        qp = q_ref[0, pl.ds(r0, c), :]
        kp = k_ref[0, pl.ds(r0, c), :]
        for h in range(2):
            s = lax.dot_general(head_query(qp, h), kp, (((1,), (1,)), ((), ())),
                                preferred_element_type=F32)
            s_ref[pl.ds(r0, c), h * c:(h + 1) * c] = (s * din_ref[h]).astype(BF16)

    def state_step(j, carry):
        st_f, st_b = carry
        jb = n_chunks - 1 - j
        score_chunk(j)
        for h in range(2):
            hs = slice(h * dv, (h + 1) * dv)
            st_ref[j, :, (2 * h) * dv:(2 * h + 1) * dv] = st_f[:, hs].astype(BF16)
            st_ref[jb, :, (2 * h + 1) * dv:(2 * h + 2) * dv] = st_b[:, hs].astype(BF16)
        st_f = st_f * cdec_ref[0] + chunk_kv(j, 0)
        st_b = st_b * cdec_ref[1] + chunk_kv(jb, 1)
        return st_f, st_b

    zero_state = jnp.zeros((pair_w, 2 * RET_V_DIM), F32)
    lax.fori_loop(0, n_chunks, state_step, (zero_state, zero_state), unroll=STATE_UNROLL)

    def mix_chunk(n):
        r0 = row0(n)
        qp = q_ref[0, pl.ds(r0, c), :]
        for h in range(2):
            vs = slice(h * dv, (h + 1) * dv)
            y = jnp.dot(s_ref[pl.ds(r0, c), h * c:(h + 1) * c], v_ref[0, pl.ds(r0, c), vs],
                        preferred_element_type=F32)
            cross = jnp.dot(head_query(qp, h), st_ref[n, :, 2 * h * dv:(2 * h + 2) * dv],
                            preferred_element_type=F32) * xi_ref[h]
            y_ref[pl.ds(r0, c), vs] = y + cross[:, :dv] + cross[:, dv:]

    def norm_chunk(n):
        r0 = row0(n)
        for h in range(2):
            vs = slice(h * dv, (h + 1) * dv)
            y = y_ref[pl.ds(r0, c), vs]
            mu = jnp.mean(y, axis=-1, keepdims=True)
            yc = y - mu
            var = jnp.mean(yc * yc, axis=-1, keepdims=True)
            yn = yc * lax.rsqrt(var + EPS) * gret_ref[:, vs]
            gate = gate_ref[0, pl.ds(r0, c), vs].astype(F32)
            o_ref[0, pl.ds(r0, c), vs] = (yn * (gate * _sigmoid(gate))).astype(BF16)

    n_batches = n_chunks // MIX_UNROLL

    def mix_norm(i, carry):
        for u in range(MIX_UNROLL):
            norm_chunk((i - 1) * MIX_UNROLL + u)
        for u in range(MIX_UNROLL):
            mix_chunk(i * MIX_UNROLL + u)
        return carry

    for u in range(MIX_UNROLL):
        mix_chunk(u)
    lax.fori_loop(1, n_batches, mix_norm, 0)
    for u in range(MIX_UNROLL):
        norm_chunk((n_batches - 1) * MIX_UNROLL + u)


def _retention(qr, kr, vr, gr, dec_f, dec_b, g_ret, batch, seq):
    pairs = RET_PAIRS
    n_chunks = seq // RET_CHUNK
    v3 = lambda a: a.reshape(pairs * batch, seq, a.shape[-1])
    slab = lambda p, b: (p * batch + b, 0, 0)
    qk_spec = pl.BlockSpec((1, seq, 2 * RET_QK_DIM), slab)
    v_spec = pl.BlockSpec((1, seq, 2 * RET_V_DIM), slab)
    dec_spec = pl.BlockSpec((1, SUBLANES, LANES), lambda p, b: (p, 0, 0))
    out = pl.pallas_call(
        functools.partial(_retention_kernel, seq=seq),
        grid=(pairs, batch),
        in_specs=[dec_spec, dec_spec, qk_spec, qk_spec, v_spec, v_spec,
                  pl.BlockSpec((1, 2 * RET_V_DIM), lambda p, b: (0, p))],
        out_specs=v_spec,
        out_shape=jax.ShapeDtypeStruct((pairs * batch, seq, 2 * RET_V_DIM), BF16),
        scratch_shapes=[pltpu.VMEM((n_chunks, 2 * RET_QK_DIM, 4 * RET_V_DIM), BF16),
                        pltpu.VMEM((seq, 2 * RET_CHUNK), BF16),
                        pltpu.VMEM((seq, 2 * RET_V_DIM), F32),
                        pltpu.VMEM((2, RET_CHUNK, 2 * RET_QK_DIM), F32),
                        pltpu.VMEM((2, 2 * RET_QK_DIM, 2 * RET_V_DIM), F32),
                        pltpu.VMEM((2, RET_CHUNK, 2 * RET_V_DIM), F32),
                        pltpu.VMEM((2, RET_CHUNK, RET_CHUNK), F32)],
        compiler_params=_params(2),
        name="retention",
    )(dec_f, dec_b, v3(qr), v3(kr), v3(vr), v3(gr), g_ret)
    return out.reshape(pairs, batch * seq, 2 * RET_V_DIM)


def _mem_kv_kernel(mem_ref, g_ref, w_ref, k_ref, v_ref):
    n = _rms(mem_ref[0], g_ref[...]).astype(BF16)
    kv = jnp.dot(n, w_ref[...], preferred_element_type=F32)
    k_ref[0] = kv[:, :MEM_WIDTH].astype(BF16)
    v_ref[0] = kv[:, MEM_WIDTH:].astype(BF16)


def _mem_kv(mem, g_mem, w_mem_kv):
    batch, mem_len, _ = mem.shape
    out = jax.ShapeDtypeStruct((batch, mem_len, MEM_WIDTH), BF16)
    spec = pl.BlockSpec((1, mem_len, MEM_WIDTH), lambda b: (b, 0, 0))
    return pl.pallas_call(
        _mem_kv_kernel,
        grid=(batch,),
        in_specs=[pl.BlockSpec((1, mem_len, D_MODEL), lambda b: (b, 0, 0)),
                  _resident((1, D_MODEL)), _resident(w_mem_kv.shape)],
        out_specs=[spec, spec],
        out_shape=[out, out],
        compiler_params=_params(1),
        name="mem_kv",
    )(mem, g_mem, w_mem_kv)


def _merge_kernel(x_ref, g_ref, o0_ref, o1_ref, o2_ref, l0_ref, l1_ref, l2_ref, yr_ref, qm_ref,
                  km_ref, vm_ref, wg_ref, bg_ref, wpa_ref, wpr_ref, wpm_ref, wo_ref, h_ref,
                  *stage_refs):
    x = x_ref[...]
    tm = x.shape[0]
    n = _rms(x, g_ref[...]).astype(BF16)
    def gate(i):
        return _sigmoid(jnp.dot(n, wg_ref[:, i * D_MODEL:(i + 1) * D_MODEL],
                                preferred_element_type=F32)
                        + bg_ref[:, i * D_MODEL:(i + 1) * D_MODEL])

    def token_rows(ref, dil, stages):
        if dil == 1:
            return ref[...].astype(F32)
        stage_ref = next(stages)
        n_lane_blocks = GROUP_WIDTH // LANES
        residue = lambda r, jb: ref[:, r * GROUP_WIDTH + jb * LANES:
                                    r * GROUP_WIDTH + (jb + 1) * LANES].astype(F32)
        if dil > MAX_ROW_STRIDE:
            assert dil == MAX_ROW_STRIDE * MAX_ROW_STRIDE
            stage2_ref = next(stages)
            n1 = tm // MAX_ROW_STRIDE
            for jb in range(n_lane_blocks):
                for r1 in range(MAX_ROW_STRIDE):
                    for r2 in range(MAX_ROW_STRIDE):
                        stage2_ref[jb, pl.ds(r1 * n1 + r2, tm // dil, stride=MAX_ROW_STRIDE), :] = (
                            residue(r1 + MAX_ROW_STRIDE * r2, jb))
                    stage_ref[jb, pl.ds(r1, n1, stride=MAX_ROW_STRIDE), :] = (
                        stage2_ref[jb, r1 * n1:(r1 + 1) * n1, :])
        else:
            for r in range(dil):
                for jb in range(n_lane_blocks):
                    stage_ref[jb, pl.ds(r, tm // dil, stride=dil), :] = residue(r, jb)
        return jnp.concatenate([stage_ref[jb] for jb in range(n_lane_blocks)], axis=-1)

    y_r = jnp.concatenate([yr_ref[p] for p in range(RET_PAIRS)], axis=-1)
    pr = jnp.dot(y_r, wpr_ref[...], preferred_element_type=F32)

    ym = []
    for hd in range(MEM_HEADS):
        cs = slice(hd * MEM_HEAD_DIM, (hd + 1) * MEM_HEAD_DIM)
        s = lax.dot_general(qm_ref[:, cs], km_ref[0, :, cs], (((1,), (1,)), ((), ())),
                            preferred_element_type=F32) * (MEM_HEAD_DIM ** -0.5 * LOG2_E)
        p = jnp.exp2(s - jnp.max(s, axis=-1, keepdims=True)).astype(BF16)
        vm = vm_ref[0, :, cs]
        ol = jnp.dot(p, jnp.concatenate([vm, jnp.ones_like(vm)], axis=-1),
                     preferred_element_type=F32)
        ym.append((ol[:, :MEM_HEAD_DIM] / ol[:, MEM_HEAD_DIM:]).astype(BF16))
    pm = jnp.dot(jnp.concatenate(ym, axis=-1), wpm_ref[...], preferred_element_type=F32)

    stages = iter(stage_refs)
    outs, lses = [], []
    for o_ref, l_ref, dil in zip((o0_ref, o1_ref, o2_ref), (l0_ref, l1_ref, l2_ref), _DILATIONS):
        outs.append(token_rows(o_ref, dil, stages))
        lses.append(token_rows(l_ref, dil, stages))
    top = jnp.maximum(jnp.maximum(lses[0], lses[1]), lses[2])
    es = [jnp.exp2(l - top) for l in lses]
    inv = 1.0 / (es[0] + es[1] + es[2])
    pa = None
    for g in range(len(outs)):
        y_g = (outs[g] * (es[g] * inv)).astype(BF16)
        part = jnp.dot(y_g, wpa_ref[g * GROUP_WIDTH:(g + 1) * GROUP_WIDTH, :],
                       preferred_element_type=F32)
        pa = part if pa is None else pa + part

    merged = (gate(0) * pa + gate(1) * pr + gate(2) * pm).astype(BF16)
    h_ref[...] = x + jnp.dot(merged, wo_ref[...], preferred_element_type=F32)


def _merge(x2, g_mix, os_, lses, y_r, q_m, k_m, v_m, w_gate, b_gate, w_pa, w_pr, w_pm, w_out,
           seq):
    t = x2.shape[0]
    tm = ROW_TILE
    tiles_per_seq = seq // tm
    row = lambda i: (i, 0)
    mem_len = k_m.shape[1]
    mem_spec = pl.BlockSpec((1, mem_len, MEM_WIDTH), lambda i: (i // tiles_per_seq, 0, 0))
    group_specs = [pl.BlockSpec((tm // d, d * GROUP_WIDTH), row) for d in _DILATIONS]
    n_stages = 2 * sum((d > 1) + (d > MAX_ROW_STRIDE) for d in _DILATIONS)
    return pl.pallas_call(
        _merge_kernel,
        grid=(t // tm,),
        in_specs=[
            pl.BlockSpec((tm, D_MODEL), row), _resident((1, D_MODEL)),
            *group_specs, *group_specs,
            pl.BlockSpec((RET_PAIRS, tm, 2 * RET_V_DIM), lambda i: (0, i, 0)),
            pl.BlockSpec((tm, MEM_WIDTH), row),
            mem_spec, mem_spec,
            _resident(w_gate.shape), _resident(b_gate.shape), _resident(w_pa.shape),
            _resident(w_pr.shape), _resident(w_pm.shape), _resident(w_out.shape),
        ],
        out_specs=pl.BlockSpec((tm, D_MODEL), row),
        out_shape=jax.ShapeDtypeStruct((t, D_MODEL), F32),
        scratch_shapes=[pltpu.VMEM((GROUP_WIDTH // LANES, tm, LANES), F32)] * n_stages,
        compiler_params=_params(1),
        name="merge",
    )(x2, g_mix, *os_, *lses, y_r, q_m, k_m, v_m, w_gate, b_gate, w_pa, w_pr, w_pm, w_out)


FF_CHUNK = MXU_WIDTH
FF_GROUP_CHUNKS = (4, 4, 3)
HALO = SUBLANES


N_FF_CHUNKS = D_FF // FF_CHUNK


def _ffn_kernel(h_ref, prev_ref, next_ref, gffn_ref, wup_ref, cw_ref, cb_ref, wdown_ref,
                gfin_ref, out_ref, *scratch, tiles_per_seq):
    u_refs, act_refs, y_ref = scratch[:N_FF_CHUNKS], scratch[N_FF_CHUNKS:-1], scratch[-1]
    i = pl.program_id(0)
    tm = h_ref.shape[0]
    g = gffn_ref[...]
    has_prev = (i % tiles_per_seq) != 0
    has_next = (i % tiles_per_seq) != tiles_per_seq - 1
    n_prev = jnp.where(has_prev, _rms(prev_ref[...], g), 0.0)
    n_next = jnp.where(has_next, _rms(next_ref[...], g), 0.0)
    n_ext = jnp.concatenate([n_prev, _rms(h_ref[...], g), n_next], axis=0).astype(BF16)

    half_rows = tm // 2
    lane_blocks = FF_CHUNK // LANES

    def conv(u_ref, ab, c0, parity):
        parts = []
        for jb in range(lane_blocks):
            sl = slice(c0 + jb * LANES, c0 + (jb + 1) * LANES)
            taps = [u_ref[ab, jb, pl.ds(HALO - 1 + parity + k, half_rows, stride=2), :]
                    * cw_ref[k:k + 1, sl] for k in range(3)]
            parts.append(taps[0] + taps[1] + taps[2] + cb_ref[:, sl])
        return jnp.concatenate(parts, axis=-1)

    def up(chunk):
        c0 = chunk * FF_CHUNK
        for ab, col in enumerate((c0, D_FF + c0)):
            u = jnp.dot(n_ext, wup_ref[:, col:col + FF_CHUNK], preferred_element_type=F32)
            for jb in range(lane_blocks):
                u_refs[chunk][ab, jb] = u[:, jb * LANES:(jb + 1) * LANES]

    def gate(chunk, act_ref, col):
        c0 = chunk * FF_CHUNK
        for parity in range(2):
            a = conv(u_refs[chunk], 0, c0, parity)
            b = conv(u_refs[chunk], 1, D_FF + c0, parity)
            act_ref[parity * half_rows:(parity + 1) * half_rows, col:col + FF_CHUNK] = (
                (a * _sigmoid(a) * b).astype(BF16))

    def down(gi):
        grp = groups[gi]
        return jnp.dot(act_refs[gi][...], wdown_ref[grp[0] * FF_CHUNK:(grp[-1] + 1) * FF_CHUNK, :],
                       preferred_element_type=F32)

    assert sum(FF_GROUP_CHUNKS) == N_FF_CHUNKS
    first = [sum(FF_GROUP_CHUNKS[:g]) for g in range(len(FF_GROUP_CHUNKS))]
    groups = [range(f, f + n) for f, n in zip(first, FF_GROUP_CHUNKS)]
    y = None
    for step in range(len(groups) + 2):
        if step < len(groups):
            for chunk in groups[step]:
                up(chunk)
        if 2 <= step:
            part = down(step - 2)
            y = part if y is None else y + part
        if 1 <= step <= len(groups):
            for j, chunk in enumerate(groups[step - 1]):
                gate(chunk, act_refs[step - 1], j * FF_CHUNK)
    for parity in range(2):
        for jb in range(D_MODEL // LANES):
            y_ref[jb, pl.ds(parity, half_rows, stride=2), :] = (
                y[parity * half_rows:(parity + 1) * half_rows, jb * LANES:(jb + 1) * LANES])
    y_tok = jnp.concatenate([y_ref[jb] for jb in range(D_MODEL // LANES)], axis=-1)
    out_ref[...] = _rms(h_ref[...] + y_tok, gfin_ref[...])


def _ffn(h, g_ffn, w_up, conv_w, conv_b, w_down, g_final, seq):
    t = h.shape[0]
    tm = ROW_TILE
    tiles_per_seq = seq // tm
    halo_blocks = tm // HALO
    last_block = t // HALO - 1
    row = lambda i: (i, 0)
    return pl.pallas_call(
        functools.partial(_ffn_kernel, tiles_per_seq=tiles_per_seq),
        grid=(t // tm,),
        in_specs=[
            pl.BlockSpec((tm, D_MODEL), row),
            pl.BlockSpec((HALO, D_MODEL), lambda i: (jnp.maximum(i * halo_blocks - 1, 0), 0)),
            pl.BlockSpec((HALO, D_MODEL),
                         lambda i: (jnp.minimum((i + 1) * halo_blocks, last_block), 0)),
            _resident((1, D_MODEL)), _resident(w_up.shape), _resident(conv_w.shape),
            _resident(conv_b.shape), _resident(w_down.shape), _resident((1, D_MODEL)),
        ],
        out_specs=pl.BlockSpec((tm, D_MODEL), row),
        out_shape=jax.ShapeDtypeStruct((t, D_MODEL), F32),
        scratch_shapes=[*[pltpu.VMEM((2, FF_CHUNK // LANES, tm + 2 * HALO, LANES), F32)]
                        * N_FF_CHUNKS,
                        *[pltpu.VMEM((tm, n * FF_CHUNK), BF16) for n in FF_GROUP_CHUNKS],
                        pltpu.VMEM((D_MODEL // LANES, tm, LANES), F32)],
        compiler_params=_params(1),
        name="ffn",
    )(h, h, h, g_ffn, w_up, conv_w, conv_b, w_down, g_final)


def _rotary_tables(seq):
    inv = ROPE_THETA ** (-jnp.arange(0, HEAD_DIM, 2, dtype=F32) / HEAD_DIM)
    ang = jnp.arange(seq, dtype=F32)[:, None] * inv[None, :]
    cos, sin = jnp.cos(ang), jnp.sin(ang)
    reps = LANES // HEAD_DIM
    cos_t = jnp.tile(jnp.concatenate([cos, cos], axis=-1), (1, reps))
    sin_t = jnp.tile(jnp.concatenate([-sin, sin], axis=-1), (1, reps))
    return cos_t, sin_t


def _pair_lanes(v):
    pairs = v.reshape(RET_PAIRS, 2, 1)
    lanes = jnp.broadcast_to(pairs, (RET_PAIRS, 2, RET_QK_DIM)).reshape(RET_PAIRS, 1, LANES)
    return jnp.broadcast_to(lanes, (RET_PAIRS, SUBLANES, LANES)).astype(F32)


def _layer(h2, mem, batch, seq, g_mix, w_in, w_mem_kv, g_mem, decay_fwd, decay_bwd, g_ret,
           w_proj_attn, w_proj_ret, w_proj_mem, w_gate, b_gate, w_out):
    bf = lambda w: w.astype(BF16)
    row = lambda v: v.reshape(1, -1).astype(F32)
    cos_t, sin_t = _rotary_tables(seq)
    proj = _in_proj(h2, row(g_mix), cos_t, sin_t, bf(w_in), seq)
    qa, ka, va = proj[0:3], proj[3:6], proj[6:9]
    q_r, k_r, v_r, g_r, q_m = proj[9:]

    os_, lses = [], []
    for g, (_, dilation) in enumerate(ATTN_GROUPS):
        o, lse = _band_attn(qa[g], ka[g], va[g], batch, seq, dilation)
        os_.append(o)
        lses.append(lse)

    y_r = _retention(q_r, k_r, v_r, g_r, _pair_lanes(decay_fwd), _pair_lanes(decay_bwd),
                     row(g_ret), batch, seq)
    k_m, v_m = _mem_kv(mem, row(g_mem), bf(w_mem_kv))
    return _merge(h2, row(g_mix), os_, lses, y_r, q_m, k_m, v_m, bf(w_gate), row(b_gate),
                  bf(w_proj_attn), bf(w_proj_ret), bf(w_proj_mem), bf(w_out), seq)


def kernel(x, mem, g_mix, w_in, w_mem_kv, g_mem, ret_decay_fwd, ret_decay_bwd, g_ret,
           w_proj_attn, w_proj_ret, w_proj_mem, w_gate, b_gate, w_out,
           g_ffn, w_up, conv_w, conv_b, w_down, g_final):
    batch, seq, d = x.shape
    depth = w_in.shape[0]
    assert d == D_MODEL and depth == 1 and seq % ROW_TILE == 0 and seq % WIDE_ROW_TILE == 0
    h2 = x.reshape(batch * seq, d)
    l = 0
    h2 = _layer(h2, mem, batch, seq, g_mix[l], w_in[l], w_mem_kv[l], g_mem[l],
                ret_decay_fwd[l], ret_decay_bwd[l], g_ret[l], w_proj_attn[l], w_proj_ret[l],
                w_proj_mem[l], w_gate[l], b_gate[l], w_out[l])
    out = _ffn(h2, g_ffn[l].reshape(1, -1), w_up[l].astype(BF16), conv_w[l],
               conv_b[l].reshape(1, -1), w_down[l].astype(BF16), g_final.reshape(1, -1), seq)
    return out.reshape(batch, seq, d)
```
